```python
import math
import jax, jax.numpy as jnp
from jax import lax
import numpy as np

D_MODEL = 1024
BATCH = 16
SEQ = 2048
DEPTH = 4

N_META = 16
D_FF = 2816
REC_WIDTH = 512
REC_BLOCKS = 8
REC_BLOCK_DIM = REC_WIDTH // REC_BLOCKS
CONV_WIDTH = 4
LRU_C = 8.0
N_Q_HEADS = 8
N_KV_HEADS = 2
Q_PER_KV = N_Q_HEADS // N_KV_HEADS
HEAD_DIM = 64
ATTN_WIDTH = N_Q_HEADS * HEAD_DIM
KV_WIDTH = N_KV_HEADS * HEAD_DIM
WINDOW = 128
BLOCK = 128
ROPE_DIM = HEAD_DIM // 4
ROPE_THETA = 500000.0
D_IN = 2 * REC_WIDTH + ATTN_WIDTH + 2 * KV_WIDTH
D_MIX = REC_WIDTH + ATTN_WIDTH
DEEPNORM_ALPHA = (2.0 * DEPTH) ** 0.25
DEEPNORM_BETA = (8.0 * DEPTH) ** -0.25
LN_EPS = 1e-5
RMS_EPS = 1e-6
NEG_INF = -1e30

kernel_name = "hybrid_rglru_swa_sink_macaron_deepnorm"


def layer_norm(x, g, b):
    xf = x.astype(jnp.float32)
    mu = jnp.mean(xf, axis=-1, keepdims=True)
    var = jnp.mean(jnp.square(xf - mu), axis=-1, keepdims=True)
    return ((xf - mu) * lax.rsqrt(var + LN_EPS) * g.astype(jnp.float32) + b.astype(jnp.float32)).astype(x.dtype)


def rms_norm(x, g):
    xf = x.astype(jnp.float32)
    ms = jnp.mean(jnp.square(xf), axis=-1, keepdims=True)
    return (xf * lax.rsqrt(ms + RMS_EPS) * g.astype(jnp.float32)).astype(x.dtype)


def swiglu(x, w_gate, w_up, w_down):
    return (jax.nn.silu(x @ w_gate) * (x @ w_up)) @ w_down


def causal_depthwise_conv(x, w, b):
    L = x.shape[1]
    xp = jnp.pad(x, ((0, 0), (CONV_WIDTH - 1, 0), (0, 0)))
    y = b + xp[:, 0:L] * w[0]
    for k in range(1, CONV_WIDTH):
        y = y + xp[:, k:k + L] * w[k]
    return y


def rglru(x, wa, ba, wx, bx, lam):
    B, L, _ = x.shape
    xb = x.reshape(B, L, REC_BLOCKS, REC_BLOCK_DIM)
    r = jax.nn.sigmoid(jnp.einsum('blhi,hij->blhj', xb, wa).reshape(B, L, REC_WIDTH) + ba).astype(jnp.float32)
    i = jax.nn.sigmoid(jnp.einsum('blhi,hij->blhj', xb, wx).reshape(B, L, REC_WIDTH) + bx)
    log_a = -LRU_C * jax.nn.softplus(-lam.astype(jnp.float32)) * r
    a = jnp.exp(log_a)
    gated_x = jnp.sqrt(-jnp.expm1(2.0 * log_a)) * (i * x).astype(jnp.float32)

    def combine(c1, c2):
        a1, b1 = c1
        a2, b2 = c2
        return a1 * a2, a2 * b1 + b2

    _, h = lax.associative_scan(combine, (a, gated_x), axis=1)
    return h.astype(x.dtype)


def rope_tables(L):
    pos = jnp.arange(L, dtype=jnp.float32)
    inv_freq = ROPE_THETA ** (-jnp.arange(0, ROPE_DIM, 2, dtype=jnp.float32) / ROPE_DIM)
    ang = pos[:, None] * inv_freq[None, :]
    return jnp.cos(ang), jnp.sin(ang)


def apply_partial_rope(t, cos, sin):
    tf = t.astype(jnp.float32)
    x1 = tf[..., :ROPE_DIM // 2]
    x2 = tf[..., ROPE_DIM // 2:ROPE_DIM]
    c = cos[None, :, None, :]
    s = sin[None, :, None, :]
    out = jnp.concatenate([x1 * c - x2 * s, x2 * c + x1 * s, tf[..., ROPE_DIM:]], axis=-1)
    return out.astype(t.dtype)


def swa_sink_attention(q, k, v, sinks):
    B, L = q.shape[0], q.shape[1]
    pad = (-L) % BLOCK
    nb = (L + pad) // BLOCK
    padl = ((0, 0), (pad, 0), (0, 0), (0, 0))
    qb = jnp.pad(q, padl).reshape(B, nb, BLOCK, N_KV_HEADS, Q_PER_KV, HEAD_DIM)
    kb = jnp.pad(k, padl).reshape(B, nb, BLOCK, N_KV_HEADS, HEAD_DIM)
    vb = jnp.pad(v, padl).reshape(B, nb, BLOCK, N_KV_HEADS, HEAD_DIM)
    prev = ((0, 0), (1, 0), (0, 0), (0, 0), (0, 0))
    k_band = jnp.concatenate([jnp.pad(kb, prev)[:, :-1], kb], axis=2)
    v_band = jnp.concatenate([jnp.pad(vb, prev)[:, :-1], vb], axis=2)
    k_meta = k[:, :N_META]
    v_meta = v[:, :N_META]
    scale = HEAD_DIM ** -0.5
    s_band = jnp.einsum('bnqgrd,bnkgd->bngrqk', qb, k_band).astype(jnp.float32) * scale
    s_meta = jnp.einsum('bnqgrd,bmgd->bngrqm', qb, k_meta).astype(jnp.float32) * scale
    q_pos = jnp.arange(nb)[:, None] * BLOCK + jnp.arange(BLOCK)[None, :] - pad
    k_pos = (jnp.arange(nb)[:, None] - 1) * BLOCK + jnp.arange(2 * BLOCK)[None, :] - pad
    dist = q_pos[:, :, None] - k_pos[:, None, :]
    band_mask = (dist >= 0) & (dist < WINDOW) & (k_pos[:, None, :] >= N_META)
    meta_mask = jnp.arange(N_META)[None, None, :] <= q_pos[:, :, None]
    s_band = jnp.where(band_mask[None, :, None, None], s_band, NEG_INF)
    s_meta = jnp.where(meta_mask[None, :, None, None], s_meta, NEG_INF)
    sink = jnp.broadcast_to(sinks.astype(jnp.float32).reshape(1, 1, N_KV_HEADS, Q_PER_KV, 1, 1),
                            s_meta.shape[:-1] + (1,))
    probs = jax.nn.softmax(jnp.concatenate([sink, s_meta, s_band], axis=-1), axis=-1)
    p_meta = probs[..., 1:1 + N_META].astype(v.dtype)
    p_band = probs[..., 1 + N_META:].astype(v.dtype)
    o = (jnp.einsum('bngrqm,bmgd->bnqgrd', p_meta, v_meta)
         + jnp.einsum('bngrqk,bnkgd->bnqgrd', p_band, v_band))
    return o.reshape(B, nb * BLOCK, ATTN_WIDTH)[:, pad:]


def hybrid_mixer(h, w_in, conv_w, conv_b, ga_w, ga_b, gx_w, gx_b, lam, sinks, g_rec, g_attn, w_out, cos, sin):
    B, L, _ = h.shape
    proj = h @ w_in
    x_rec, gate, q, k, v = jnp.split(
        proj, [REC_WIDTH, 2 * REC_WIDTH, 2 * REC_WIDTH + ATTN_WIDTH, 2 * REC_WIDTH + ATTN_WIDTH + KV_WIDTH], axis=-1)
    x_rec = causal_depthwise_conv(x_rec, conv_w, conv_b)
    y_rec = rglru(x_rec, ga_w, ga_b, gx_w, gx_b, lam) * jax.nn.gelu(gate, approximate=True)
    q = apply_partial_rope(q.reshape(B, L, N_Q_HEADS, HEAD_DIM), cos, sin)
    k = apply_partial_rope(k.reshape(B, L, N_KV_HEADS, HEAD_DIM), cos, sin)
    v = v.reshape(B, L, N_KV_HEADS, HEAD_DIM)
    y_attn = swa_sink_attention(q, k, v, sinks)
    y = jnp.concatenate([rms_norm(y_rec, g_rec), rms_norm(y_attn, g_attn)], axis=-1)
    return y @ w_out


def setup_inputs(seed: int = 0) -> dict:
    key = jax.random.key(seed)
    ks = jax.random.split(key, 26)
    f32 = jnp.float32

    def nrm(k, shape, scale):
        return jax.random.normal(k, shape, f32) * scale

    u = jax.random.uniform(ks[12], (DEPTH, REC_WIDTH), f32, 0.9, 0.999)
    s = u ** (1.0 / LRU_C)
    lru_lambda = jnp.log(s) - jnp.log1p(-s)
    return {
        'x': nrm(ks[0], (BATCH, SEQ, D_MODEL), 1.0),
        'meta_tokens': nrm(ks[1], (N_META, D_MODEL), 1.0),
        'ffn1_w_gate': nrm(ks[2], (DEPTH, D_MODEL, D_FF), D_MODEL ** -0.5),
        'ffn1_w_up': nrm(ks[3], (DEPTH, D_MODEL, D_FF), D_MODEL ** -0.5),
        'ffn1_w_down': nrm(ks[4], (DEPTH, D_FF, D_MODEL), D_FF ** -0.5 * DEEPNORM_BETA),
        'ln1_g': 1.0 + nrm(ks[5], (DEPTH, D_MODEL), 0.02),
        'ln1_b': nrm(ks[6], (DEPTH, D_MODEL), 0.02),
        'w_in': nrm(ks[7], (DEPTH, D_MODEL, D_IN), D_MODEL ** -0.5),
        'conv_w': nrm(ks[8], (DEPTH, CONV_WIDTH, REC_WIDTH), CONV_WIDTH ** -0.5),
        'conv_b': nrm(ks[9], (DEPTH, REC_WIDTH), 0.02),
        'gate_a_w': nrm(ks[10], (DEPTH, REC_BLOCKS, REC_BLOCK_DIM, REC_BLOCK_DIM), REC_BLOCK_DIM ** -0.5),
        'gate_a_b': nrm(ks[11], (DEPTH, REC_WIDTH), 0.02),
        'gate_x_w': nrm(ks[13], (DEPTH, REC_BLOCKS, REC_BLOCK_DIM, REC_BLOCK_DIM), REC_BLOCK_DIM ** -0.5),
        'gate_x_b': nrm(ks[14], (DEPTH, REC_WIDTH), 0.02),
        'lru_lambda': lru_lambda,
        'attn_sinks': nrm(ks[15], (DEPTH, N_Q_HEADS), 0.5),
        'norm_rec_g': 1.0 + nrm(ks[16], (DEPTH, REC_WIDTH), 0.02),
        'norm_attn_g': 1.0 + nrm(ks[17], (DEPTH, ATTN_WIDTH), 0.02),
        'w_out': nrm(ks[18], (DEPTH, D_MIX, D_MODEL), D_MIX ** -0.5 * DEEPNORM_BETA),
        'ln2_g': 1.0 + nrm(ks[19], (DEPTH, D_MODEL), 0.02),
        'ln2_b': nrm(ks[20], (DEPTH, D_MODEL), 0.02),
        'ffn2_w_gate': nrm(ks[21], (DEPTH, D_MODEL, D_FF), D_MODEL ** -0.5),
        'ffn2_w_up': nrm(ks[22], (DEPTH, D_MODEL, D_FF), D_MODEL ** -0.5),
        'ffn2_w_down': nrm(ks[23], (DEPTH, D_FF, D_MODEL), D_FF ** -0.5 * DEEPNORM_BETA),
        'ln3_g': 1.0 + nrm(ks[24], (DEPTH, D_MODEL), 0.02),
        'ln3_b': nrm(ks[25], (DEPTH, D_MODEL), 0.02),
    }


def reference(x, meta_tokens, ffn1_w_gate, ffn1_w_up, ffn1_w_down, ln1_g, ln1_b, w_in, conv_w, conv_b,
              gate_a_w, gate_a_b, gate_x_w, gate_x_b, lru_lambda, attn_sinks, norm_rec_g, norm_attn_g,
              w_out, ln2_g, ln2_b, ffn2_w_gate, ffn2_w_up, ffn2_w_down, ln3_g, ln3_b):
    B = x.shape[0]
    meta = jnp.broadcast_to(meta_tokens[None].astype(x.dtype), (B, N_META, D_MODEL))
    h = jnp.concatenate([meta, x], axis=1)
    cos, sin = rope_tables(h.shape[1])
    for l in range(DEPTH):
        h = layer_norm(DEEPNORM_ALPHA * h + 0.5 * swiglu(h, ffn1_w_gate[l], ffn1_w_up[l], ffn1_w_down[l]),
                       ln1_g[l], ln1_b[l])
        m = hybrid_mixer(h, w_in[l], conv_w[l], conv_b[l], gate_a_w[l], gate_a_b[l], gate_x_w[l], gate_x_b[l],
                         lru_lambda[l], attn_sinks[l], norm_rec_g[l], norm_attn_g[l], w_out[l], cos, sin)
        h = layer_norm(DEEPNORM_ALPHA * h + m, ln2_g[l], ln2_b[l])
        h = layer_norm(DEEPNORM_ALPHA * h + 0.5 * swiglu(h, ffn2_w_gate[l], ffn2_w_up[l], ffn2_w_down[l]),
                       ln3_g[l], ln3_b[l])
    return h[:, N_META:]
```

```python
import functools

import jax
import jax.numpy as jnp
from jax import lax
from jax.experimental import pallas as pl
from jax.experimental.pallas import tpu as pltpu

F32 = jnp.float32
BF16 = jnp.bfloat16

D_MODEL = 1024
DEPTH = 4
N_META = 16
D_FF = 2816
REC_WIDTH = 512
REC_BLOCKS = 8
REC_BLOCK_DIM = REC_WIDTH // REC_BLOCKS
CONV_WIDTH = 4
LRU_C = 8.0
N_Q_HEADS = 8
N_KV_HEADS = 2
HEAD_DIM = 64
ATTN_WIDTH = N_Q_HEADS * HEAD_DIM
KV_WIDTH = N_KV_HEADS * HEAD_DIM
WINDOW = 128
ROPE_DIM = HEAD_DIM // 4
ROPE_THETA = 500000.0
D_IN = 2 * REC_WIDTH + ATTN_WIDTH + 2 * KV_WIDTH
D_MIX = REC_WIDTH + ATTN_WIDTH
DEEPNORM_ALPHA = (2.0 * DEPTH) ** 0.25
LN_EPS = 1e-5
RMS_EPS = 1e-6
NEG_INF = -1e30

LANES = 128
SUBLANES = 8
MXU_DIM = 256
VMEM_LIMIT_BYTES = 56 * 1024 * 1024

FFN_ROWS = 512
FF_CHUNK = MXU_DIM
MIX_A_ROWS = 512
MIX_B_ROWS = 256
GATE_HALF = REC_WIDTH // 2


def _layer_norm(y, g, b):
    mu = jnp.mean(y, axis=-1, keepdims=True)
    d = y - mu
    var = jnp.mean(d * d, axis=-1, keepdims=True)
    return d * lax.rsqrt(var + LN_EPS) * g + b


def _rms_norm(y, g):
    ms = jnp.mean(y * y, axis=-1, keepdims=True)
    return y * lax.rsqrt(ms + RMS_EPS) * g


def _dot(a, b):
    return jnp.dot(a, b, preferred_element_type=F32)


def _dot_nt(a, b):
    return lax.dot_general(a, b, (((1,), (1,)), ((), ())), preferred_element_type=F32)


def _ffn_kernel(x_ref, wg_ref, wu_ref, wd_ref, g_ref, b_ref, o_ref):
    x = x_ref[...]
    xb = x.astype(BF16)
    acc = None
    for c in range(D_FF // FF_CHUNK):
        sl = slice(c * FF_CHUNK, (c + 1) * FF_CHUNK)
        gate = _dot(xb, wg_ref[:, sl])
        up = _dot(xb, wu_ref[:, sl])
        act = (gate * jax.nn.sigmoid(gate) * up).astype(BF16)
        part = _dot(act, wd_ref[sl, :])
        acc = part if acc is None else acc + part
    y = DEEPNORM_ALPHA * x + 0.5 * acc
    o_ref[...] = _layer_norm(y, g_ref[...], b_ref[...])


def _ffn(x, wg, wu, wd, g, b, layer, rows):
    tokens = x.shape[0]
    resident = dict(pipeline_mode=pl.Buffered(1))
    return pl.pallas_call(
        _ffn_kernel,
        grid=(tokens // rows,),
        in_specs=[
            pl.BlockSpec((rows, D_MODEL), lambda i: (i, 0)),
            pl.BlockSpec((None, D_MODEL, D_FF), lambda i: (layer, 0, 0), **resident),
            pl.BlockSpec((None, D_MODEL, D_FF), lambda i: (layer, 0, 0), **resident),
            pl.BlockSpec((None, D_FF, D_MODEL), lambda i: (layer, 0, 0), **resident),
            pl.BlockSpec((None, 1, D_MODEL), lambda i: (layer, 0, 0)),
            pl.BlockSpec((None, 1, D_MODEL), lambda i: (layer, 0, 0)),
        ],
        out_specs=pl.BlockSpec((rows, D_MODEL), lambda i: (i, 0)),
        out_shape=jax.ShapeDtypeStruct((tokens, D_MODEL), F32),
        compiler_params=pltpu.CompilerParams(
            dimension_semantics=("arbitrary",), vmem_limit_bytes=VMEM_LIMIT_BYTES),
    )(x, wg, wu, wd, g, b)


def _neg_expm1(y):
    series = y * (1.0 + y * (1.0 / 2.0) * (1.0 + y * (1.0 / 3.0) * (1.0 + y * (1.0 / 4.0) * (
        1.0 + y * (1.0 / 5.0) * (1.0 + y * (1.0 / 6.0) * (1.0 + y * (1.0 / 7.0)))))))
    return jnp.where(y > -0.25, -series, 1.0 - jnp.exp(y))


def _softplus(x):
    return jnp.maximum(x, 0.0) + jnp.log1p(jnp.exp(-jnp.abs(x)))


def _conv_and_gates(xr, xbuf, cw_ref, cb_ref, wa_ref, ba_ref, wx_ref, bx_ref, lam_ref):
    rows = xr.shape[0]
    xc = cb_ref[...] + xbuf[pl.ds(SUBLANES - 3, rows), :] * cw_ref[0:1, :]
    xc = xc + xbuf[pl.ds(SUBLANES - 2, rows), :] * cw_ref[1:2, :]
    xc = xc + xbuf[pl.ds(SUBLANES - 1, rows), :] * cw_ref[2:3, :]
    xc = xc + xr * cw_ref[3:4, :]
    xcb = xc.astype(BF16)
    lo, hi = xcb[:, :GATE_HALF], xcb[:, GATE_HALF:]
    r_lin = jnp.concatenate([_dot(lo, wa_ref[0]), _dot(hi, wa_ref[1])], axis=1) + ba_ref[...]
    i_lin = jnp.concatenate([_dot(lo, wx_ref[0]), _dot(hi, wx_ref[1])], axis=1) + bx_ref[...]
    r = jax.nn.sigmoid(r_lin)
    i = jax.nn.sigmoid(i_lin)
    log_a = (-LRU_C * _softplus(-lam_ref[...])) * r
    a = jnp.exp(log_a)
    b = jnp.sqrt(_neg_expm1(2.0 * log_a)) * (i * xc)
    return a, b


def _scan8(a, b):
    rows, width = a.shape
    groups = rows // SUBLANES
    a3 = a.reshape(groups, SUBLANES, width)
    b3 = b.reshape(groups, SUBLANES, width)
    sub = lax.broadcasted_iota(jnp.int32, (groups, SUBLANES, width), 1)
    for s in (1, 2, 4):
        keep = sub >= s
        b_prev = jnp.where(keep, pltpu.roll(b3, s, 1), 0.0)
        a_prev = jnp.where(keep, pltpu.roll(a3, s, 1), 1.0)
        b3 = b3 + a3 * b_prev
        a3 = a3 * a_prev
    return a3.reshape(rows, width), b3.reshape(rows, width)


def _row_bcast(ref, r):
    return jnp.broadcast_to(ref[pl.ds(r, 1), :], (SUBLANES, ref.shape[1]))


def _rope(t, cos, s1, s2):
    cols = []
    for j in range(t.shape[1] // LANES):
        tj = t[:, j * LANES:(j + 1) * LANES]
        up = pltpu.roll(tj, LANES - ROPE_DIM // 2, 1)
        down = pltpu.roll(tj, ROPE_DIM // 2, 1)
        cols.append(tj * cos + up * s1 + down * s2)
    return cols[0] if len(cols) == 1 else jnp.concatenate(cols, axis=1)


def _split_kv(t):
    swapped = pltpu.roll(t, HEAD_DIM, 1)
    low = lax.broadcasted_iota(jnp.int32, t.shape, 1) < HEAD_DIM
    zero = jnp.zeros_like(t)
    parts = [jnp.where(low, t, zero), jnp.where(low, zero, swapped),
             jnp.where(low, swapped, zero), jnp.where(low, zero, t)]
    return jnp.concatenate(parts, axis=1).astype(BF16)


def _gelu_gate_norm(hseq, gate, grec_ref):
    y = hseq * jax.nn.gelu(gate, approximate=True)
    return _rms_norm(y, grec_ref[...]).astype(BF16)


def _head_pair(s_band, band_ok, s_meta, meta_ok_even, meta_ok_odd, sink_even, sink_odd,
               v_band, v_meta):
    rows = s_meta.shape[0]
    sme = jnp.where(meta_ok_even, s_meta, NEG_INF)
    smo = jnp.where(meta_ok_odd, s_meta, NEG_INF)
    m_e = jnp.maximum(jnp.max(sme, axis=-1, keepdims=True), sink_even)
    m_o = jnp.maximum(jnp.max(smo, axis=-1, keepdims=True), sink_odd)
    if s_band is not None:
        half = s_band.shape[1] // 2
        se = jnp.where(band_ok, s_band[:, :half], NEG_INF)
        so = jnp.where(band_ok, s_band[:, half:], NEG_INF)
        m_e = jnp.maximum(m_e, jnp.max(se, axis=-1, keepdims=True))
        m_o = jnp.maximum(m_o, jnp.max(so, axis=-1, keepdims=True))
    pme = jnp.exp(sme - m_e)
    pmo = jnp.exp(smo - m_o)
    den_e = jnp.sum(pme, axis=-1, keepdims=True) + jnp.exp(sink_even - m_e)
    den_o = jnp.sum(pmo, axis=-1, keepdims=True) + jnp.exp(sink_odd - m_o)
    out = _dot((pme + pmo).astype(BF16), v_meta)
    if s_band is not None:
        pe = jnp.exp(se - m_e)
        po = jnp.exp(so - m_o)
        den_e = den_e + jnp.sum(pe, axis=-1, keepdims=True)
        den_o = den_o + jnp.sum(po, axis=-1, keepdims=True)
        out = out + _dot(jnp.concatenate([pe, po], axis=1).astype(BF16), v_band)
    low = lax.broadcasted_iota(jnp.int32, (rows, LANES), 1) < HEAD_DIM
    return out * jnp.where(low, 1.0 / den_e, 1.0 / den_o)


def _col(ref_or_val, j):
    return ref_or_val[:, j * LANES:(j + 1) * LANES]


def _mix_a_kernel(h_ref, win_ref, cw_ref, cb_ref, wa_ref, ba_ref, wx_ref, bx_ref, lam_ref,
                  grec_ref, cos_ref, s1_ref, s2_ref, ctail_ref, hinit_ref,
                  yrec_ref, q_ref, k4_ref, v4_ref,
                  xbuf, state, a1_buf, b1_buf, a2_buf, b2_buf, h2_buf, h3_buf):
    rows = h_ref.shape[0]
    g1 = rows // SUBLANES
    g2 = g1 // SUBLANES

    @pl.when(pl.program_id(1) == 0)
    def _():
        xbuf[0:SUBLANES, :] = ctail_ref[...]
        state[...] = hinit_ref[...]

    proj = _dot(h_ref[...].astype(BF16), win_ref[...])
    xr = proj[:, :REC_WIDTH]
    gate = proj[:, REC_WIDTH:2 * REC_WIDTH]
    q = proj[:, 2 * REC_WIDTH:2 * REC_WIDTH + ATTN_WIDTH]
    k = proj[:, 2 * REC_WIDTH + ATTN_WIDTH:2 * REC_WIDTH + ATTN_WIDTH + KV_WIDTH]
    v = proj[:, 2 * REC_WIDTH + ATTN_WIDTH + KV_WIDTH:]

    xbuf[pl.ds(SUBLANES, rows), :] = xr
    a, b = _conv_and_gates(xr, xbuf, cw_ref, cb_ref, wa_ref, ba_ref, wx_ref, bx_ref, lam_ref)
    xbuf[0:SUBLANES, :] = xr[rows - SUBLANES:, :]

    ends = lambda buf, c, n: buf[c, pl.ds(SUBLANES - 1, n, stride=SUBLANES), :]
    col_seqs = []
    for c in range(REC_WIDTH // LANES):
        lanes = slice(c * LANES, (c + 1) * LANES)
        h0 = state[:, lanes]
        a1, b1 = _scan8(a[:, lanes], b[:, lanes])
        a1_buf[c] = a1
        b1_buf[c] = b1
        a2, b2 = _scan8(ends(a1_buf, c, g1), ends(b1_buf, c, g1))
        a2_buf[c] = a2
        b2_buf[c] = b2
        a3, b3 = _scan8(ends(a2_buf, c, g2), ends(b2_buf, c, g2))
        h3_buf[c, 0:SUBLANES, :] = h0
        h3_buf[c, SUBLANES:, :] = b3 + a3 * h0
        h2_buf[c, 0:SUBLANES, :] = h0
        for j in range(g2):
            sl = slice(j * SUBLANES, (j + 1) * SUBLANES)
            h2_buf[c, pl.ds(SUBLANES + j * SUBLANES, SUBLANES), :] = (
                b2[sl] + a2[sl] * _row_bcast(h3_buf.at[c], SUBLANES - 1 + j))
        pieces = []
        for j in range(g1):
            sl = slice(j * SUBLANES, (j + 1) * SUBLANES)
            pieces.append(b1[sl] + a1[sl] * _row_bcast(h2_buf.at[c], SUBLANES - 1 + j))
        col_seqs.append(jnp.concatenate(pieces, axis=0))
        state[:, lanes] = _row_bcast(h2_buf.at[c], SUBLANES - 1 + g1)
    hseq = jnp.concatenate(col_seqs, axis=1)

    yrec_ref[...] = _gelu_gate_norm(hseq, gate, grec_ref)
    cos, s1, s2 = cos_ref[...], s1_ref[...], s2_ref[...]
    q_ref[...] = (_rope(q, cos, s1, s2) * (HEAD_DIM ** -0.5)).astype(BF16)
    k4_ref[...] = _split_kv(_rope(k, cos, s1, s2))
    v4_ref[...] = _split_kv(v)


def _mix_a(h, w_in, conv_w, conv_b, wa, ba, wx, bx, lam, g_rec, cos, s1, s2, ctail, hinit,
           layer, batch, seq):
    rows = MIX_A_ROWS
    nc = seq // rows
    tokens = batch * seq
    tok = lambda width: pl.BlockSpec((rows, width), lambda bi, ci: (bi * nc + ci, 0))
    vec = lambda width: pl.BlockSpec((None, 1, width), lambda bi, ci: (layer, 0, 0))
    gates = pl.BlockSpec((None, 2, GATE_HALF, GATE_HALF), lambda bi, ci: (layer, 0, 0, 0))
    table = pl.BlockSpec((rows, LANES), lambda bi, ci: (ci, 0))
    seed = pl.BlockSpec((SUBLANES, REC_WIDTH), lambda bi, ci: (0, 0))
    g1 = rows // SUBLANES
    g2 = g1 // SUBLANES
    ncol = REC_WIDTH // LANES
    return pl.pallas_call(
        _mix_a_kernel,
        grid=(batch, nc),
        in_specs=[
            tok(D_MODEL),
            pl.BlockSpec((None, D_MODEL, D_IN), lambda bi, ci: (layer, 0, 0),
                         pipeline_mode=pl.Buffered(1)),
            pl.BlockSpec((None, CONV_WIDTH, REC_WIDTH), lambda bi, ci: (layer, 0, 0)),
            vec(REC_WIDTH), gates, vec(REC_WIDTH), gates, vec(REC_WIDTH), vec(REC_WIDTH),
            vec(REC_WIDTH), table, table, table, seed, seed,
        ],
        out_specs=[tok(REC_WIDTH), tok(ATTN_WIDTH), tok(4 * LANES), tok(4 * LANES)],
        out_shape=[jax.ShapeDtypeStruct((tokens, REC_WIDTH), BF16),
                   jax.ShapeDtypeStruct((tokens, ATTN_WIDTH), BF16),
                   jax.ShapeDtypeStruct((tokens, 4 * LANES), BF16),
                   jax.ShapeDtypeStruct((tokens, 4 * LANES), BF16)],
        scratch_shapes=[
            pltpu.VMEM((SUBLANES + rows, REC_WIDTH), F32),
            pltpu.VMEM((SUBLANES, REC_WIDTH), F32),
            pltpu.VMEM((ncol, rows, LANES), F32),
            pltpu.VMEM((ncol, rows, LANES), F32),
            pltpu.VMEM((ncol, g1, LANES), F32),
            pltpu.VMEM((ncol, g1, LANES), F32),
            pltpu.VMEM((ncol, SUBLANES + g1, LANES), F32),
            pltpu.VMEM((ncol, SUBLANES + g2, LANES), F32),
        ],
        compiler_params=pltpu.CompilerParams(
            dimension_semantics=("arbitrary", "arbitrary"), vmem_limit_bytes=VMEM_LIMIT_BYTES),
    )(h, w_in, conv_w, conv_b, wa, ba, wx, bx, lam, g_rec, cos, s1, s2, ctail, hinit)


def _mix_b_kernel(layer, sink_ref, q_ref, k4_ref, v4_ref, kp_ref, vp_ref, km_ref, vm_ref,
                  yrec_ref, h_ref, wout_ref, gattn_ref, lng_ref, lnb_ref, o_ref):
    rows = q_ref.shape[0]
    first = pl.program_id(1) == 0
    qi = lax.broadcasted_iota(jnp.int32, (WINDOW, 2 * WINDOW), 0)
    kj = lax.broadcasted_iota(jnp.int32, (WINDOW, 2 * WINDOW), 1)
    band_ok = (kj > qi) & (kj <= qi + WINDOW)
    band_ok_first = band_ok & (kj >= jnp.where(first, WINDOW, 0))
    lane32 = lax.broadcasted_iota(jnp.int32, (WINDOW, 2 * N_META), 1)
    meta_even = lane32 < N_META
    meta_odd = lane32 >= N_META

    blocks = []
    for jb in range(rows // WINDOW):
        cur = slice(jb * WINDOW, (jb + 1) * WINDOW)
        prv = slice((jb - 1) * WINDOW, jb * WINDOW)
        ok = band_ok_first if jb == 0 else band_ok
        cols = []
        for g in range(N_KV_HEADS):
            lo_c, hi_c = 2 * g, 2 * g + 1
            if jb == 0:
                k_lo_prev, k_hi_prev = _col(kp_ref, lo_c), _col(kp_ref, hi_c)
                v_lo_prev, v_hi_prev = _col(vp_ref, lo_c), _col(vp_ref, hi_c)
            else:
                k_lo_prev, k_hi_prev = k4_ref[prv, lo_c * LANES:(lo_c + 1) * LANES], k4_ref[prv, hi_c * LANES:(hi_c + 1) * LANES]
                v_lo_prev, v_hi_prev = v4_ref[prv, lo_c * LANES:(lo_c + 1) * LANES], v4_ref[prv, hi_c * LANES:(hi_c + 1) * LANES]
            k_band = jnp.concatenate(
                [k_lo_prev, k4_ref[cur, lo_c * LANES:(lo_c + 1) * LANES],
                 k_hi_prev, k4_ref[cur, hi_c * LANES:(hi_c + 1) * LANES]], axis=0)
            v_band = jnp.concatenate(
                [v_lo_prev, v4_ref[cur, lo_c * LANES:(lo_c + 1) * LANES],
                 v_hi_prev, v4_ref[cur, hi_c * LANES:(hi_c + 1) * LANES]], axis=0)
            k_meta = jnp.concatenate([_col(km_ref, lo_c), _col(km_ref, hi_c)], axis=0)
            v_meta = jnp.concatenate([_col(vm_ref, lo_c), _col(vm_ref, hi_c)], axis=0)
            lhs = jnp.concatenate([q_ref[cur, lo_c * LANES:(lo_c + 1) * LANES],
                                   q_ref[cur, hi_c * LANES:(hi_c + 1) * LANES]], axis=0)
            s_band = _dot_nt(lhs, k_band)
            s_meta = _dot_nt(lhs, k_meta)
            for c in range(2):
                rs = slice(c * WINDOW, (c + 1) * WINDOW)
                head = 4 * g + 2 * c
                cols.append(_head_pair(
                    s_band[rs], ok, s_meta[rs], meta_even, meta_odd,
                    sink_ref[layer, head], sink_ref[layer, head + 1], v_band, v_meta))
        y_attn = jnp.concatenate(cols, axis=1)
        blocks.append(_rms_norm(y_attn, gattn_ref[...]).astype(BF16))
    y_attn = blocks[0] if len(blocks) == 1 else jnp.concatenate(blocks, axis=0)
    mixed = _dot(jnp.concatenate([yrec_ref[...], y_attn], axis=1), wout_ref[...])
    o_ref[...] = _layer_norm(DEEPNORM_ALPHA * h_ref[...] + mixed, lng_ref[...], lnb_ref[...])


def _mix_b(sinks, q, k4, v4, km4, vm4, yrec, h, w_out, g_attn, ln_g, ln_b, layer, batch, seq):
    rows = MIX_B_ROWS
    nc = seq // rows
    blocks_per_step = rows // WINDOW
    blocks_per_seq = seq // WINDOW
    tokens = batch * seq
    tok = lambda width: pl.BlockSpec((rows, width), lambda bi, ci: (bi * nc + ci, 0))
    prev = pl.BlockSpec(
        (WINDOW, 4 * LANES),
        lambda bi, ci: (jnp.maximum(bi * blocks_per_seq + ci * blocks_per_step - 1, 0), 0))
    meta = pl.BlockSpec((N_META, 4 * LANES), lambda bi, ci: (0, 0))
    vec = lambda width: pl.BlockSpec((None, 1, width), lambda bi, ci: (layer, 0, 0))
    return pl.pallas_call(
        functools.partial(_mix_b_kernel, layer),
        grid=(batch, nc),
        in_specs=[
            pl.BlockSpec(memory_space=pltpu.SMEM),
            tok(ATTN_WIDTH), tok(4 * LANES), tok(4 * LANES), prev, prev, meta, meta,
            tok(REC_WIDTH), tok(D_MODEL),
            pl.BlockSpec((None, D_MIX, D_MODEL), lambda bi, ci: (layer, 0, 0),
                         pipeline_mode=pl.Buffered(1)),
            vec(ATTN_WIDTH), vec(D_MODEL), vec(D_MODEL),
        ],
        out_specs=tok(D_MODEL),
        out_shape=jax.ShapeDtypeStruct((tokens, D_MODEL), F32),
        compiler_params=pltpu.CompilerParams(
            dimension_semantics=("arbitrary", "arbitrary"), vmem_limit_bytes=VMEM_LIMIT_BYTES),
    )(sinks, q, k4, v4, k4, v4, km4, vm4, yrec, h, w_out, g_attn, ln_g, ln_b)


def _meta_mix_kernel(layer, sink_ref, h_ref, win_ref, cw_ref, cb_ref, wa_ref, ba_ref, wx_ref,
                     bx_ref, lam_ref, grec_ref, cos_ref, s1_ref, s2_ref, wout_ref, gattn_ref,
                     lng_ref, lnb_ref,
                     o_ref, ctail_ref, hstate_ref, k4_ref, v4_ref, xbuf, hbuf):
    rows = N_META
    h_in = h_ref[...]
    proj = _dot(h_in.astype(BF16), win_ref[...])
    xr = proj[:, :REC_WIDTH]
    gate = proj[:, REC_WIDTH:2 * REC_WIDTH]
    q = proj[:, 2 * REC_WIDTH:2 * REC_WIDTH + ATTN_WIDTH]
    k = proj[:, 2 * REC_WIDTH + ATTN_WIDTH:2 * REC_WIDTH + ATTN_WIDTH + KV_WIDTH]
    v = proj[:, 2 * REC_WIDTH + ATTN_WIDTH + KV_WIDTH:]

    xbuf[0:SUBLANES, :] = jnp.zeros((SUBLANES, REC_WIDTH), F32)
    xbuf[pl.ds(SUBLANES, rows), :] = xr
    a, b = _conv_and_gates(xr, xbuf, cw_ref, cb_ref, wa_ref, ba_ref, wx_ref, bx_ref, lam_ref)
    ctail_ref[...] = xr[rows - SUBLANES:, :]

    a1, b1 = _scan8(a, b)
    hbuf[0:SUBLANES, :] = b1[:SUBLANES]
    hbuf[SUBLANES:, :] = b1[SUBLANES:] + a1[SUBLANES:] * _row_bcast(hbuf, SUBLANES - 1)
    hseq = hbuf[...]
    hstate_ref[...] = _row_bcast(hbuf, rows - 1)
    y_rec = _gelu_gate_norm(hseq, gate, grec_ref)

    cos, s1, s2 = cos_ref[...], s1_ref[...], s2_ref[...]
    qb = (_rope(q, cos, s1, s2) * (HEAD_DIM ** -0.5)).astype(BF16)
    k4 = _split_kv(_rope(k, cos, s1, s2))
    v4 = _split_kv(v)
    k4_ref[...] = k4
    v4_ref[...] = v4

    qi = lax.broadcasted_iota(jnp.int32, (rows, 2 * N_META), 0)
    lane32 = lax.broadcasted_iota(jnp.int32, (rows, 2 * N_META), 1)
    meta_even = (lane32 < N_META) & (lane32 <= qi)
    meta_odd = (lane32 >= N_META) & (lane32 - N_META <= qi)
    cols = []
    for g in range(N_KV_HEADS):
        lo_c, hi_c = 2 * g, 2 * g + 1
        k_meta = jnp.concatenate([_col(k4, lo_c), _col(k4, hi_c)], axis=0)
        v_meta = jnp.concatenate([_col(v4, lo_c), _col(v4, hi_c)], axis=0)
        lhs = jnp.concatenate([_col(qb, lo_c), _col(qb, hi_c)], axis=0)
        s_meta = _dot_nt(lhs, k_meta)
        for c in range(2):
            head = 4 * g + 2 * c
            cols.append(_head_pair(
                None, None, s_meta[c * rows:(c + 1) * rows], meta_even, meta_odd,
                sink_ref[layer, head], sink_ref[layer, head + 1], None, v_meta))
    y_attn = _rms_norm(jnp.concatenate(cols, axis=1), gattn_ref[...]).astype(BF16)
    mixed = _dot(jnp.concatenate([y_rec, y_attn], axis=1), wout_ref[...])
    o_ref[...] = _layer_norm(DEEPNORM_ALPHA * h_in + mixed, lng_ref[...], lnb_ref[...])


def _meta_mix(sinks, h, w_in, conv_w, conv_b, wa, ba, wx, bx, lam, g_rec, cos, s1, s2, w_out,
              g_attn, ln_g, ln_b, layer):
    vec = lambda width: pl.BlockSpec((None, 1, width), lambda i: (layer, 0, 0))
    gates = pl.BlockSpec((None, 2, GATE_HALF, GATE_HALF), lambda i: (layer, 0, 0, 0))
    table = pl.BlockSpec((N_META, LANES), lambda i: (0, 0))
    full = lambda r, w: pl.BlockSpec((r, w), lambda i: (0, 0))
    return pl.pallas_call(
        functools.partial(_meta_mix_kernel, layer),
        grid=(1,),
        in_specs=[
            pl.BlockSpec(memory_space=pltpu.SMEM),
            full(N_META, D_MODEL),
            pl.BlockSpec((None, D_MODEL, D_IN), lambda i: (layer, 0, 0)),
            pl.BlockSpec((None, CONV_WIDTH, REC_WIDTH), lambda i: (layer, 0, 0)),
            vec(REC_WIDTH), gates, vec(REC_WIDTH), gates, vec(REC_WIDTH), vec(REC_WIDTH),
            vec(REC_WIDTH), table, table, table,
            pl.BlockSpec((None, D_MIX, D_MODEL), lambda i: (layer, 0, 0)),
            vec(ATTN_WIDTH), vec(D_MODEL), vec(D_MODEL),
        ],
        out_specs=[full(N_META, D_MODEL), full(SUBLANES, REC_WIDTH), full(SUBLANES, REC_WIDTH),
                   full(N_META, 4 * LANES), full(N_META, 4 * LANES)],
        out_shape=[jax.ShapeDtypeStruct((N_META, D_MODEL), F32),
                   jax.ShapeDtypeStruct((SUBLANES, REC_WIDTH), F32),
                   jax.ShapeDtypeStruct((SUBLANES, REC_WIDTH), F32),
                   jax.ShapeDtypeStruct((N_META, 4 * LANES), BF16),
                   jax.ShapeDtypeStruct((N_META, 4 * LANES), BF16)],
        scratch_shapes=[pltpu.VMEM((SUBLANES + N_META, REC_WIDTH), F32),
                        pltpu.VMEM((N_META, REC_WIDTH), F32)],
        compiler_params=pltpu.CompilerParams(
            dimension_semantics=("arbitrary",), vmem_limit_bytes=VMEM_LIMIT_BYTES),
    )(sinks, h, w_in, conv_w, conv_b, wa, ba, wx, bx, lam, g_rec, cos, s1, s2, w_out, g_attn,
      ln_g, ln_b)


def _block_diag_halves(w):
    per_half = REC_BLOCKS // 2
    out = jnp.zeros((w.shape[0], 2, GATE_HALF, GATE_HALF), w.dtype)
    for blk in range(REC_BLOCKS):
        half, pos = divmod(blk, per_half)
        sl = slice(pos * REC_BLOCK_DIM, (pos + 1) * REC_BLOCK_DIM)
        out = out.at[:, half, sl, sl].set(w[:, blk])
    return out.astype(BF16)


def _rope_tables(first_pos, count):
    pos = first_pos + jnp.arange(count, dtype=F32)
    inv_freq = ROPE_THETA ** (-jnp.arange(0, ROPE_DIM, 2, dtype=F32) / ROPE_DIM)
    ang = pos[:, None] * inv_freq[None, :]
    cos, sin = jnp.cos(ang), jnp.sin(ang)
    half = ROPE_DIM // 2
    pad = HEAD_DIM - ROPE_DIM
    ones = jnp.ones((count, pad), F32)
    zeros_h = jnp.zeros((count, half), F32)
    zeros_p = jnp.zeros((count, pad), F32)
    c_head = jnp.concatenate([cos, cos, ones], axis=1)
    s1_head = jnp.concatenate([-sin, zeros_h, zeros_p], axis=1)
    s2_head = jnp.concatenate([zeros_h, sin, zeros_p], axis=1)
    two = lambda t: jnp.concatenate([t, t], axis=1)
    return two(c_head), two(s1_head), two(s2_head)


def kernel(x, meta_tokens, ffn1_w_gate, ffn1_w_up, ffn1_w_down, ln1_g, ln1_b, w_in, conv_w, conv_b, gate_a_w, gate_a_b, gate_x_w, gate_x_b, lru_lambda, attn_sinks, norm_rec_g, norm_attn_g, w_out, ln2_g, ln2_b, ffn2_w_gate, ffn2_w_up, ffn2_w_down, ln3_g, ln3_b):
    batch, seq, _ = x.shape
    assert seq % MIX_A_ROWS == 0 and seq % MIX_B_ROWS == 0 and (batch * seq) % FFN_ROWS == 0
    assert MIX_A_ROWS == SUBLANES ** 3

    bf = lambda w: w.astype(BF16)
    row = lambda p: p.reshape(p.shape[0], 1, p.shape[1])
    f1g, f1u, f1d = bf(ffn1_w_gate), bf(ffn1_w_up), bf(ffn1_w_down)
    f2g, f2u, f2d = bf(ffn2_w_gate), bf(ffn2_w_up), bf(ffn2_w_down)
    win, wout = bf(w_in), bf(w_out)
    wa, wx = _block_diag_halves(gate_a_w), _block_diag_halves(gate_x_w)
    l1g, l1b, l2g, l2b, l3g, l3b = map(row, (ln1_g, ln1_b, ln2_g, ln2_b, ln3_g, ln3_b))
    cb, ba, bx, lam, grec, gattn = map(
        row, (conv_b, gate_a_b, gate_x_b, lru_lambda, norm_rec_g, norm_attn_g))
    cos_m, s1_m, s2_m = _rope_tables(0.0, N_META)
    cos_x, s1_x, s2_x = _rope_tables(float(N_META), seq)

    h = x.reshape(batch * seq, D_MODEL)
    hm = meta_tokens.astype(x.dtype)
    for l in range(DEPTH):
        hm = _ffn(hm, f1g, f1u, f1d, l1g, l1b, l, N_META)
        h = _ffn(h, f1g, f1u, f1d, l1g, l1b, l, FFN_ROWS)
        hm, ctail, hinit, km4, vm4 = _meta_mix(
            attn_sinks, hm, win, conv_w, cb, wa, ba, wx, bx, lam, grec, cos_m, s1_m, s2_m,
            wout, gattn, l2g, l2b, l)
        yrec, q, k4, v4 = _mix_a(h, win, conv_w, cb, wa, ba, wx, bx, lam, grec,
                                 cos_x, s1_x, s2_x, ctail, hinit, l, batch, seq)
        h = _mix_b(attn_sinks, q, k4, v4, km4, vm4, yrec, h, wout, gattn, l2g, l2b,
                   l, batch, seq)
        if l + 1 < DEPTH:
            hm = _ffn(hm, f2g, f2u, f2d, l3g, l3b, l, N_META)
        h = _ffn(h, f2g, f2u, f2d, l3g, l3b, l, FFN_ROWS)
    return h.reshape(batch, seq, D_MODEL)
```

```python
import functools

import jax
import jax.numpy as jnp
from jax import lax
from jax.experimental import pallas as pl
from jax.experimental.pallas import tpu as pltpu

F32 = jnp.float32
BF16 = jnp.bfloat16

D_MODEL = 1024
DEPTH = 4
N_META = 16
D_FF = 2816
REC_WIDTH = 512
REC_BLOCKS = 8
REC_BLOCK_DIM = REC_WIDTH // REC_BLOCKS
CONV_WIDTH = 4
LRU_C = 8.0
N_Q_HEADS = 8
N_KV_HEADS = 2
HEAD_DIM = 64
ATTN_WIDTH = N_Q_HEADS * HEAD_DIM
KV_WIDTH = N_KV_HEADS * HEAD_DIM
WINDOW = 128
ROPE_DIM = HEAD_DIM // 4
ROPE_THETA = 500000.0
D_IN = 2 * REC_WIDTH + ATTN_WIDTH + 2 * KV_WIDTH
D_MIX = REC_WIDTH + ATTN_WIDTH
DEEPNORM_ALPHA = (2.0 * DEPTH) ** 0.25
LN_EPS = 1e-5
RMS_EPS = 1e-6
NEG_INF = -1e30

LANES = 128
SUBLANES = 8
MXU_DIM = 256
VMEM_LIMIT_BYTES = 56 * 1024 * 1024

FFN_ROWS = 512
FF_CHUNK = MXU_DIM
MIX_A_ROWS = 512
MIX_A_BLOCK = 64
MIX_B_ROWS = 256
GATE_HALF = REC_WIDTH // 2


def _layer_norm(y, g, b):
    mu = jnp.mean(y, axis=-1, keepdims=True)
    d = y - mu
    var = jnp.mean(d * d, axis=-1, keepdims=True)
    return d * lax.rsqrt(var + LN_EPS) * g + b


def _rms_norm(y, g):
    ms = jnp.mean(y * y, axis=-1, keepdims=True)
    return y * lax.rsqrt(ms + RMS_EPS) * g


def _dot(a, b):
    return jnp.dot(a, b, preferred_element_type=F32)


def _dot_nt(a, b):
    return lax.dot_general(a, b, (((1,), (1,)), ((), ())), preferred_element_type=F32)


def _ffn_gate_up(xb, c, wg_ref, wu_ref):
    sl = slice(c * FF_CHUNK, (c + 1) * FF_CHUNK)
    return _dot(xb, wg_ref[:, sl]), _dot(xb, wu_ref[:, sl])


def _ffn_down(gate_up, c, wd_ref):
    gate, up = gate_up
    act = (gate * jax.nn.sigmoid(gate) * up).astype(BF16)
    return _dot(act, wd_ref[c * FF_CHUNK:(c + 1) * FF_CHUNK, :])


def _ffn_chunk(xb, c, wg_ref, wu_ref, wd_ref):
    return _ffn_down(_ffn_gate_up(xb, c, wg_ref, wu_ref), c, wd_ref)


def _ffn_body(x, wg_ref, wu_ref, wd_ref, g_ref, b_ref):
    xb = x.astype(BF16)
    acc = None
    for c in range(D_FF // FF_CHUNK):
        part = _ffn_chunk(xb, c, wg_ref, wu_ref, wd_ref)
        acc = part if acc is None else acc + part
    y = DEEPNORM_ALPHA * x + 0.5 * acc
    return _layer_norm(y, g_ref[...], b_ref[...])


def _ffn_kernel(x_ref, wg_ref, wu_ref, wd_ref, g_ref, b_ref, o_ref):
    o_ref[...] = _ffn_body(x_ref[...], wg_ref, wu_ref, wd_ref, g_ref, b_ref)


def _ffn(x, wg, wu, wd, g, b, layer, rows):
    tokens = x.shape[0]
    resident = dict(pipeline_mode=pl.Buffered(1))
    return pl.pallas_call(
        _ffn_kernel,
        grid=(tokens // rows,),
        in_specs=[
            pl.BlockSpec((rows, D_MODEL), lambda i: (i, 0)),
            pl.BlockSpec((None, D_MODEL, D_FF), lambda i: (layer, 0, 0), **resident),
            pl.BlockSpec((None, D_MODEL, D_FF), lambda i: (layer, 0, 0), **resident),
            pl.BlockSpec((None, D_FF, D_MODEL), lambda i: (layer, 0, 0), **resident),
            pl.BlockSpec((None, 1, D_MODEL), lambda i: (layer, 0, 0)),
            pl.BlockSpec((None, 1, D_MODEL), lambda i: (layer, 0, 0)),
        ],
        out_specs=pl.BlockSpec((rows, D_MODEL), lambda i: (i, 0)),
        out_shape=jax.ShapeDtypeStruct((tokens, D_MODEL), F32),
        compiler_params=pltpu.CompilerParams(
            dimension_semantics=("arbitrary",), vmem_limit_bytes=VMEM_LIMIT_BYTES),
    )(x, wg, wu, wd, g, b)


def _neg_expm1(y):
    series = y * (1.0 + y * (1.0 / 2.0) * (1.0 + y * (1.0 / 3.0) * (1.0 + y * (1.0 / 4.0) * (
        1.0 + y * (1.0 / 5.0) * (1.0 + y * (1.0 / 6.0) * (1.0 + y * (1.0 / 7.0)))))))
    return jnp.where(y > -0.25, -series, 1.0 - jnp.exp(y))


def _softplus(x):
    return jnp.maximum(x, 0.0) + jnp.log1p(jnp.exp(-jnp.abs(x)))


def _conv_and_gates(xr, back, cw_ref, cb_ref, wa_ref, ba_ref, wx_ref, bx_ref, lam_ref):
    xc = cb_ref[...] + back(3) * cw_ref[0:1, :]
    xc = xc + back(2) * cw_ref[1:2, :]
    xc = xc + back(1) * cw_ref[2:3, :]
    xc = xc + xr * cw_ref[3:4, :]
    xcb = xc.astype(BF16)
    lo, hi = xcb[:, :GATE_HALF], xcb[:, GATE_HALF:]
    r_lin = jnp.concatenate([_dot(lo, wa_ref[0]), _dot(hi, wa_ref[1])], axis=1) + ba_ref[...]
    i_lin = jnp.concatenate([_dot(lo, wx_ref[0]), _dot(hi, wx_ref[1])], axis=1) + bx_ref[...]
    r = jax.nn.sigmoid(r_lin)
    i = jax.nn.sigmoid(i_lin)
    log_a = (-LRU_C * _softplus(-lam_ref[...])) * r
    a = jnp.exp(log_a)
    b = jnp.sqrt(_neg_expm1(2.0 * log_a)) * (i * xc)
    return a, b


def _scan8(a, b):
    rows, width = a.shape
    groups = rows // SUBLANES
    a3 = a.reshape(groups, SUBLANES, width)
    b3 = b.reshape(groups, SUBLANES, width)
    sub = lax.broadcasted_iota(jnp.int32, (groups, SUBLANES, width), 1)
    for s in (1, 2, 4):
        keep = sub >= s
        b_prev = jnp.where(keep, pltpu.roll(b3, s, 1), 0.0)
        a_prev = jnp.where(keep, pltpu.roll(a3, s, 1), 1.0)
        b3 = b3 + a3 * b_prev
        a3 = a3 * a_prev
    return a3.reshape(rows, width), b3.reshape(rows, width)


def _row_bcast(ref, r):
    return jnp.broadcast_to(ref[pl.ds(r, 1), :], (SUBLANES, ref.shape[1]))


def _rope(t, cos, s1, s2):
    cols = []
    for j in range(t.shape[1] // LANES):
        tj = t[:, j * LANES:(j + 1) * LANES]
        up = pltpu.roll(tj, LANES - ROPE_DIM // 2, 1)
        down = pltpu.roll(tj, ROPE_DIM // 2, 1)
        cols.append(tj * cos + up * s1 + down * s2)
    return cols[0] if len(cols) == 1 else jnp.concatenate(cols, axis=1)


def _split_kv(t):
    swapped = pltpu.roll(t, HEAD_DIM, 1)
    low = lax.broadcasted_iota(jnp.int32, t.shape, 1) < HEAD_DIM
    zero = jnp.zeros_like(t)
    parts = [jnp.where(low, t, zero), jnp.where(low, zero, swapped),
             jnp.where(low, swapped, zero), jnp.where(low, zero, t)]
    return jnp.concatenate(parts, axis=1).astype(BF16)


def _gelu_gate_norm(hseq, gate, grec_ref):
    y = hseq * jax.nn.gelu(gate, approximate=True)
    return _rms_norm(y, grec_ref[...]).astype(BF16)


def _head_pair(s_band, band_ok, s_meta, meta_ok_even, meta_ok_odd, sink_even, sink_odd,
               v_band, v_meta):
    rows = s_meta.shape[0]
    sme = jnp.where(meta_ok_even, s_meta, NEG_INF)
    smo = jnp.where(meta_ok_odd, s_meta, NEG_INF)
    m_e = jnp.maximum(jnp.max(sme, axis=-1, keepdims=True), sink_even)
    m_o = jnp.maximum(jnp.max(smo, axis=-1, keepdims=True), sink_odd)
    if s_band is not None:
        half = s_band.shape[1] // 2
        se = jnp.where(band_ok, s_band[:, :half], NEG_INF)
        so = jnp.where(band_ok, s_band[:, half:], NEG_INF)
        m_e = jnp.maximum(m_e, jnp.max(se, axis=-1, keepdims=True))
        m_o = jnp.maximum(m_o, jnp.max(so, axis=-1, keepdims=True))
    pme = jnp.exp(sme - m_e)
    pmo = jnp.exp(smo - m_o)
    den_e = jnp.sum(pme, axis=-1, keepdims=True) + jnp.exp(sink_even - m_e)
    den_o = jnp.sum(pmo, axis=-1, keepdims=True) + jnp.exp(sink_odd - m_o)
    out = _dot((pme + pmo).astype(BF16), v_meta)
    if s_band is not None:
        pe = jnp.exp(se - m_e)
        po = jnp.exp(so - m_o)
        den_e = den_e + jnp.sum(pe, axis=-1, keepdims=True)
        den_o = den_o + jnp.sum(po, axis=-1, keepdims=True)
        out = out + _dot(jnp.concatenate([pe, po], axis=1).astype(BF16), v_band)
    low = lax.broadcasted_iota(jnp.int32, (rows, LANES), 1) < HEAD_DIM
    return out * jnp.where(low, 1.0 / den_e, 1.0 / den_o)


def _col(ref_or_val, j):
    return ref_or_val[:, j * LANES:(j + 1) * LANES]


def _mix_a_block(blk, pbuf, state, a1_buf, b1_buf, h2_buf, cw_ref, cb_ref, wa_ref, ba_ref, wx_ref,
                 bx_ref, lam_ref, grec_ref, cos_ref, s1_ref, s2_ref, yrec_ref, q_ref, k4_ref,
                 v4_ref):
    rows = MIX_A_BLOCK
    out_rows = slice(blk * rows, (blk + 1) * rows)
    r0 = SUBLANES + blk * rows
    xr = pbuf[pl.ds(r0, rows), 0:REC_WIDTH]
    back = lambda k: pbuf[pl.ds(r0 - k, rows), 0:REC_WIDTH]
    a, b = _conv_and_gates(xr, back, cw_ref, cb_ref, wa_ref, ba_ref, wx_ref, bx_ref, lam_ref)

    groups = rows // SUBLANES
    col_seqs = []
    for c in range(REC_WIDTH // LANES):
        lanes = slice(c * LANES, (c + 1) * LANES)
        h0 = state[:, lanes]
        a1, b1 = _scan8(a[:, lanes], b[:, lanes])
        a1_buf[c] = a1
        b1_buf[c] = b1
        a2, b2 = _scan8(a1_buf[c, pl.ds(SUBLANES - 1, groups, stride=SUBLANES), :],
                        b1_buf[c, pl.ds(SUBLANES - 1, groups, stride=SUBLANES), :])
        h2_buf[c, 0:SUBLANES, :] = h0
        h2_buf[c, SUBLANES:, :] = b2 + a2 * h0
        pieces = []
        for j in range(groups):
            sl = slice(j * SUBLANES, (j + 1) * SUBLANES)
            pieces.append(b1[sl] + a1[sl] * _row_bcast(h2_buf.at[c], SUBLANES - 1 + j))
        col_seqs.append(jnp.concatenate(pieces, axis=0))
        state[:, lanes] = _row_bcast(h2_buf.at[c], SUBLANES - 1 + groups)
    hseq = jnp.concatenate(col_seqs, axis=1)

    gate = pbuf[pl.ds(r0, rows), REC_WIDTH:2 * REC_WIDTH]
    q = pbuf[pl.ds(r0, rows), 2 * REC_WIDTH:2 * REC_WIDTH + ATTN_WIDTH]
    k = pbuf[pl.ds(r0, rows), 2 * REC_WIDTH + ATTN_WIDTH:2 * REC_WIDTH + ATTN_WIDTH + KV_WIDTH]
    v = pbuf[pl.ds(r0, rows), 2 * REC_WIDTH + ATTN_WIDTH + KV_WIDTH:D_IN]
    yrec_ref[out_rows, :] = _gelu_gate_norm(hseq, gate, grec_ref)
    cos, s1, s2 = cos_ref[out_rows, :], s1_ref[out_rows, :], s2_ref[out_rows, :]
    q_ref[out_rows, :] = (_rope(q, cos, s1, s2) * (HEAD_DIM ** -0.5)).astype(BF16)
    k4_ref[out_rows, :] = _split_kv(_rope(k, cos, s1, s2))
    v4_ref[out_rows, :] = _split_kv(v)


def _ffn_mix_a_kernel(tiles_per_seq, x_ref, wg_ref, wu_ref, wd_ref, lng_ref, lnb_ref, win_ref,
                      cw_ref, cb_ref, wa_ref, ba_ref, wx_ref, bx_ref, lam_ref, grec_ref, cos_ref,
                      s1_ref, s2_ref, ctail_ref, hinit_ref,
                      h1_ref, yrec_ref, q_ref, k4_ref, v4_ref,
                      stash, pbuf, state, a1_buf, b1_buf, h2_buf):
    rows = x_ref.shape[0]
    step = pl.program_id(0)

    @pl.when(step == 0)
    def _():
        stash[...] = jnp.zeros(stash.shape, F32)
        pbuf[pl.ds(rows, SUBLANES), 0:REC_WIDTH] = jnp.zeros((SUBLANES, REC_WIDTH), F32)
        state[...] = jnp.zeros(state.shape, F32)

    first = lax.rem(step - 1, tiles_per_seq) == 0
    pbuf[0:SUBLANES, 0:REC_WIDTH] = jnp.where(
        first, ctail_ref[...], pbuf[pl.ds(rows, SUBLANES), 0:REC_WIDTH])
    state[...] = jnp.where(first, hinit_ref[...], state[...])
    pbuf[pl.ds(SUBLANES, rows), :] = _dot(stash[...].astype(BF16), win_ref[...])

    x = x_ref[...]
    xb = x.astype(BF16)
    chunks = D_FF // FF_CHUNK
    acc = None
    gate_up = _ffn_gate_up(xb, 0, wg_ref, wu_ref)
    for c in range(chunks):
        nxt = _ffn_gate_up(xb, c + 1, wg_ref, wu_ref) if c + 1 < chunks else None
        if c < rows // MIX_A_BLOCK:
            _mix_a_block(c, pbuf, state, a1_buf, b1_buf, h2_buf, cw_ref, cb_ref, wa_ref, ba_ref,
                         wx_ref, bx_ref, lam_ref, grec_ref, cos_ref, s1_ref, s2_ref, yrec_ref,
                         q_ref, k4_ref, v4_ref)
        part = _ffn_down(gate_up, c, wd_ref)
        acc = part if acc is None else acc + part
        gate_up = nxt
    h1 = _layer_norm(DEEPNORM_ALPHA * x + 0.5 * acc, lng_ref[...], lnb_ref[...])
    h1_ref[...] = h1
    stash[...] = h1


def _ffn_mix_a(x, wg, wu, wd, ln_g, ln_b, w_in, conv_w, conv_b, wa, ba, wx, bx, lam, g_rec,
               cos, s1, s2, ctail, hinit, layer, seq):
    rows = MIX_A_ROWS
    tokens = x.shape[0]
    tiles = tokens // rows
    tiles_per_seq = seq // rows
    cur = lambda width: pl.BlockSpec((rows, width), lambda i: (jnp.minimum(i, tiles - 1), 0))
    prv = lambda width: pl.BlockSpec((rows, width), lambda i: (jnp.maximum(i - 1, 0), 0))
    vec = lambda width: pl.BlockSpec((None, 1, width), lambda i: (layer, 0, 0))
    resident = lambda a, b: pl.BlockSpec((None, a, b), lambda i: (layer, 0, 0),
                                         pipeline_mode=pl.Buffered(1))
    gates = pl.BlockSpec((None, 2, GATE_HALF, GATE_HALF), lambda i: (layer, 0, 0, 0))
    table = pl.BlockSpec(
        (rows, LANES), lambda i: (lax.rem(jnp.maximum(i - 1, 0), tiles_per_seq), 0))
    seed = pl.BlockSpec((SUBLANES, REC_WIDTH), lambda i: (0, 0))
    groups = MIX_A_BLOCK // SUBLANES
    ncol = REC_WIDTH // LANES
    return pl.pallas_call(
        functools.partial(_ffn_mix_a_kernel, tiles_per_seq),
        grid=(tiles + 1,),
        in_specs=[
            cur(D_MODEL), resident(D_MODEL, D_FF), resident(D_MODEL, D_FF),
            resident(D_FF, D_MODEL), vec(D_MODEL), vec(D_MODEL),
            resident(D_MODEL, D_IN),
            pl.BlockSpec((None, CONV_WIDTH, REC_WIDTH), lambda i: (layer, 0, 0)),
            vec(REC_WIDTH), gates, vec(REC_WIDTH), gates, vec(REC_WIDTH), vec(REC_WIDTH),
            vec(REC_WIDTH), table, table, table, seed, seed,
        ],
        out_specs=[cur(D_MODEL), prv(REC_WIDTH), prv(ATTN_WIDTH), prv(4 * LANES),
                   prv(4 * LANES)],
        out_shape=[jax.ShapeDtypeStruct((tokens, D_MODEL), F32),
                   jax.ShapeDtypeStruct((tokens, REC_WIDTH), BF16),
                   jax.ShapeDtypeStruct((tokens, ATTN_WIDTH), BF16),
                   jax.ShapeDtypeStruct((tokens, 4 * LANES), BF16),
                   jax.ShapeDtypeStruct((tokens, 4 * LANES), BF16)],
        scratch_shapes=[
            pltpu.VMEM((rows, D_MODEL), F32),
            pltpu.VMEM((SUBLANES + rows, D_IN), F32),
            pltpu.VMEM((SUBLANES, REC_WIDTH), F32),
            pltpu.VMEM((ncol, MIX_A_BLOCK, LANES), F32),
            pltpu.VMEM((ncol, MIX_A_BLOCK, LANES), F32),
            pltpu.VMEM((ncol, SUBLANES + groups, LANES), F32),
        ],
        compiler_params=pltpu.CompilerParams(
            dimension_semantics=("arbitrary",), vmem_limit_bytes=VMEM_LIMIT_BYTES),
    )(x, wg, wu, wd, ln_g, ln_b, w_in, conv_w, conv_b, wa, ba, wx, bx, lam, g_rec, cos, s1, s2,
      ctail, hinit)


def _mix_b_kernel(layer, sink_ref, q_ref, k4_ref, v4_ref, kp_ref, vp_ref, km_ref, vm_ref,
                  yrec_ref, h_ref, wout_ref, gattn_ref, lng_ref, lnb_ref, o_ref):
    rows = q_ref.shape[0]
    first = pl.program_id(1) == 0
    qi = lax.broadcasted_iota(jnp.int32, (WINDOW, 2 * WINDOW), 0)
    kj = lax.broadcasted_iota(jnp.int32, (WINDOW, 2 * WINDOW), 1)
    band_ok = (kj > qi) & (kj <= qi + WINDOW)
    band_ok_first = band_ok & (kj >= jnp.where(first, WINDOW, 0))
    lane32 = lax.broadcasted_iota(jnp.int32, (WINDOW, 2 * N_META), 1)
    meta_even = lane32 < N_META
    meta_odd = lane32 >= N_META

    blocks = []
    for jb in range(rows // WINDOW):
        cur = slice(jb * WINDOW, (jb + 1) * WINDOW)
        prv = slice((jb - 1) * WINDOW, jb * WINDOW)
        ok = band_ok_first if jb == 0 else band_ok
        cols = []
        for g in range(N_KV_HEADS):
            lo_c, hi_c = 2 * g, 2 * g + 1
            if jb == 0:
                k_lo_prev, k_hi_prev = _col(kp_ref, lo_c), _col(kp_ref, hi_c)
                v_lo_prev, v_hi_prev = _col(vp_ref, lo_c), _col(vp_ref, hi_c)
            else:
                k_lo_prev, k_hi_prev = k4_ref[prv, lo_c * LANES:(lo_c + 1) * LANES], k4_ref[prv, hi_c * LANES:(hi_c + 1) * LANES]
                v_lo_prev, v_hi_prev = v4_ref[prv, lo_c * LANES:(lo_c + 1) * LANES], v4_ref[prv, hi_c * LANES:(hi_c + 1) * LANES]
            k_band = jnp.concatenate(
                [k_lo_prev, k4_ref[cur, lo_c * LANES:(lo_c + 1) * LANES],
                 k_hi_prev, k4_ref[cur, hi_c * LANES:(hi_c + 1) * LANES]], axis=0)
            v_band = jnp.concatenate(
                [v_lo_prev, v4_ref[cur, lo_c * LANES:(lo_c + 1) * LANES],
                 v_hi_prev, v4_ref[cur, hi_c * LANES:(hi_c + 1) * LANES]], axis=0)
            k_meta = jnp.concatenate([_col(km_ref, lo_c), _col(km_ref, hi_c)], axis=0)
            v_meta = jnp.concatenate([_col(vm_ref, lo_c), _col(vm_ref, hi_c)], axis=0)
            lhs = jnp.concatenate([q_ref[cur, lo_c * LANES:(lo_c + 1) * LANES],
                                   q_ref[cur, hi_c * LANES:(hi_c + 1) * LANES]], axis=0)
            s_band = _dot_nt(lhs, k_band)
            s_meta = _dot_nt(lhs, k_meta)
            for c in range(2):
                rs = slice(c * WINDOW, (c + 1) * WINDOW)
                head = 4 * g + 2 * c
                cols.append(_head_pair(
                    s_band[rs], ok, s_meta[rs], meta_even, meta_odd,
                    sink_ref[layer, head], sink_ref[layer, head + 1], v_band, v_meta))
        y_attn = jnp.concatenate(cols, axis=1)
        blocks.append(_rms_norm(y_attn, gattn_ref[...]).astype(BF16))
    y_attn = blocks[0] if len(blocks) == 1 else jnp.concatenate(blocks, axis=0)
    mixed = _dot(jnp.concatenate([yrec_ref[...], y_attn], axis=1), wout_ref[...])
    o_ref[...] = _layer_norm(DEEPNORM_ALPHA * h_ref[...] + mixed, lng_ref[...], lnb_ref[...])


def _mix_b(sinks, q, k4, v4, km4, vm4, yrec, h, w_out, g_attn, ln_g, ln_b, layer, batch, seq):
    rows = MIX_B_ROWS
    nc = seq // rows
    blocks_per_step = rows // WINDOW
    blocks_per_seq = seq // WINDOW
    tokens = batch * seq
    tok = lambda width: pl.BlockSpec((rows, width), lambda bi, ci: (bi * nc + ci, 0))
    prev = pl.BlockSpec(
        (WINDOW, 4 * LANES),
        lambda bi, ci: (jnp.maximum(bi * blocks_per_seq + ci * blocks_per_step - 1, 0), 0))
    meta = pl.BlockSpec((N_META, 4 * LANES), lambda bi, ci: (0, 0))
    vec = lambda width: pl.BlockSpec((None, 1, width), lambda bi, ci: (layer, 0, 0))
    return pl.pallas_call(
        functools.partial(_mix_b_kernel, layer),
        grid=(batch, nc),
        in_specs=[
            pl.BlockSpec(memory_space=pltpu.SMEM),
            tok(ATTN_WIDTH), tok(4 * LANES), tok(4 * LANES), prev, prev, meta, meta,
            tok(REC_WIDTH), tok(D_MODEL),
            pl.BlockSpec((None, D_MIX, D_MODEL), lambda bi, ci: (layer, 0, 0),
                         pipeline_mode=pl.Buffered(1)),
            vec(ATTN_WIDTH), vec(D_MODEL), vec(D_MODEL),
        ],
        out_specs=tok(D_MODEL),
        out_shape=jax.ShapeDtypeStruct((tokens, D_MODEL), F32),
        compiler_params=pltpu.CompilerParams(
            dimension_semantics=("arbitrary", "arbitrary"), vmem_limit_bytes=VMEM_LIMIT_BYTES),
    )(sinks, q, k4, v4, k4, v4, km4, vm4, yrec, h, w_out, g_attn, ln_g, ln_b)


def _meta_mix_kernel(layer, sink_ref, h_ref, win_ref, cw_ref, cb_ref, wa_ref, ba_ref, wx_ref,
                     bx_ref, lam_ref, grec_ref, cos_ref, s1_ref, s2_ref, wout_ref, gattn_ref,
                     lng_ref, lnb_ref,
                     o_ref, ctail_ref, hstate_ref, k4_ref, v4_ref, xbuf, hbuf):
    rows = N_META
    h_in = h_ref[...]
    proj = _dot(h_in.astype(BF16), win_ref[...])
    xr = proj[:, :REC_WIDTH]
    gate = proj[:, REC_WIDTH:2 * REC_WIDTH]
    q = proj[:, 2 * REC_WIDTH:2 * REC_WIDTH + ATTN_WIDTH]
    k = proj[:, 2 * REC_WIDTH + ATTN_WIDTH:2 * REC_WIDTH + ATTN_WIDTH + KV_WIDTH]
    v = proj[:, 2 * REC_WIDTH + ATTN_WIDTH + KV_WIDTH:]

    xbuf[0:SUBLANES, :] = jnp.zeros((SUBLANES, REC_WIDTH), F32)
    xbuf[pl.ds(SUBLANES, rows), :] = xr
    back = lambda k: xbuf[pl.ds(SUBLANES - k, rows), :]
    a, b = _conv_and_gates(xr, back, cw_ref, cb_ref, wa_ref, ba_ref, wx_ref, bx_ref, lam_ref)
    ctail_ref[...] = xr[rows - SUBLANES:, :]

    a1, b1 = _scan8(a, b)
    hbuf[0:SUBLANES, :] = b1[:SUBLANES]
    hbuf[SUBLANES:, :] = b1[SUBLANES:] + a1[SUBLANES:] * _row_bcast(hbuf, SUBLANES - 1)
    hseq = hbuf[...]
    hstate_ref[...] = _row_bcast(hbuf, rows - 1)
    y_rec = _gelu_gate_norm(hseq, gate, grec_ref)

    cos, s1, s2 = cos_ref[...], s1_ref[...], s2_ref[...]
    qb = (_rope(q, cos, s1, s2) * (HEAD_DIM ** -0.5)).astype(BF16)
    k4 = _split_kv(_rope(k, cos, s1, s2))
    v4 = _split_kv(v)
    k4_ref[...] = k4
    v4_ref[...] = v4

    qi = lax.broadcasted_iota(jnp.int32, (rows, 2 * N_META), 0)
    lane32 = lax.broadcasted_iota(jnp.int32, (rows, 2 * N_META), 1)
    meta_even = (lane32 < N_META) & (lane32 <= qi)
    meta_odd = (lane32 >= N_META) & (lane32 - N_META <= qi)
    cols = []
    for g in range(N_KV_HEADS):
        lo_c, hi_c = 2 * g, 2 * g + 1
        k_meta = jnp.concatenate([_col(k4, lo_c), _col(k4, hi_c)], axis=0)
        v_meta = jnp.concatenate([_col(v4, lo_c), _col(v4, hi_c)], axis=0)
        lhs = jnp.concatenate([_col(qb, lo_c), _col(qb, hi_c)], axis=0)
        s_meta = _dot_nt(lhs, k_meta)
        for c in range(2):
            head = 4 * g + 2 * c
            cols.append(_head_pair(
                None, None, s_meta[c * rows:(c + 1) * rows], meta_even, meta_odd,
                sink_ref[layer, head], sink_ref[layer, head + 1], None, v_meta))
    y_attn = _rms_norm(jnp.concatenate(cols, axis=1), gattn_ref[...]).astype(BF16)
    mixed = _dot(jnp.concatenate([y_rec, y_attn], axis=1), wout_ref[...])
    o_ref[...] = _layer_norm(DEEPNORM_ALPHA * h_in + mixed, lng_ref[...], lnb_ref[...])


def _meta_mix(sinks, h, w_in, conv_w, conv_b, wa, ba, wx, bx, lam, g_rec, cos, s1, s2, w_out,
              g_attn, ln_g, ln_b, layer):
    vec = lambda width: pl.BlockSpec((None, 1, width), lambda i: (layer, 0, 0))
    gates = pl.BlockSpec((None, 2, GATE_HALF, GATE_HALF), lambda i: (layer, 0, 0, 0))
    table = pl.BlockSpec((N_META, LANES), lambda i: (0, 0))
    full = lambda r, w: pl.BlockSpec((r, w), lambda i: (0, 0))
    return pl.pallas_call(
        functools.partial(_meta_mix_kernel, layer),
        grid=(1,),
        in_specs=[
            pl.BlockSpec(memory_space=pltpu.SMEM),
            full(N_META, D_MODEL),
            pl.BlockSpec((None, D_MODEL, D_IN), lambda i: (layer, 0, 0)),
            pl.BlockSpec((None, CONV_WIDTH, REC_WIDTH), lambda i: (layer, 0, 0)),
            vec(REC_WIDTH), gates, vec(REC_WIDTH), gates, vec(REC_WIDTH), vec(REC_WIDTH),
            vec(REC_WIDTH), table, table, table,
            pl.BlockSpec((None, D_MIX, D_MODEL), lambda i: (layer, 0, 0)),
            vec(ATTN_WIDTH), vec(D_MODEL), vec(D_MODEL),
        ],
        out_specs=[full(N_META, D_MODEL), full(SUBLANES, REC_WIDTH), full(SUBLANES, REC_WIDTH),
                   full(N_META, 4 * LANES), full(N_META, 4 * LANES)],
        out_shape=[jax.ShapeDtypeStruct((N_META, D_MODEL), F32),
                   jax.ShapeDtypeStruct((SUBLANES, REC_WIDTH), F32),
                   jax.ShapeDtypeStruct((SUBLANES, REC_WIDTH), F32),
                   jax.ShapeDtypeStruct((N_META, 4 * LANES), BF16),
                   jax.ShapeDtypeStruct((N_META, 4 * LANES), BF16)],
        scratch_shapes=[pltpu.VMEM((SUBLANES + N_META, REC_WIDTH), F32),
                        pltpu.VMEM((N_META, REC_WIDTH), F32)],
        compiler_params=pltpu.CompilerParams(
            dimension_semantics=("arbitrary",), vmem_limit_bytes=VMEM_LIMIT_BYTES),
    )(sinks, h, w_in, conv_w, conv_b, wa, ba, wx, bx, lam, g_rec, cos, s1, s2, w_out, g_attn,
      ln_g, ln_b)


def _block_diag_halves(w):
    per_half = REC_BLOCKS // 2
    out = jnp.zeros((w.shape[0], 2, GATE_HALF, GATE_HALF), w.dtype)
    for blk in range(REC_BLOCKS):
        half, pos = divmod(blk, per_half)
        sl = slice(pos * REC_BLOCK_DIM, (pos + 1) * REC_BLOCK_DIM)
        out = out.at[:, half, sl, sl].set(w[:, blk])
    return out.astype(BF16)


def _rope_tables(first_pos, count):
    pos = first_pos + jnp.arange(count, dtype=F32)
    inv_freq = ROPE_THETA ** (-jnp.arange(0, ROPE_DIM, 2, dtype=F32) / ROPE_DIM)
    ang = pos[:, None] * inv_freq[None, :]
    cos, sin = jnp.cos(ang), jnp.sin(ang)
    half = ROPE_DIM // 2
    pad = HEAD_DIM - ROPE_DIM
    ones = jnp.ones((count, pad), F32)
    zeros_h = jnp.zeros((count, half), F32)
    zeros_p = jnp.zeros((count, pad), F32)
    c_head = jnp.concatenate([cos, cos, ones], axis=1)
    s1_head = jnp.concatenate([-sin, zeros_h, zeros_p], axis=1)
    s2_head = jnp.concatenate([zeros_h, sin, zeros_p], axis=1)
    two = lambda t: jnp.concatenate([t, t], axis=1)
    return two(c_head), two(s1_head), two(s2_head)


def kernel(x, meta_tokens, ffn1_w_gate, ffn1_w_up, ffn1_w_down, ln1_g, ln1_b, w_in, conv_w, conv_b, gate_a_w, gate_a_b, gate_x_w, gate_x_b, lru_lambda, attn_sinks, norm_rec_g, norm_attn_g, w_out, ln2_g, ln2_b, ffn2_w_gate, ffn2_w_up, ffn2_w_down, ln3_g, ln3_b):
    batch, seq, _ = x.shape
    assert seq % MIX_A_ROWS == 0 and seq % MIX_B_ROWS == 0 and (batch * seq) % FFN_ROWS == 0
    assert MIX_A_BLOCK == SUBLANES ** 2 and MIX_A_ROWS // MIX_A_BLOCK <= D_FF // FF_CHUNK

    bf = lambda w: w.astype(BF16)
    row = lambda p: p.reshape(p.shape[0], 1, p.shape[1])
    f1g, f1u, f1d = bf(ffn1_w_gate), bf(ffn1_w_up), bf(ffn1_w_down)
    f2g, f2u, f2d = bf(ffn2_w_gate), bf(ffn2_w_up), bf(ffn2_w_down)
    win, wout = bf(w_in), bf(w_out)
    wa, wx = _block_diag_halves(gate_a_w), _block_diag_halves(gate_x_w)
    l1g, l1b, l2g, l2b, l3g, l3b = map(row, (ln1_g, ln1_b, ln2_g, ln2_b, ln3_g, ln3_b))
    cb, ba, bx, lam, grec, gattn = map(
        row, (conv_b, gate_a_b, gate_x_b, lru_lambda, norm_rec_g, norm_attn_g))
    cos_m, s1_m, s2_m = _rope_tables(0.0, N_META)
    cos_x, s1_x, s2_x = _rope_tables(float(N_META), seq)

    h = x.reshape(batch * seq, D_MODEL)
    hm = meta_tokens.astype(x.dtype)
    for l in range(DEPTH):
        hm = _ffn(hm, f1g, f1u, f1d, l1g, l1b, l, N_META)
        hm, ctail, hinit, km4, vm4 = _meta_mix(
            attn_sinks, hm, win, conv_w, cb, wa, ba, wx, bx, lam, grec, cos_m, s1_m, s2_m,
            wout, gattn, l2g, l2b, l)
        h, yrec, q, k4, v4 = _ffn_mix_a(
            h, f1g, f1u, f1d, l1g, l1b, win, conv_w, cb, wa, ba, wx, bx, lam, grec,
            cos_x, s1_x, s2_x, ctail, hinit, l, seq)
        h = _mix_b(attn_sinks, q, k4, v4, km4, vm4, yrec, h, wout, gattn, l2g, l2b,
                   l, batch, seq)
        if l + 1 < DEPTH:
            hm = _ffn(hm, f2g, f2u, f2d, l3g, l3b, l, N_META)
        h = _ffn(h, f2g, f2u, f2d, l3g, l3b, l, FFN_ROWS)
    return h.reshape(batch, seq, D_MODEL)
```

```python
import functools

import jax
import jax.numpy as jnp
from jax import lax
from jax.experimental import pallas as pl
from jax.experimental.pallas import tpu as pltpu

F32 = jnp.float32
BF16 = jnp.bfloat16

D_MODEL = 1024
DEPTH = 4
N_META = 16
D_FF = 2816
REC_WIDTH = 512
REC_BLOCKS = 8
REC_BLOCK_DIM = REC_WIDTH // REC_BLOCKS
CONV_WIDTH = 4
LRU_C = 8.0
N_Q_HEADS = 8
N_KV_HEADS = 2
HEAD_DIM = 64
ATTN_WIDTH = N_Q_HEADS * HEAD_DIM
KV_WIDTH = N_KV_HEADS * HEAD_DIM
WINDOW = 128
ROPE_DIM = HEAD_DIM // 4
ROPE_THETA = 500000.0
D_IN = 2 * REC_WIDTH + ATTN_WIDTH + 2 * KV_WIDTH
D_MIX = REC_WIDTH + ATTN_WIDTH
DEEPNORM_ALPHA = (2.0 * DEPTH) ** 0.25
LN_EPS = 1e-5
RMS_EPS = 1e-6
NEG_INF = -1e30
Q_SCALE = HEAD_DIM ** -0.5

LANES = 128
SUBLANES = 8
MXU_DIM = 256
VMEM_LIMIT_BYTES = 56 * 1024 * 1024

FFN_ROWS = 512
FF_CHUNK = MXU_DIM
MIX_A_ROWS = 512
MIX_B_ROWS = 256
GATE_HALF = REC_WIDTH // 2
SERIES_CUTOFF = -1.0 / 16.0


def _layer_norm(y, g, b):
    mu = jnp.mean(y, axis=-1, keepdims=True)
    d = y - mu
    var = jnp.mean(d * d, axis=-1, keepdims=True)
    return d * lax.rsqrt(var + LN_EPS) * g + b


def _rms_norm(y, g):
    ms = jnp.mean(y * y, axis=-1, keepdims=True)
    return y * lax.rsqrt(ms + RMS_EPS) * g


def _dot(a, b):
    return jnp.dot(a, b, preferred_element_type=F32)


def _dot_nt(a, b):
    return lax.dot_general(a, b, (((1,), (1,)), ((), ())), preferred_element_type=F32)


def _ffn_kernel(x_ref, wg_ref, wu_ref, wd_ref, g_ref, b_ref, o_ref):
    x = x_ref[...]
    xb = x.astype(BF16)
    acc = None
    for c in range(D_FF // FF_CHUNK):
        sl = slice(c * FF_CHUNK, (c + 1) * FF_CHUNK)
        gate = _dot(xb, wg_ref[:, sl])
        up = _dot(xb, wu_ref[:, sl])
        act = (gate * jax.nn.sigmoid(gate) * up).astype(BF16)
        part = _dot(act, wd_ref[sl, :])
        acc = part if acc is None else acc + part
    y = DEEPNORM_ALPHA * x + 0.5 * acc
    o_ref[...] = _layer_norm(y, g_ref[...], b_ref[...])


def _ffn(x, wg, wu, wd, g, b, layer, rows):
    tokens = x.shape[0]
    resident = dict(pipeline_mode=pl.Buffered(1))
    return pl.pallas_call(
        _ffn_kernel,
        grid=(tokens // rows,),
        in_specs=[
            pl.BlockSpec((rows, D_MODEL), lambda i: (i, 0)),
            pl.BlockSpec((None, D_MODEL, D_FF), lambda i: (layer, 0, 0), **resident),
            pl.BlockSpec((None, D_MODEL, D_FF), lambda i: (layer, 0, 0), **resident),
            pl.BlockSpec((None, D_FF, D_MODEL), lambda i: (layer, 0, 0), **resident),
            pl.BlockSpec((None, 1, D_MODEL), lambda i: (layer, 0, 0)),
            pl.BlockSpec((None, 1, D_MODEL), lambda i: (layer, 0, 0)),
        ],
        out_specs=pl.BlockSpec((rows, D_MODEL), lambda i: (i, 0)),
        out_shape=jax.ShapeDtypeStruct((tokens, D_MODEL), F32),
        compiler_params=pltpu.CompilerParams(
            dimension_semantics=("arbitrary",), vmem_limit_bytes=VMEM_LIMIT_BYTES),
    )(x, wg, wu, wd, g, b)


def _softplus(x):
    return jnp.maximum(x, 0.0) + jnp.log1p(jnp.exp(-jnp.abs(x)))


def _conv_and_gates(xr, back, cw_ref, cb_ref, wa_ref, ba_ref, wx_ref, bx_ref, lam_ref):
    xc = cb_ref[...] + back(3) * cw_ref[0:1, :]
    xc = xc + back(2) * cw_ref[1:2, :]
    xc = xc + back(1) * cw_ref[2:3, :]
    xc = xc + xr * cw_ref[3:4, :]
    xcb = xc.astype(BF16)
    lo, hi = xcb[:, :GATE_HALF], xcb[:, GATE_HALF:]
    r_lin = jnp.concatenate([_dot(lo, wa_ref[0]), _dot(hi, wa_ref[1])], axis=1) + ba_ref[...]
    i_lin = jnp.concatenate([_dot(lo, wx_ref[0]), _dot(hi, wx_ref[1])], axis=1) + bx_ref[...]
    r = jax.nn.sigmoid(r_lin)
    i = jax.nn.sigmoid(i_lin)
    log_a = (-LRU_C * _softplus(-lam_ref[...])) * r
    a = jnp.exp(log_a)
    y = 2.0 * log_a
    series = -y * (1.0 + y * (0.5 + y * (1.0 / 6.0 + y * (1.0 / 24.0))))
    one_minus_a2 = jnp.where(y > SERIES_CUTOFF, series, 1.0 - a * a)
    b = jnp.sqrt(one_minus_a2) * (i * xc)
    return a, b


def _scan8(a, b):
    rows, width = a.shape
    groups = rows // SUBLANES
    a3 = a.reshape(groups, SUBLANES, width)
    b3 = b.reshape(groups, SUBLANES, width)
    sub = lax.broadcasted_iota(jnp.int32, (groups, SUBLANES, width), 1)
    for s in (1, 2, 4):
        keep = sub >= s
        b_prev = jnp.where(keep, pltpu.roll(b3, s, 1), 0.0)
        a_prev = jnp.where(keep, pltpu.roll(a3, s, 1), 1.0)
        b3 = b3 + a3 * b_prev
        a3 = a3 * a_prev
    return a3.reshape(rows, width), b3.reshape(rows, width)


def _row_bcast(ref, r):
    return jnp.broadcast_to(ref[pl.ds(r, 1), :], (SUBLANES, ref.shape[1]))


def _rope(t, cos, s1, s2):
    cols = []
    for j in range(t.shape[1] // LANES):
        tj = t[:, j * LANES:(j + 1) * LANES]
        up = pltpu.roll(tj, LANES - ROPE_DIM // 2, 1)
        down = pltpu.roll(tj, ROPE_DIM // 2, 1)
        cols.append(tj * cos + up * s1 + down * s2)
    return cols[0] if len(cols) == 1 else jnp.concatenate(cols, axis=1)


def _split_kv(t):
    swapped = pltpu.roll(t, HEAD_DIM, 1)
    low = lax.broadcasted_iota(jnp.int32, t.shape, 1) < HEAD_DIM
    zero = jnp.zeros_like(t)
    parts = [jnp.where(low, t, zero), jnp.where(low, zero, swapped),
             jnp.where(low, swapped, zero), jnp.where(low, zero, t)]
    return jnp.concatenate(parts, axis=1).astype(BF16)


def _gelu_gate_norm(hseq, gate, grec_ref):
    y = hseq * jax.nn.gelu(gate, approximate=True)
    return _rms_norm(y, grec_ref[...]).astype(BF16)


def _pair_probs(s_band, band_ok, s_meta, meta_ok_even, meta_ok_odd, sink_even, sink_odd):
    rows = s_meta.shape[0]
    sme = jnp.where(meta_ok_even, s_meta, NEG_INF)
    smo = jnp.where(meta_ok_odd, s_meta, NEG_INF)
    m_e = jnp.maximum(jnp.max(sme, axis=-1, keepdims=True), sink_even)
    m_o = jnp.maximum(jnp.max(smo, axis=-1, keepdims=True), sink_odd)
    if s_band is not None:
        half = s_band.shape[1] // 2
        se = jnp.where(band_ok, s_band[:, :half], NEG_INF)
        so = jnp.where(band_ok, s_band[:, half:], NEG_INF)
        m_e = jnp.maximum(m_e, jnp.max(se, axis=-1, keepdims=True))
        m_o = jnp.maximum(m_o, jnp.max(so, axis=-1, keepdims=True))
    pme = jnp.exp(sme - m_e)
    pmo = jnp.exp(smo - m_o)
    den_e = jnp.sum(pme, axis=-1, keepdims=True) + jnp.exp(sink_even - m_e)
    den_o = jnp.sum(pmo, axis=-1, keepdims=True) + jnp.exp(sink_odd - m_o)
    p_band = None
    if s_band is not None:
        pe = jnp.exp(se - m_e)
        po = jnp.exp(so - m_o)
        den_e = den_e + jnp.sum(pe, axis=-1, keepdims=True)
        den_o = den_o + jnp.sum(po, axis=-1, keepdims=True)
        p_band = jnp.concatenate([pe, po], axis=1).astype(BF16)
    low = lax.broadcasted_iota(jnp.int32, (rows, LANES), 1) < HEAD_DIM
    inv = jnp.where(low, 1.0 / den_e, 1.0 / den_o)
    return p_band, (pme + pmo).astype(BF16), inv


def _pair_out(p_band, p_meta, inv, v_band, v_meta):
    out = _dot(p_meta, v_meta)
    if p_band is not None:
        out = out + _dot(p_band, v_band)
    return out * inv


def _col(ref_or_val, j, rows=slice(None)):
    return ref_or_val[rows, j * LANES:(j + 1) * LANES]


def _mix_a_kernel(h_ref, win_ref, cw_ref, cb_ref, wa_ref, ba_ref, wx_ref, bx_ref, lam_ref,
                  grec_ref, cosq_ref, s1q_ref, s2q_ref, cos_ref, s1_ref, s2_ref, ctail_ref,
                  hinit_ref,
                  yrec_ref, q_ref, k4_ref, v4_ref,
                  xbuf, state, a1_buf, b1_buf, a2_buf, b2_buf, h2_buf, h3_buf):
    rows = h_ref.shape[0]
    g1 = rows // SUBLANES
    g2 = g1 // SUBLANES

    @pl.when(pl.program_id(1) == 0)
    def _():
        xbuf[0:SUBLANES, :] = ctail_ref[...]
        state[...] = hinit_ref[...]

    proj = _dot(h_ref[...].astype(BF16), win_ref[...])
    xr = proj[:, :REC_WIDTH]
    gate = proj[:, REC_WIDTH:2 * REC_WIDTH]
    q = proj[:, 2 * REC_WIDTH:2 * REC_WIDTH + ATTN_WIDTH]
    k = proj[:, 2 * REC_WIDTH + ATTN_WIDTH:2 * REC_WIDTH + ATTN_WIDTH + KV_WIDTH]
    v = proj[:, 2 * REC_WIDTH + ATTN_WIDTH + KV_WIDTH:]

    xbuf[pl.ds(SUBLANES, rows), :] = xr
    back = lambda d: xbuf[pl.ds(SUBLANES - d, rows), :]
    a, b = _conv_and_gates(xr, back, cw_ref, cb_ref, wa_ref, ba_ref, wx_ref, bx_ref, lam_ref)
    xbuf[0:SUBLANES, :] = xr[rows - SUBLANES:, :]

    ends = lambda buf, c, n: buf[c, pl.ds(SUBLANES - 1, n, stride=SUBLANES), :]
    col_seqs = []
    for c in range(REC_WIDTH // LANES):
        lanes = slice(c * LANES, (c + 1) * LANES)
        h0 = state[:, lanes]
        a1, b1 = _scan8(a[:, lanes], b[:, lanes])
        a1_buf[c] = a1
        b1_buf[c] = b1
        a2, b2 = _scan8(ends(a1_buf, c, g1), ends(b1_buf, c, g1))
        a2_buf[c] = a2
        b2_buf[c] = b2
        a3, b3 = _scan8(ends(a2_buf, c, g2), ends(b2_buf, c, g2))
        h3_buf[c, 0:SUBLANES, :] = h0
        h3_buf[c, SUBLANES:, :] = b3 + a3 * h0
        h2_buf[c, 0:SUBLANES, :] = h0
        for j in range(g2):
            sl = slice(j * SUBLANES, (j + 1) * SUBLANES)
            h2_buf[c, pl.ds(SUBLANES + j * SUBLANES, SUBLANES), :] = (
                b2[sl] + a2[sl] * _row_bcast(h3_buf.at[c], SUBLANES - 1 + j))
        pieces = []
        for j in range(g1):
            sl = slice(j * SUBLANES, (j + 1) * SUBLANES)
            pieces.append(b1[sl] + a1[sl] * _row_bcast(h2_buf.at[c], SUBLANES - 1 + j))
        col_seqs.append(jnp.concatenate(pieces, axis=0))
        state[:, lanes] = _row_bcast(h2_buf.at[c], SUBLANES - 1 + g1)
    hseq = jnp.concatenate(col_seqs, axis=1)

    yrec_ref[...] = _gelu_gate_norm(hseq, gate, grec_ref)
    q_ref[...] = _rope(q, cosq_ref[...], s1q_ref[...], s2q_ref[...]).astype(BF16)
    k4_ref[...] = _split_kv(_rope(k, cos_ref[...], s1_ref[...], s2_ref[...]))
    v4_ref[...] = _split_kv(v)


def _mix_a(h, w_in, conv_w, conv_b, wa, ba, wx, bx, lam, g_rec, q_tables, k_tables, ctail,
           hinit, layer, batch, seq):
    rows = MIX_A_ROWS
    nc = seq // rows
    tokens = batch * seq
    tok = lambda width: pl.BlockSpec((rows, width), lambda bi, ci: (bi * nc + ci, 0))
    vec = lambda width: pl.BlockSpec((None, 1, width), lambda bi, ci: (layer, 0, 0))
    gates = pl.BlockSpec((None, 2, GATE_HALF, GATE_HALF), lambda bi, ci: (layer, 0, 0, 0))
    table = pl.BlockSpec((rows, LANES), lambda bi, ci: (ci, 0))
    seed = pl.BlockSpec((SUBLANES, REC_WIDTH), lambda bi, ci: (0, 0))
    g1 = rows // SUBLANES
    g2 = g1 // SUBLANES
    ncol = REC_WIDTH // LANES
    return pl.pallas_call(
        _mix_a_kernel,
        grid=(batch, nc),
        in_specs=[
            tok(D_MODEL),
            pl.BlockSpec((None, D_MODEL, D_IN), lambda bi, ci: (layer, 0, 0),
                         pipeline_mode=pl.Buffered(1)),
            pl.BlockSpec((None, CONV_WIDTH, REC_WIDTH), lambda bi, ci: (layer, 0, 0)),
            vec(REC_WIDTH), gates, vec(REC_WIDTH), gates, vec(REC_WIDTH), vec(REC_WIDTH),
            vec(REC_WIDTH), table, table, table, table, table, table, seed, seed,
        ],
        out_specs=[tok(REC_WIDTH), tok(ATTN_WIDTH), tok(4 * LANES), tok(4 * LANES)],
        out_shape=[jax.ShapeDtypeStruct((tokens, REC_WIDTH), BF16),
                   jax.ShapeDtypeStruct((tokens, ATTN_WIDTH), BF16),
                   jax.ShapeDtypeStruct((tokens, 4 * LANES), BF16),
                   jax.ShapeDtypeStruct((tokens, 4 * LANES), BF16)],
        scratch_shapes=[
            pltpu.VMEM((SUBLANES + rows, REC_WIDTH), F32),
            pltpu.VMEM((SUBLANES, REC_WIDTH), F32),
            pltpu.VMEM((ncol, rows, LANES), F32),
            pltpu.VMEM((ncol, rows, LANES), F32),
            pltpu.VMEM((ncol, g1, LANES), F32),
            pltpu.VMEM((ncol, g1, LANES), F32),
            pltpu.VMEM((ncol, SUBLANES + g1, LANES), F32),
            pltpu.VMEM((ncol, SUBLANES + g2, LANES), F32),
        ],
        compiler_params=pltpu.CompilerParams(
            dimension_semantics=("arbitrary", "arbitrary"), vmem_limit_bytes=VMEM_LIMIT_BYTES),
    )(h, w_in, conv_w, conv_b, wa, ba, wx, bx, lam, g_rec, *q_tables, *k_tables, ctail, hinit)


def _mix_b_kernel(layer, sink_ref, q_ref, k4_ref, v4_ref, kp_ref, vp_ref, km_ref, vm_ref,
                  yrec_ref, h_ref, wout_ref, gattn_ref, lng_ref, lnb_ref, o_ref):
    rows = q_ref.shape[0]
    first = pl.program_id(1) == 0
    qi = lax.broadcasted_iota(jnp.int32, (WINDOW, 2 * WINDOW), 0)
    kj = lax.broadcasted_iota(jnp.int32, (WINDOW, 2 * WINDOW), 1)
    band_ok = (kj > qi) & (kj <= qi + WINDOW)
    band_ok_first = band_ok & (kj >= jnp.where(first, WINDOW, 0))
    lane32 = lax.broadcasted_iota(jnp.int32, (WINDOW, 2 * N_META), 1)
    meta_even = lane32 < N_META
    meta_odd = lane32 >= N_META
    items = [(jb, g) for jb in range(rows // WINDOW) for g in range(N_KV_HEADS)]

    def band(ref, prev_ref, jb, g):
        cur = slice(jb * WINDOW, (jb + 1) * WINDOW)
        prv = slice((jb - 1) * WINDOW, jb * WINDOW)
        parts = []
        for cc in (2 * g, 2 * g + 1):
            parts.append(_col(prev_ref, cc) if jb == 0 else _col(ref, cc, prv))
            parts.append(_col(ref, cc, cur))
        return jnp.concatenate(parts, axis=0)

    def meta(ref, g):
        return jnp.concatenate([_col(ref, 2 * g), _col(ref, 2 * g + 1)], axis=0)

    scores = {}
    for jb, g in items:
        cur = slice(jb * WINDOW, (jb + 1) * WINDOW)
        lhs = jnp.concatenate([_col(q_ref, 2 * g, cur), _col(q_ref, 2 * g + 1, cur)], axis=0)
        scores[jb, g] = (_dot_nt(lhs, band(k4_ref, kp_ref, jb, g)),
                         _dot_nt(lhs, meta(km_ref, g)))
    probs = {}
    for jb, g in items:
        s_band, s_meta = scores[jb, g]
        ok = band_ok_first if jb == 0 else band_ok
        for c in range(2):
            rs = slice(c * WINDOW, (c + 1) * WINDOW)
            head = 4 * g + 2 * c
            probs[jb, g, c] = _pair_probs(s_band[rs], ok, s_meta[rs], meta_even, meta_odd,
                                          sink_ref[layer, head], sink_ref[layer, head + 1])
    outs = {}
    for jb, g in items:
        v_band, v_meta = band(v4_ref, vp_ref, jb, g), meta(vm_ref, g)
        for c in range(2):
            outs[jb, 2 * g + c] = _pair_out(*probs[jb, g, c], v_band, v_meta)
    blocks = []
    for jb in range(rows // WINDOW):
        y_attn = jnp.concatenate([outs[jb, cc] for cc in range(2 * N_KV_HEADS)], axis=1)
        blocks.append(_rms_norm(y_attn, gattn_ref[...]).astype(BF16))
    y_attn = blocks[0] if len(blocks) == 1 else jnp.concatenate(blocks, axis=0)
    mixed = _dot(jnp.concatenate([yrec_ref[...], y_attn], axis=1), wout_ref[...])
    o_ref[...] = _layer_norm(DEEPNORM_ALPHA * h_ref[...] + mixed, lng_ref[...], lnb_ref[...])


def _mix_b(sinks, q, k4, v4, km4, vm4, yrec, h, w_out, g_attn, ln_g, ln_b, layer, batch, seq):
    rows = MIX_B_ROWS
    nc = seq // rows
    blocks_per_step = rows // WINDOW
    blocks_per_seq = seq // WINDOW
    tokens = batch * seq
    tok = lambda width: pl.BlockSpec((rows, width), lambda bi, ci: (bi * nc + ci, 0))
    prev = pl.BlockSpec(
        (WINDOW, 4 * LANES),
        lambda bi, ci: (jnp.maximum(bi * blocks_per_seq + ci * blocks_per_step - 1, 0), 0))
    meta = pl.BlockSpec((N_META, 4 * LANES), lambda bi, ci: (0, 0))
    vec = lambda width: pl.BlockSpec((None, 1, width), lambda bi, ci: (layer, 0, 0))
    return pl.pallas_call(
        functools.partial(_mix_b_kernel, layer),
        grid=(batch, nc),
        in_specs=[
            pl.BlockSpec(memory_space=pltpu.SMEM),
            tok(ATTN_WIDTH), tok(4 * LANES), tok(4 * LANES), prev, prev, meta, meta,
            tok(REC_WIDTH), tok(D_MODEL),
            pl.BlockSpec((None, D_MIX, D_MODEL), lambda bi, ci: (layer, 0, 0),
                         pipeline_mode=pl.Buffered(1)),
            vec(ATTN_WIDTH), vec(D_MODEL), vec(D_MODEL),
        ],
        out_specs=tok(D_MODEL),
        out_shape=jax.ShapeDtypeStruct((tokens, D_MODEL), F32),
        compiler_params=pltpu.CompilerParams(
            dimension_semantics=("arbitrary", "arbitrary"), vmem_limit_bytes=VMEM_LIMIT_BYTES),
    )(sinks, q, k4, v4, k4, v4, km4, vm4, yrec, h, w_out, g_attn, ln_g, ln_b)


def _meta_mix_kernel(layer, sink_ref, h_ref, win_ref, cw_ref, cb_ref, wa_ref, ba_ref, wx_ref,
                     bx_ref, lam_ref, grec_ref, cosq_ref, s1q_ref, s2q_ref, cos_ref, s1_ref,
                     s2_ref, wout_ref, gattn_ref, lng_ref, lnb_ref,
                     o_ref, ctail_ref, hstate_ref, k4_ref, v4_ref, xbuf, hbuf):
    rows = N_META
    h_in = h_ref[...]
    proj = _dot(h_in.astype(BF16), win_ref[...])
    xr = proj[:, :REC_WIDTH]
    gate = proj[:, REC_WIDTH:2 * REC_WIDTH]
    q = proj[:, 2 * REC_WIDTH:2 * REC_WIDTH + ATTN_WIDTH]
    k = proj[:, 2 * REC_WIDTH + ATTN_WIDTH:2 * REC_WIDTH + ATTN_WIDTH + KV_WIDTH]
    v = proj[:, 2 * REC_WIDTH + ATTN_WIDTH + KV_WIDTH:]

    xbuf[0:SUBLANES, :] = jnp.zeros((SUBLANES, REC_WIDTH), F32)
    xbuf[pl.ds(SUBLANES, rows), :] = xr
    back = lambda d: xbuf[pl.ds(SUBLANES - d, rows), :]
    a, b = _conv_and_gates(xr, back, cw_ref, cb_ref, wa_ref, ba_ref, wx_ref, bx_ref, lam_ref)
    ctail_ref[...] = xr[rows - SUBLANES:, :]

    a1, b1 = _scan8(a, b)
    hbuf[0:SUBLANES, :] = b1[:SUBLANES]
    hbuf[SUBLANES:, :] = b1[SUBLANES:] + a1[SUBLANES:] * _row_bcast(hbuf, SUBLANES - 1)
    hseq = hbuf[...]
    hstate_ref[...] = _row_bcast(hbuf, rows - 1)
    y_rec = _gelu_gate_norm(hseq, gate, grec_ref)

    qb = _rope(q, cosq_ref[...], s1q_ref[...], s2q_ref[...]).astype(BF16)
    k4 = _split_kv(_rope(k, cos_ref[...], s1_ref[...], s2_ref[...]))
    v4 = _split_kv(v)
    k4_ref[...] = k4
    v4_ref[...] = v4

    qi = lax.broadcasted_iota(jnp.int32, (rows, 2 * N_META), 0)
    lane32 = lax.broadcasted_iota(jnp.int32, (rows, 2 * N_META), 1)
    meta_even = (lane32 < N_META) & (lane32 <= qi)
    meta_odd = (lane32 >= N_META) & (lane32 - N_META <= qi)
    cols = []
    for g in range(N_KV_HEADS):
        lo_c, hi_c = 2 * g, 2 * g + 1
        k_meta = jnp.concatenate([_col(k4, lo_c), _col(k4, hi_c)], axis=0)
        v_meta = jnp.concatenate([_col(v4, lo_c), _col(v4, hi_c)], axis=0)
        lhs = jnp.concatenate([_col(qb, lo_c), _col(qb, hi_c)], axis=0)
        s_meta = _dot_nt(lhs, k_meta)
        for c in range(2):
            head = 4 * g + 2 * c
            pieces = _pair_probs(None, None, s_meta[c * rows:(c + 1) * rows], meta_even, meta_odd,
                                 sink_ref[layer, head], sink_ref[layer, head + 1])
            cols.append(_pair_out(*pieces, None, v_meta))
    y_attn = _rms_norm(jnp.concatenate(cols, axis=1), gattn_ref[...]).astype(BF16)
    mixed = _dot(jnp.concatenate([y_rec, y_attn], axis=1), wout_ref[...])
    o_ref[...] = _layer_norm(DEEPNORM_ALPHA * h_in + mixed, lng_ref[...], lnb_ref[...])


def _meta_mix(sinks, h, w_in, conv_w, conv_b, wa, ba, wx, bx, lam, g_rec, q_tables, k_tables,
              w_out, g_attn, ln_g, ln_b, layer):
    vec = lambda width: pl.BlockSpec((None, 1, width), lambda i: (layer, 0, 0))
    gates = pl.BlockSpec((None, 2, GATE_HALF, GATE_HALF), lambda i: (layer, 0, 0, 0))
    table = pl.BlockSpec((N_META, LANES), lambda i: (0, 0))
    full = lambda r, w: pl.BlockSpec((r, w), lambda i: (0, 0))
    return pl.pallas_call(
        functools.partial(_meta_mix_kernel, layer),
        grid=(1,),
        in_specs=[
            pl.BlockSpec(memory_space=pltpu.SMEM),
            full(N_META, D_MODEL),
            pl.BlockSpec((None, D_MODEL, D_IN), lambda i: (layer, 0, 0)),
            pl.BlockSpec((None, CONV_WIDTH, REC_WIDTH), lambda i: (layer, 0, 0)),
            vec(REC_WIDTH), gates, vec(REC_WIDTH), gates, vec(REC_WIDTH), vec(REC_WIDTH),
            vec(REC_WIDTH), table, table, table, table, table, table,
            pl.BlockSpec((None, D_MIX, D_MODEL), lambda i: (layer, 0, 0)),
            vec(ATTN_WIDTH), vec(D_MODEL), vec(D_MODEL),
        ],
        out_specs=[full(N_META, D_MODEL), full(SUBLANES, REC_WIDTH), full(SUBLANES, REC_WIDTH),
                   full(N_META, 4 * LANES), full(N_META, 4 * LANES)],
        out_shape=[jax.ShapeDtypeStruct((N_META, D_MODEL), F32),
                   jax.ShapeDtypeStruct((SUBLANES, REC_WIDTH), F32),
                   jax.ShapeDtypeStruct((SUBLANES, REC_WIDTH), F32),
                   jax.ShapeDtypeStruct((N_META, 4 * LANES), BF16),
                   jax.ShapeDtypeStruct((N_META, 4 * LANES), BF16)],
        scratch_shapes=[pltpu.VMEM((SUBLANES + N_META, REC_WIDTH), F32),
                        pltpu.VMEM((N_META, REC_WIDTH), F32)],
        compiler_params=pltpu.CompilerParams(
            dimension_semantics=("arbitrary",), vmem_limit_bytes=VMEM_LIMIT_BYTES),
    )(sinks, h, w_in, conv_w, conv_b, wa, ba, wx, bx, lam, g_rec, *q_tables, *k_tables, w_out,
      g_attn, ln_g, ln_b)


def _block_diag_halves(w):
    per_half = REC_BLOCKS // 2
    out = jnp.zeros((w.shape[0], 2, GATE_HALF, GATE_HALF), w.dtype)
    for blk in range(REC_BLOCKS):
        half, pos = divmod(blk, per_half)
        sl = slice(pos * REC_BLOCK_DIM, (pos + 1) * REC_BLOCK_DIM)
        out = out.at[:, half, sl, sl].set(w[:, blk])
    return out.astype(BF16)


def _rope_tables(first_pos, count):
    pos = first_pos + jnp.arange(count, dtype=F32)
    inv_freq = ROPE_THETA ** (-jnp.arange(0, ROPE_DIM, 2, dtype=F32) / ROPE_DIM)
    ang = pos[:, None] * inv_freq[None, :]
    cos, sin = jnp.cos(ang), jnp.sin(ang)
    half = ROPE_DIM // 2
    pad = HEAD_DIM - ROPE_DIM
    ones = jnp.ones((count, pad), F32)
    zeros_h = jnp.zeros((count, half), F32)
    zeros_p = jnp.zeros((count, pad), F32)
    c_head = jnp.concatenate([cos, cos, ones], axis=1)
    s1_head = jnp.concatenate([-sin, zeros_h, zeros_p], axis=1)
    s2_head = jnp.concatenate([zeros_h, sin, zeros_p], axis=1)
    two = lambda t: jnp.concatenate([t, t], axis=1)
    return two(c_head), two(s1_head), two(s2_head)


def kernel(x, meta_tokens, ffn1_w_gate, ffn1_w_up, ffn1_w_down, ln1_g, ln1_b, w_in, conv_w, conv_b, gate_a_w, gate_a_b, gate_x_w, gate_x_b, lru_lambda, attn_sinks, norm_rec_g, norm_attn_g, w_out, ln2_g, ln2_b, ffn2_w_gate, ffn2_w_up, ffn2_w_down, ln3_g, ln3_b):
    batch, seq, _ = x.shape
    assert seq % MIX_A_ROWS == 0 and seq % MIX_B_ROWS == 0 and (batch * seq) % FFN_ROWS == 0
    assert MIX_A_ROWS == SUBLANES ** 3

    bf = lambda w: w.astype(BF16)
    row = lambda p: p.reshape(p.shape[0], 1, p.shape[1])
    f1g, f1u, f1d = bf(ffn1_w_gate), bf(ffn1_w_up), bf(ffn1_w_down)
    f2g, f2u, f2d = bf(ffn2_w_gate), bf(ffn2_w_up), bf(ffn2_w_down)
    win, wout = bf(w_in), bf(w_out)
    wa, wx = _block_diag_halves(gate_a_w), _block_diag_halves(gate_x_w)
    l1g, l1b, l2g, l2b, l3g, l3b = map(row, (ln1_g, ln1_b, ln2_g, ln2_b, ln3_g, ln3_b))
    cb, ba, bx, lam, grec, gattn = map(
        row, (conv_b, gate_a_b, gate_x_b, lru_lambda, norm_rec_g, norm_attn_g))
    k_tab_m = _rope_tables(0.0, N_META)
    k_tab_x = _rope_tables(float(N_META), seq)
    q_tab_m = tuple(t * Q_SCALE for t in k_tab_m)
    q_tab_x = tuple(t * Q_SCALE for t in k_tab_x)

    h = x.reshape(batch * seq, D_MODEL)
    hm = meta_tokens.astype(x.dtype)
    for l in range(DEPTH):
        hm = _ffn(hm, f1g, f1u, f1d, l1g, l1b, l, N_META)
        h = _ffn(h, f1g, f1u, f1d, l1g, l1b, l, FFN_ROWS)
        hm, ctail, hinit, km4, vm4 = _meta_mix(
            attn_sinks, hm, win, conv_w, cb, wa, ba, wx, bx, lam, grec, q_tab_m, k_tab_m,
            wout, gattn, l2g, l2b, l)
        yrec, q, k4, v4 = _mix_a(h, win, conv_w, cb, wa, ba, wx, bx, lam, grec,
                                 q_tab_x, k_tab_x, ctail, hinit, l, batch, seq)
        h = _mix_b(attn_sinks, q, k4, v4, km4, vm4, yrec, h, wout, gattn, l2g, l2b,
                   l, batch, seq)
        if l + 1 < DEPTH:
            hm = _ffn(hm, f2g, f2u, f2d, l3g, l3b, l, N_META)
        h = _ffn(h, f2g, f2u, f2d, l3g, l3b, l, FFN_ROWS)
    return h.reshape(batch, seq, D_MODEL)
```

```python
import functools

import jax
import jax.numpy as jnp
from jax import lax
from jax.experimental import pallas as pl
from jax.experimental.pallas import tpu as pltpu

F32 = jnp.float32
BF16 = jnp.bfloat16

D_MODEL = 1024
DEPTH = 4
N_META = 16
D_FF = 2816
REC_WIDTH = 512
REC_BLOCKS = 8
REC_BLOCK_DIM = REC_WIDTH // REC_BLOCKS
CONV_WIDTH = 4
LRU_C = 8.0
N_Q_HEADS = 8
N_KV_HEADS = 2
HEAD_DIM = 64
ATTN_WIDTH = N_Q_HEADS * HEAD_DIM
KV_WIDTH = N_KV_HEADS * HEAD_DIM
WINDOW = 128
ROPE_DIM = HEAD_DIM // 4
ROPE_THETA = 500000.0
D_IN = 2 * REC_WIDTH + ATTN_WIDTH + 2 * KV_WIDTH
D_MIX = REC_WIDTH + ATTN_WIDTH
DEEPNORM_ALPHA = (2.0 * DEPTH) ** 0.25
LN_EPS = 1e-5
RMS_EPS = 1e-6
NEG_INF = -1e30
Q_SCALE = HEAD_DIM ** -0.5

LANES = 128
SUBLANES = 8
MXU_DIM = 256
VMEM_LIMIT_BYTES = 56 * 1024 * 1024

FFN_ROWS = 1024
FFN_SUB_ROWS = 512
FF_CHUNK = MXU_DIM
MIX_A_ROWS = 512
MIX_B_ROWS = 256
GATE_HALF = REC_WIDTH // 2
SERIES_CUTOFF = -1.0 / 64.0


def _layer_norm(y, g, b):
    mu = jnp.mean(y, axis=-1, keepdims=True)
    d = y - mu
    var = jnp.mean(d * d, axis=-1, keepdims=True)
    return d * lax.rsqrt(var + LN_EPS) * g + b


def _rms_norm(y, g):
    ms = jnp.mean(y * y, axis=-1, keepdims=True)
    return y * lax.rsqrt(ms + RMS_EPS) * g


def _dot(a, b):
    return jnp.dot(a, b, preferred_element_type=F32)


def _dot_nt(a, b):
    return lax.dot_general(a, b, (((1,), (1,)), ((), ())), preferred_element_type=F32)


def _ffn_kernel(x_ref, wg_ref, wu_ref, wd_ref, g_ref, b_ref, o_ref):
    sub = min(x_ref.shape[0], FFN_SUB_ROWS)
    for r in range(x_ref.shape[0] // sub):
        rs = slice(r * sub, (r + 1) * sub)
        x = x_ref[rs, :]
        xb = x.astype(BF16)
        acc = None
        for c in range(D_FF // FF_CHUNK):
            sl = slice(c * FF_CHUNK, (c + 1) * FF_CHUNK)
            gate = _dot(xb, wg_ref[:, sl])
            up = _dot(xb, wu_ref[:, sl])
            act = (gate * jax.nn.sigmoid(gate) * up).astype(BF16)
            part = _dot(act, wd_ref[sl, :])
            acc = part if acc is None else acc + part
        y = DEEPNORM_ALPHA * x + 0.5 * acc
        o_ref[rs, :] = _layer_norm(y, g_ref[...], b_ref[...])


def _ffn(x, wg, wu, wd, g, b, layer, rows):
    tokens = x.shape[0]
    resident = dict(pipeline_mode=pl.Buffered(1))
    return pl.pallas_call(
        _ffn_kernel,
        grid=(tokens // rows,),
        in_specs=[
            pl.BlockSpec((rows, D_MODEL), lambda i: (i, 0)),
            pl.BlockSpec((None, D_MODEL, D_FF), lambda i: (layer, 0, 0), **resident),
            pl.BlockSpec((None, D_MODEL, D_FF), lambda i: (layer, 0, 0), **resident),
            pl.BlockSpec((None, D_FF, D_MODEL), lambda i: (layer, 0, 0), **resident),
            pl.BlockSpec((None, 1, D_MODEL), lambda i: (layer, 0, 0)),
            pl.BlockSpec((None, 1, D_MODEL), lambda i: (layer, 0, 0)),
        ],
        out_specs=pl.BlockSpec((rows, D_MODEL), lambda i: (i, 0)),
        out_shape=jax.ShapeDtypeStruct((tokens, D_MODEL), F32),
        compiler_params=pltpu.CompilerParams(
            dimension_semantics=("arbitrary",), vmem_limit_bytes=VMEM_LIMIT_BYTES),
    )(x, wg, wu, wd, g, b)


def _softplus(x):
    return jnp.maximum(x, 0.0) + jnp.log1p(jnp.exp(-jnp.abs(x)))


def _conv_and_gates(xr, back, cw_ref, cb_ref, wa_ref, ba_ref, wx_ref, bx_ref, lam_ref):
    xc = cb_ref[...] + back(3) * cw_ref[0:1, :]
    xc = xc + back(2) * cw_ref[1:2, :]
    xc = xc + back(1) * cw_ref[2:3, :]
    xc = xc + xr * cw_ref[3:4, :]
    xcb = xc.astype(BF16)
    lo, hi = xcb[:, :GATE_HALF], xcb[:, GATE_HALF:]
    r_lin = jnp.concatenate([_dot(lo, wa_ref[0]), _dot(hi, wa_ref[1])], axis=1) + ba_ref[...]
    i_lin = jnp.concatenate([_dot(lo, wx_ref[0]), _dot(hi, wx_ref[1])], axis=1) + bx_ref[...]
    r = jax.nn.sigmoid(r_lin)
    i = jax.nn.sigmoid(i_lin)
    log_a = (-LRU_C * _softplus(-lam_ref[...])) * r
    a = jnp.exp(log_a)
    y = 2.0 * log_a
    series = -y * (1.0 + y * (0.5 + y * (1.0 / 6.0)))
    one_minus_a2 = jnp.where(y > SERIES_CUTOFF, series, 1.0 - a * a)
    b = jnp.sqrt(one_minus_a2) * (i * xc)
    return a, b


def _delayed(xr, prev8, d):
    rolled = pltpu.roll(xr, d, 0)
    sub = lax.broadcasted_iota(jnp.int32, prev8.shape, 0)
    head = jnp.where(sub < d, pltpu.roll(prev8, d, 0), rolled[:SUBLANES])
    return jnp.concatenate([head, rolled[SUBLANES:]], axis=0)


def _scan8(a, b):
    rows, width = a.shape
    groups = rows // SUBLANES
    a3 = a.reshape(groups, SUBLANES, width)
    b3 = b.reshape(groups, SUBLANES, width)
    sub = lax.broadcasted_iota(jnp.int32, (groups, SUBLANES, width), 1)
    for s in (1, 2, 4):
        keep = sub >= s
        b_prev = jnp.where(keep, pltpu.roll(b3, s, 1), 0.0)
        a_prev = jnp.where(keep, pltpu.roll(a3, s, 1), 1.0)
        b3 = b3 + a3 * b_prev
        a3 = a3 * a_prev
    return a3.reshape(rows, width), b3.reshape(rows, width)


def _row_bcast(ref, r):
    return jnp.broadcast_to(ref[pl.ds(r, 1), :], (SUBLANES, ref.shape[1]))


def _rope(t, cos, s1, s2):
    cols = []
    for j in range(t.shape[1] // LANES):
        tj = t[:, j * LANES:(j + 1) * LANES]
        up = pltpu.roll(tj, LANES - ROPE_DIM // 2, 1)
        down = pltpu.roll(tj, ROPE_DIM // 2, 1)
        cols.append(tj * cos + up * s1 + down * s2)
    return cols[0] if len(cols) == 1 else jnp.concatenate(cols, axis=1)


def _split_kv(t):
    swapped = pltpu.roll(t, HEAD_DIM, 1)
    low = lax.broadcasted_iota(jnp.int32, t.shape, 1) < HEAD_DIM
    zero = jnp.zeros_like(t)
    parts = [jnp.where(low, t, zero), jnp.where(low, zero, swapped),
             jnp.where(low, swapped, zero), jnp.where(low, zero, t)]
    return jnp.concatenate(parts, axis=1).astype(BF16)


def _gelu_gate_norm(hseq, gate, grec_ref):
    y = hseq * jax.nn.gelu(gate, approximate=True)
    return _rms_norm(y, grec_ref[...]).astype(BF16)


def _pair_probs(s_band, band_ok, s_meta, meta_ok_even, meta_ok_odd, sink_even, sink_odd):
    rows = s_meta.shape[0]
    sme = jnp.where(meta_ok_even, s_meta, NEG_INF)
    smo = jnp.where(meta_ok_odd, s_meta, NEG_INF)
    m_e = jnp.maximum(jnp.max(sme, axis=-1, keepdims=True), sink_even)
    m_o = jnp.maximum(jnp.max(smo, axis=-1, keepdims=True), sink_odd)
    if s_band is not None:
        half = s_band.shape[1] // 2
        se = jnp.where(band_ok, s_band[:, :half], NEG_INF)
        so = jnp.where(band_ok, s_band[:, half:], NEG_INF)
        m_e = jnp.maximum(m_e, jnp.max(se, axis=-1, keepdims=True))
        m_o = jnp.maximum(m_o, jnp.max(so, axis=-1, keepdims=True))
    pme = jnp.exp(sme - m_e)
    pmo = jnp.exp(smo - m_o)
    den_e = jnp.sum(pme, axis=-1, keepdims=True) + jnp.exp(sink_even - m_e)
    den_o = jnp.sum(pmo, axis=-1, keepdims=True) + jnp.exp(sink_odd - m_o)
    p_band = None
    if s_band is not None:
        pe = jnp.exp(se - m_e)
        po = jnp.exp(so - m_o)
        den_e = den_e + jnp.sum(pe, axis=-1, keepdims=True)
        den_o = den_o + jnp.sum(po, axis=-1, keepdims=True)
        p_band = jnp.concatenate([pe, po], axis=1).astype(BF16)
    low = lax.broadcasted_iota(jnp.int32, (rows, LANES), 1) < HEAD_DIM
    inv = jnp.where(low, 1.0 / den_e, 1.0 / den_o)
    return p_band, (pme + pmo).astype(BF16), inv


def _pair_out(p_band, p_meta, inv, v_band, v_meta):
    out = _dot(p_meta, v_meta)
    if p_band is not None:
        out = out + _dot(p_band, v_band)
    return out * inv


def _col(ref_or_val, j, rows=slice(None)):
    return ref_or_val[rows, j * LANES:(j + 1) * LANES]


def _mix_a_kernel(h_ref, win_ref, cw_ref, cb_ref, wa_ref, ba_ref, wx_ref, bx_ref, lam_ref,
                  grec_ref, cosq_ref, s1q_ref, s2q_ref, cos_ref, s1_ref, s2_ref, ctail_ref,
                  hinit_ref,
                  yrec_ref, q_ref, k4_ref, v4_ref,
                  hist, state, a1_buf, b1_buf, a2_buf, b2_buf, h2_buf, h3_buf):
    rows = h_ref.shape[0]
    g1 = rows // SUBLANES
    g2 = g1 // SUBLANES

    @pl.when(pl.program_id(1) == 0)
    def _():
        hist[...] = ctail_ref[...]
        state[...] = hinit_ref[...]

    proj = _dot(h_ref[...].astype(BF16), win_ref[...])
    xr = proj[:, :REC_WIDTH]
    gate = proj[:, REC_WIDTH:2 * REC_WIDTH]
    q = proj[:, 2 * REC_WIDTH:2 * REC_WIDTH + ATTN_WIDTH]
    k = proj[:, 2 * REC_WIDTH + ATTN_WIDTH:2 * REC_WIDTH + ATTN_WIDTH + KV_WIDTH]
    v = proj[:, 2 * REC_WIDTH + ATTN_WIDTH + KV_WIDTH:]

    prev8 = hist[...]
    back = lambda d: _delayed(xr, prev8, d)
    a, b = _conv_and_gates(xr, back, cw_ref, cb_ref, wa_ref, ba_ref, wx_ref, bx_ref, lam_ref)
    hist[...] = xr[rows - SUBLANES:, :]

    ends = lambda buf, c, n: buf[c, pl.ds(SUBLANES - 1, n, stride=SUBLANES), :]
    col_seqs = []
    for c in range(REC_WIDTH // LANES):
        lanes = slice(c * LANES, (c + 1) * LANES)
        h0 = state[:, lanes]
        a1, b1 = _scan8(a[:, lanes], b[:, lanes])
        a1_buf[c] = a1
        b1_buf[c] = b1
        a2, b2 = _scan8(ends(a1_buf, c, g1), ends(b1_buf, c, g1))
        a2_buf[c] = a2
        b2_buf[c] = b2
        a3, b3 = _scan8(ends(a2_buf, c, g2), ends(b2_buf, c, g2))
        h3_buf[c, 0:SUBLANES, :] = h0
        h3_buf[c, SUBLANES:, :] = b3 + a3 * h0
        h2_buf[c, 0:SUBLANES, :] = h0
        for j in range(g2):
            sl = slice(j * SUBLANES, (j + 1) * SUBLANES)
            h2_buf[c, pl.ds(SUBLANES + j * SUBLANES, SUBLANES), :] = (
                b2[sl] + a2[sl] * _row_bcast(h3_buf.at[c], SUBLANES - 1 + j))
        pieces = []
        for j in range(g1):
            sl = slice(j * SUBLANES, (j + 1) * SUBLANES)
            pieces.append(b1[sl] + a1[sl] * _row_bcast(h2_buf.at[c], SUBLANES - 1 + j))
        col_seqs.append(jnp.concatenate(pieces, axis=0))
        state[:, lanes] = _row_bcast(h2_buf.at[c], SUBLANES - 1 + g1)
    hseq = jnp.concatenate(col_seqs, axis=1)

    yrec_ref[...] = _gelu_gate_norm(hseq, gate, grec_ref)
    q_ref[...] = _rope(q, cosq_ref[...], s1q_ref[...], s2q_ref[...]).astype(BF16)
    k4_ref[...] = _split_kv(_rope(k, cos_ref[...], s1_ref[...], s2_ref[...]))
    v4_ref[...] = _split_kv(v)


def _mix_a(h, w_in, conv_w, conv_b, wa, ba, wx, bx, lam, g_rec, q_tables, k_tables, ctail,
           hinit, layer, batch, seq):
    rows = MIX_A_ROWS
    nc = seq // rows
    tokens = batch * seq
    tok = lambda width: pl.BlockSpec((rows, width), lambda bi, ci: (bi * nc + ci, 0))
    vec = lambda width: pl.BlockSpec((None, 1, width), lambda bi, ci: (layer, 0, 0))
    gates = pl.BlockSpec((None, 2, GATE_HALF, GATE_HALF), lambda bi, ci: (layer, 0, 0, 0))
    table = pl.BlockSpec((rows, LANES), lambda bi, ci: (ci, 0))
    seed = pl.BlockSpec((SUBLANES, REC_WIDTH), lambda bi, ci: (0, 0))
    g1 = rows // SUBLANES
    g2 = g1 // SUBLANES
    ncol = REC_WIDTH // LANES
    return pl.pallas_call(
        _mix_a_kernel,
        grid=(batch, nc),
        in_specs=[
            tok(D_MODEL),
            pl.BlockSpec((None, D_MODEL, D_IN), lambda bi, ci: (layer, 0, 0),
                         pipeline_mode=pl.Buffered(1)),
            pl.BlockSpec((None, CONV_WIDTH, REC_WIDTH), lambda bi, ci: (layer, 0, 0)),
            vec(REC_WIDTH), gates, vec(REC_WIDTH), gates, vec(REC_WIDTH), vec(REC_WIDTH),
            vec(REC_WIDTH), table, table, table, table, table, table, seed, seed,
        ],
        out_specs=[tok(REC_WIDTH), tok(ATTN_WIDTH), tok(4 * LANES), tok(4 * LANES)],
        out_shape=[jax.ShapeDtypeStruct((tokens, REC_WIDTH), BF16),
                   jax.ShapeDtypeStruct((tokens, ATTN_WIDTH), BF16),
                   jax.ShapeDtypeStruct((tokens, 4 * LANES), BF16),
                   jax.ShapeDtypeStruct((tokens, 4 * LANES), BF16)],
        scratch_shapes=[
            pltpu.VMEM((SUBLANES, REC_WIDTH), F32),
            pltpu.VMEM((SUBLANES, REC_WIDTH), F32),
            pltpu.VMEM((ncol, rows, LANES), F32),
            pltpu.VMEM((ncol, rows, LANES), F32),
            pltpu.VMEM((ncol, g1, LANES), F32),
            pltpu.VMEM((ncol, g1, LANES), F32),
            pltpu.VMEM((ncol, SUBLANES + g1, LANES), F32),
            pltpu.VMEM((ncol, SUBLANES + g2, LANES), F32),
        ],
        compiler_params=pltpu.CompilerParams(
            dimension_semantics=("arbitrary", "arbitrary"), vmem_limit_bytes=VMEM_LIMIT_BYTES),
    )(h, w_in, conv_w, conv_b, wa, ba, wx, bx, lam, g_rec, *q_tables, *k_tables, ctail, hinit)


def _mix_b_kernel(layer, sink_ref, q_ref, k4_ref, v4_ref, kp_ref, vp_ref, km_ref, vm_ref,
                  yrec_ref, h_ref, wout_ref, gattn_ref, lng_ref, lnb_ref, o_ref):
    rows = q_ref.shape[0]
    first = pl.program_id(1) == 0
    qi = lax.broadcasted_iota(jnp.int32, (WINDOW, 2 * WINDOW), 0)
    kj = lax.broadcasted_iota(jnp.int32, (WINDOW, 2 * WINDOW), 1)
    band_ok = (kj > qi) & (kj <= qi + WINDOW)
    band_ok_first = band_ok & (kj >= jnp.where(first, WINDOW, 0))
    lane32 = lax.broadcasted_iota(jnp.int32, (WINDOW, 2 * N_META), 1)
    meta_even = lane32 < N_META
    meta_odd = lane32 >= N_META
    items = [(jb, g) for jb in range(rows // WINDOW) for g in range(N_KV_HEADS)]

    def band(ref, prev_ref, jb, g):
        cur = slice(jb * WINDOW, (jb + 1) * WINDOW)
        prv = slice((jb - 1) * WINDOW, jb * WINDOW)
        parts = []
        for cc in (2 * g, 2 * g + 1):
            parts.append(_col(prev_ref, cc) if jb == 0 else _col(ref, cc, prv))
            parts.append(_col(ref, cc, cur))
        return jnp.concatenate(parts, axis=0)

    def meta(ref, g):
        return jnp.concatenate([_col(ref, 2 * g), _col(ref, 2 * g + 1)], axis=0)

    outs = {}
    for jb, g in items:
        cur = slice(jb * WINDOW, (jb + 1) * WINDOW)
        lhs = jnp.concatenate([_col(q_ref, 2 * g, cur), _col(q_ref, 2 * g + 1, cur)], axis=0)
        s_band = _dot_nt(lhs, band(k4_ref, kp_ref, jb, g))
        s_meta = _dot_nt(lhs, meta(km_ref, g))
        v_band, v_meta = band(v4_ref, vp_ref, jb, g), meta(vm_ref, g)
        ok = band_ok_first if jb == 0 else band_ok
        for c in range(2):
            rs = slice(c * WINDOW, (c + 1) * WINDOW)
            head = 4 * g + 2 * c
            pieces = _pair_probs(s_band[rs], ok, s_meta[rs], meta_even, meta_odd,
                                 sink_ref[layer, head], sink_ref[layer, head + 1])
            outs[jb, 2 * g + c] = _pair_out(*pieces, v_band, v_meta)
    blocks = []
    for jb in range(rows // WINDOW):
        y_attn = jnp.concatenate([outs[jb, cc] for cc in range(2 * N_KV_HEADS)], axis=1)
        blocks.append(_rms_norm(y_attn, gattn_ref[...]).astype(BF16))
    y_attn = blocks[0] if len(blocks) == 1 else jnp.concatenate(blocks, axis=0)
    mixed = _dot(jnp.concatenate([yrec_ref[...], y_attn], axis=1), wout_ref[...])
    o_ref[...] = _layer_norm(DEEPNORM_ALPHA * h_ref[...] + mixed, lng_ref[...], lnb_ref[...])


def _mix_b(sinks, q, k4, v4, km4, vm4, yrec, h, w_out, g_attn, ln_g, ln_b, layer, batch, seq):
    rows = MIX_B_ROWS
    nc = seq // rows
    blocks_per_step = rows // WINDOW
    blocks_per_seq = seq // WINDOW
    tokens = batch * seq
    tok = lambda width: pl.BlockSpec((rows, width), lambda bi, ci: (bi * nc + ci, 0))
    prev = pl.BlockSpec(
        (WINDOW, 4 * LANES),
        lambda bi, ci: (jnp.maximum(bi * blocks_per_seq + ci * blocks_per_step - 1, 0), 0))
    meta = pl.BlockSpec((N_META, 4 * LANES), lambda bi, ci: (0, 0))
    vec = lambda width: pl.BlockSpec((None, 1, width), lambda bi, ci: (layer, 0, 0))
    return pl.pallas_call(
        functools.partial(_mix_b_kernel, layer),
        grid=(batch, nc),
        in_specs=[
            pl.BlockSpec(memory_space=pltpu.SMEM),
            tok(ATTN_WIDTH), tok(4 * LANES), tok(4 * LANES), prev, prev, meta, meta,
            tok(REC_WIDTH), tok(D_MODEL),
            pl.BlockSpec((None, D_MIX, D_MODEL), lambda bi, ci: (layer, 0, 0),
                         pipeline_mode=pl.Buffered(1)),
            vec(ATTN_WIDTH), vec(D_MODEL), vec(D_MODEL),
        ],
        out_specs=tok(D_MODEL),
        out_shape=jax.ShapeDtypeStruct((tokens, D_MODEL), F32),
        compiler_params=pltpu.CompilerParams(
            dimension_semantics=("arbitrary", "arbitrary"), vmem_limit_bytes=VMEM_LIMIT_BYTES),
    )(sinks, q, k4, v4, k4, v4, km4, vm4, yrec, h, w_out, g_attn, ln_g, ln_b)


def _meta_mix_kernel(layer, sink_ref, h_ref, win_ref, cw_ref, cb_ref, wa_ref, ba_ref, wx_ref,
                     bx_ref, lam_ref, grec_ref, cosq_ref, s1q_ref, s2q_ref, cos_ref, s1_ref,
                     s2_ref, wout_ref, gattn_ref, lng_ref, lnb_ref,
                     o_ref, ctail_ref, hstate_ref, k4_ref, v4_ref, hbuf):
    rows = N_META
    h_in = h_ref[...]
    proj = _dot(h_in.astype(BF16), win_ref[...])
    xr = proj[:, :REC_WIDTH]
    gate = proj[:, REC_WIDTH:2 * REC_WIDTH]
    q = proj[:, 2 * REC_WIDTH:2 * REC_WIDTH + ATTN_WIDTH]
    k = proj[:, 2 * REC_WIDTH + ATTN_WIDTH:2 * REC_WIDTH + ATTN_WIDTH + KV_WIDTH]
    v = proj[:, 2 * REC_WIDTH + ATTN_WIDTH + KV_WIDTH:]

    back = lambda d: _delayed(xr, jnp.zeros((SUBLANES, REC_WIDTH), F32), d)
    a, b = _conv_and_gates(xr, back, cw_ref, cb_ref, wa_ref, ba_ref, wx_ref, bx_ref, lam_ref)
    ctail_ref[...] = xr[rows - SUBLANES:, :]

    a1, b1 = _scan8(a, b)
    hbuf[0:SUBLANES, :] = b1[:SUBLANES]
    hbuf[SUBLANES:, :] = b1[SUBLANES:] + a1[SUBLANES:] * _row_bcast(hbuf, SUBLANES - 1)
    hseq = hbuf[...]
    hstate_ref[...] = _row_bcast(hbuf, rows - 1)
    y_rec = _gelu_gate_norm(hseq, gate, grec_ref)

    qb = _rope(q, cosq_ref[...], s1q_ref[...], s2q_ref[...]).astype(BF16)
    k4 = _split_kv(_rope(k, cos_ref[...], s1_ref[...], s2_ref[...]))
    v4 = _split_kv(v)
    k4_ref[...] = k4
    v4_ref[...] = v4

    qi = lax.broadcasted_iota(jnp.int32, (rows, 2 * N_META), 0)
    lane32 = lax.broadcasted_iota(jnp.int32, (rows, 2 * N_META), 1)
    meta_even = (lane32 < N_META) & (lane32 <= qi)
    meta_odd = (lane32 >= N_META) & (lane32 - N_META <= qi)
    cols = []
    for g in range(N_KV_HEADS):
        lo_c, hi_c = 2 * g, 2 * g + 1
        k_meta = jnp.concatenate([_col(k4, lo_c), _col(k4, hi_c)], axis=0)
        v_meta = jnp.concatenate([_col(v4, lo_c), _col(v4, hi_c)], axis=0)
        lhs = jnp.concatenate([_col(qb, lo_c), _col(qb, hi_c)], axis=0)
        s_meta = _dot_nt(lhs, k_meta)
        for c in range(2):
            head = 4 * g + 2 * c
            pieces = _pair_probs(None, None, s_meta[c * rows:(c + 1) * rows], meta_even, meta_odd,
                                 sink_ref[layer, head], sink_ref[layer, head + 1])
            cols.append(_pair_out(*pieces, None, v_meta))
    y_attn = _rms_norm(jnp.concatenate(cols, axis=1), gattn_ref[...]).astype(BF16)
    mixed = _dot(jnp.concatenate([y_rec, y_attn], axis=1), wout_ref[...])
    o_ref[...] = _layer_norm(DEEPNORM_ALPHA * h_in + mixed, lng_ref[...], lnb_ref[...])


def _meta_mix(sinks, h, w_in, conv_w, conv_b, wa, ba, wx, bx, lam, g_rec, q_tables, k_tables,
              w_out, g_attn, ln_g, ln_b, layer):
    vec = lambda width: pl.BlockSpec((None, 1, width), lambda i: (layer, 0, 0))
    gates = pl.BlockSpec((None, 2, GATE_HALF, GATE_HALF), lambda i: (layer, 0, 0, 0))
    table = pl.BlockSpec((N_META, LANES), lambda i: (0, 0))
    full = lambda r, w: pl.BlockSpec((r, w), lambda i: (0, 0))
    return pl.pallas_call(
        functools.partial(_meta_mix_kernel, layer),
        grid=(1,),
        in_specs=[
            pl.BlockSpec(memory_space=pltpu.SMEM),
            full(N_META, D_MODEL),
            pl.BlockSpec((None, D_MODEL, D_IN), lambda i: (layer, 0, 0)),
            pl.BlockSpec((None, CONV_WIDTH, REC_WIDTH), lambda i: (layer, 0, 0)),
            vec(REC_WIDTH), gates, vec(REC_WIDTH), gates, vec(REC_WIDTH), vec(REC_WIDTH),
            vec(REC_WIDTH), table, table, table, table, table, table,
            pl.BlockSpec((None, D_MIX, D_MODEL), lambda i: (layer, 0, 0)),
            vec(ATTN_WIDTH), vec(D_MODEL), vec(D_MODEL),
        ],
        out_specs=[full(N_META, D_MODEL), full(SUBLANES, REC_WIDTH), full(SUBLANES, REC_WIDTH),
                   full(N_META, 4 * LANES), full(N_META, 4 * LANES)],
        out_shape=[jax.ShapeDtypeStruct((N_META, D_MODEL), F32),
                   jax.ShapeDtypeStruct((SUBLANES, REC_WIDTH), F32),
                   jax.ShapeDtypeStruct((SUBLANES, REC_WIDTH), F32),
                   jax.ShapeDtypeStruct((N_META, 4 * LANES), BF16),
                   jax.ShapeDtypeStruct((N_META, 4 * LANES), BF16)],
        scratch_shapes=[pltpu.VMEM((N_META, REC_WIDTH), F32)],
        compiler_params=pltpu.CompilerParams(
            dimension_semantics=("arbitrary",), vmem_limit_bytes=VMEM_LIMIT_BYTES),
    )(sinks, h, w_in, conv_w, conv_b, wa, ba, wx, bx, lam, g_rec, *q_tables, *k_tables, w_out,
      g_attn, ln_g, ln_b)


def _block_diag_halves(w):
    per_half = REC_BLOCKS // 2
    out = jnp.zeros((w.shape[0], 2, GATE_HALF, GATE_HALF), w.dtype)
    for blk in range(REC_BLOCKS):
        half, pos = divmod(blk, per_half)
        sl = slice(pos * REC_BLOCK_DIM, (pos + 1) * REC_BLOCK_DIM)
        out = out.at[:, half, sl, sl].set(w[:, blk])
    return out.astype(BF16)


def _rope_tables(first_pos, count):
    pos = first_pos + jnp.arange(count, dtype=F32)
    inv_freq = ROPE_THETA ** (-jnp.arange(0, ROPE_DIM, 2, dtype=F32) / ROPE_DIM)
    ang = pos[:, None] * inv_freq[None, :]
    cos, sin = jnp.cos(ang), jnp.sin(ang)
    half = ROPE_DIM // 2
    pad = HEAD_DIM - ROPE_DIM
    ones = jnp.ones((count, pad), F32)
    zeros_h = jnp.zeros((count, half), F32)
    zeros_p = jnp.zeros((count, pad), F32)
    c_head = jnp.concatenate([cos, cos, ones], axis=1)
    s1_head = jnp.concatenate([-sin, zeros_h, zeros_p], axis=1)
    s2_head = jnp.concatenate([zeros_h, sin, zeros_p], axis=1)
    two = lambda t: jnp.concatenate([t, t], axis=1)
    return two(c_head), two(s1_head), two(s2_head)


def kernel(x, meta_tokens, ffn1_w_gate, ffn1_w_up, ffn1_w_down, ln1_g, ln1_b, w_in, conv_w, conv_b, gate_a_w, gate_a_b, gate_x_w, gate_x_b, lru_lambda, attn_sinks, norm_rec_g, norm_attn_g, w_out, ln2_g, ln2_b, ffn2_w_gate, ffn2_w_up, ffn2_w_down, ln3_g, ln3_b):
    batch, seq, _ = x.shape
    assert seq % MIX_A_ROWS == 0 and seq % MIX_B_ROWS == 0 and (batch * seq) % FFN_ROWS == 0
    assert MIX_A_ROWS == SUBLANES ** 3

    bf = lambda w: w.astype(BF16)
    row = lambda p: p.reshape(p.shape[0], 1, p.shape[1])
    f1g, f1u, f1d = bf(ffn1_w_gate), bf(ffn1_w_up), bf(ffn1_w_down)
    f2g, f2u, f2d = bf(ffn2_w_gate), bf(ffn2_w_up), bf(ffn2_w_down)
    win, wout = bf(w_in), bf(w_out)
    wa, wx = _block_diag_halves(gate_a_w), _block_diag_halves(gate_x_w)
    l1g, l1b, l2g, l2b, l3g, l3b = map(row, (ln1_g, ln1_b, ln2_g, ln2_b, ln3_g, ln3_b))
    cb, ba, bx, lam, grec, gattn = map(
        row, (conv_b, gate_a_b, gate_x_b, lru_lambda, norm_rec_g, norm_attn_g))
    k_tab_m = _rope_tables(0.0, N_META)
    k_tab_x = _rope_tables(float(N_META), seq)
    q_tab_m = tuple(t * Q_SCALE for t in k_tab_m)
    q_tab_x = tuple(t * Q_SCALE for t in k_tab_x)

    h = x.reshape(batch * seq, D_MODEL)
    hm = meta_tokens.astype(x.dtype)
    for l in range(DEPTH):
        hm = _ffn(hm, f1g, f1u, f1d, l1g, l1b, l, N_META)
        h = _ffn(h, f1g, f1u, f1d, l1g, l1b, l, FFN_ROWS)
        hm, ctail, hinit, km4, vm4 = _meta_mix(
            attn_sinks, hm, win, conv_w, cb, wa, ba, wx, bx, lam, grec, q_tab_m, k_tab_m,
            wout, gattn, l2g, l2b, l)
        yrec, q, k4, v4 = _mix_a(h, win, conv_w, cb, wa, ba, wx, bx, lam, grec,
                                 q_tab_x, k_tab_x, ctail, hinit, l, batch, seq)
        h = _mix_b(attn_sinks, q, k4, v4, km4, vm4, yrec, h, wout, gattn, l2g, l2b,
                   l, batch, seq)
        if l + 1 < DEPTH:
            hm = _ffn(hm, f2g, f2u, f2d, l3g, l3b, l, N_META)
        h = _ffn(h, f2g, f2u, f2d, l3g, l3b, l, FFN_ROWS)
    return h.reshape(batch, seq, D_MODEL)
```

```python
import functools

import jax
import jax.numpy as jnp
from jax import lax
from jax.experimental import pallas as pl
from jax.experimental.pallas import tpu as pltpu

F32 = jnp.float32
BF16 = jnp.bfloat16

D_MODEL = 1024
DEPTH = 4
N_META = 16
D_FF = 2816
REC_WIDTH = 512
REC_BLOCKS = 8
REC_BLOCK_DIM = REC_WIDTH // REC_BLOCKS
CONV_WIDTH = 4
LRU_C = 8.0
N_Q_HEADS = 8
N_KV_HEADS = 2
HEAD_DIM = 64
ATTN_WIDTH = N_Q_HEADS * HEAD_DIM
KV_WIDTH = N_KV_HEADS * HEAD_DIM
WINDOW = 128
ROPE_DIM = HEAD_DIM // 4
ROPE_THETA = 500000.0
D_IN = 2 * REC_WIDTH + ATTN_WIDTH + 2 * KV_WIDTH
D_MIX = REC_WIDTH + ATTN_WIDTH
DEEPNORM_ALPHA = (2.0 * DEPTH) ** 0.25
LN_EPS = 1e-5
RMS_EPS = 1e-6
NEG_INF = -1e30
Q_SCALE = HEAD_DIM ** -0.5

LANES = 128
SUBLANES = 8
MXU_DIM = 256
VMEM_LIMIT_BYTES = 56 * 1024 * 1024

FFN_ROWS = 1024
FFN_SUB_ROWS = 512
FF_CHUNK = MXU_DIM
MIX_A_ROWS = 512
MIX_B_ROWS = 256
ATT_BLOCK = 64
GATE_HALF = REC_WIDTH // 2
SERIES_CUTOFF = -1.0 / 64.0


def _layer_norm(y, g, b):
    mu = jnp.mean(y, axis=-1, keepdims=True)
    d = y - mu
    var = jnp.mean(d * d, axis=-1, keepdims=True)
    return d * lax.rsqrt(var + LN_EPS) * g + b


def _rms_norm(y, g):
    ms = jnp.mean(y * y, axis=-1, keepdims=True)
    return y * lax.rsqrt(ms + RMS_EPS) * g


def _dot(a, b):
    return jnp.dot(a, b, preferred_element_type=F32)


def _dot_nt(a, b):
    return lax.dot_general(a, b, (((1,), (1,)), ((), ())), preferred_element_type=F32)


def _ffn_kernel(x_ref, wg_ref, wu_ref, wd_ref, g_ref, b_ref, o_ref, act_ref):
    sub = act_ref.shape[1]
    for r in range(x_ref.shape[0] // sub):
        rs = slice(r * sub, (r + 1) * sub)
        x = x_ref[rs, :]
        xb = x.astype(BF16)
        for c in range(D_FF // FF_CHUNK):
            sl = slice(c * FF_CHUNK, (c + 1) * FF_CHUNK)
            gate = _dot(xb, wg_ref[:, sl])
            up = _dot(xb, wu_ref[:, sl])
            act_ref[r, :, sl] = (gate * jax.nn.sigmoid(gate) * up).astype(BF16)
        y = DEEPNORM_ALPHA * x + 0.5 * _dot(act_ref[r], wd_ref[...])
        o_ref[rs, :] = _layer_norm(y, g_ref[...], b_ref[...])


def _ffn(x, wg, wu, wd, g, b, layer, rows):
    tokens = x.shape[0]
    resident = dict(pipeline_mode=pl.Buffered(1))
    sub = min(rows, FFN_SUB_ROWS)
    return pl.pallas_call(
        _ffn_kernel,
        grid=(tokens // rows,),
        scratch_shapes=[pltpu.VMEM((rows // sub, sub, D_FF), BF16)],
        in_specs=[
            pl.BlockSpec((rows, D_MODEL), lambda i: (i, 0)),
            pl.BlockSpec((None, D_MODEL, D_FF), lambda i: (layer, 0, 0), **resident),
            pl.BlockSpec((None, D_MODEL, D_FF), lambda i: (layer, 0, 0), **resident),
            pl.BlockSpec((None, D_FF, D_MODEL), lambda i: (layer, 0, 0), **resident),
            pl.BlockSpec((None, 1, D_MODEL), lambda i: (layer, 0, 0)),
            pl.BlockSpec((None, 1, D_MODEL), lambda i: (layer, 0, 0)),
        ],
        out_specs=pl.BlockSpec((rows, D_MODEL), lambda i: (i, 0)),
        out_shape=jax.ShapeDtypeStruct((tokens, D_MODEL), F32),
        compiler_params=pltpu.CompilerParams(
            dimension_semantics=("arbitrary",), vmem_limit_bytes=VMEM_LIMIT_BYTES),
    )(x, wg, wu, wd, g, b)


def _softplus(x):
    return jnp.maximum(x, 0.0) + jnp.log1p(jnp.exp(-jnp.abs(x)))


def _conv_and_gates(xr, back, cw_ref, cb_ref, wa_ref, ba_ref, wx_ref, bx_ref, lam_ref):
    xc = cb_ref[...] + back(3) * cw_ref[0:1, :]
    xc = xc + back(2) * cw_ref[1:2, :]
    xc = xc + back(1) * cw_ref[2:3, :]
    xc = xc + xr * cw_ref[3:4, :]
    xcb = xc.astype(BF16)
    lo, hi = xcb[:, :GATE_HALF], xcb[:, GATE_HALF:]
    r_lin = jnp.concatenate([_dot(lo, wa_ref[0]), _dot(hi, wa_ref[1])], axis=1) + ba_ref[...]
    i_lin = jnp.concatenate([_dot(lo, wx_ref[0]), _dot(hi, wx_ref[1])], axis=1) + bx_ref[...]
    r = jax.nn.sigmoid(r_lin)
    i = jax.nn.sigmoid(i_lin)
    log_a = (-LRU_C * _softplus(-lam_ref[...])) * r
    a = jnp.exp(log_a)
    y = 2.0 * log_a
    series = -y * (1.0 + y * (0.5 + y * (1.0 / 6.0)))
    one_minus_a2 = jnp.where(y > SERIES_CUTOFF, series, 1.0 - a * a)
    b = jnp.sqrt(one_minus_a2) * (i * xc)
    return a, b


def _delayed(xr, prev8, d):
    rolled = pltpu.roll(xr, d, 0)
    sub = lax.broadcasted_iota(jnp.int32, prev8.shape, 0)
    head = jnp.where(sub < d, pltpu.roll(prev8, d, 0), rolled[:SUBLANES])
    return jnp.concatenate([head, rolled[SUBLANES:]], axis=0)


def _scan8(a, b):
    rows, width = a.shape
    groups = rows // SUBLANES
    a3 = a.reshape(groups, SUBLANES, width)
    b3 = b.reshape(groups, SUBLANES, width)
    sub = lax.broadcasted_iota(jnp.int32, (groups, SUBLANES, width), 1)
    for s in (1, 2, 4):
        keep = sub >= s
        b_prev = jnp.where(keep, pltpu.roll(b3, s, 1), 0.0)
        a_prev = jnp.where(keep, pltpu.roll(a3, s, 1), 1.0)
        b3 = b3 + a3 * b_prev
        a3 = a3 * a_prev
    return a3.reshape(rows, width), b3.reshape(rows, width)


def _row_bcast(ref, r):
    return jnp.broadcast_to(ref[pl.ds(r, 1), :], (SUBLANES, ref.shape[1]))


def _rope(t, cos, s1, s2):
    cols = []
    for j in range(t.shape[1] // LANES):
        tj = t[:, j * LANES:(j + 1) * LANES]
        up = pltpu.roll(tj, LANES - ROPE_DIM // 2, 1)
        down = pltpu.roll(tj, ROPE_DIM // 2, 1)
        cols.append(tj * cos + up * s1 + down * s2)
    return cols[0] if len(cols) == 1 else jnp.concatenate(cols, axis=1)


def _split_kv(t):
    swapped = pltpu.roll(t, HEAD_DIM, 1)
    low = lax.broadcasted_iota(jnp.int32, t.shape, 1) < HEAD_DIM
    zero = jnp.zeros_like(t)
    parts = [jnp.where(low, t, zero), jnp.where(low, zero, swapped),
             jnp.where(low, swapped, zero), jnp.where(low, zero, t)]
    return jnp.concatenate(parts, axis=1).astype(BF16)


def _gelu_gate_norm(hseq, gate, grec_ref):
    y = hseq * jax.nn.gelu(gate, approximate=True)
    return _rms_norm(y, grec_ref[...]).astype(BF16)


def _pair_probs(s_band, band_ok, s_meta, meta_ok_even, meta_ok_odd, sink_even, sink_odd):
    rows = s_meta.shape[0]
    sme = jnp.where(meta_ok_even, s_meta, NEG_INF)
    smo = jnp.where(meta_ok_odd, s_meta, NEG_INF)
    m_e = jnp.maximum(jnp.max(sme, axis=-1, keepdims=True), sink_even)
    m_o = jnp.maximum(jnp.max(smo, axis=-1, keepdims=True), sink_odd)
    if s_band is not None:
        half = s_band.shape[1] // 2
        se = jnp.where(band_ok, s_band[:, :half], NEG_INF)
        so = jnp.where(band_ok, s_band[:, half:], NEG_INF)
        m_e = jnp.maximum(m_e, jnp.max(se, axis=-1, keepdims=True))
        m_o = jnp.maximum(m_o, jnp.max(so, axis=-1, keepdims=True))
    pme = jnp.exp(sme - m_e)
    pmo = jnp.exp(smo - m_o)
    den_e = jnp.sum(pme, axis=-1, keepdims=True) + jnp.exp(sink_even - m_e)
    den_o = jnp.sum(pmo, axis=-1, keepdims=True) + jnp.exp(sink_odd - m_o)
    p_band = None
    if s_band is not None:
        pe = jnp.exp(se - m_e)
        po = jnp.exp(so - m_o)
        den_e = den_e + jnp.sum(pe, axis=-1, keepdims=True)
        den_o = den_o + jnp.sum(po, axis=-1, keepdims=True)
        p_band = jnp.concatenate([pe, po], axis=1).astype(BF16)
    low = lax.broadcasted_iota(jnp.int32, (rows, LANES), 1) < HEAD_DIM
    inv = jnp.where(low, 1.0 / den_e, 1.0 / den_o)
    return p_band, (pme + pmo).astype(BF16), inv


def _pair_out(p_band, p_meta, inv, v_band, v_meta):
    out = _dot(p_meta, v_meta)
    if p_band is not None:
        out = out + _dot(p_band, v_band)
    return out * inv


def _col(ref_or_val, j, rows=slice(None)):
    return ref_or_val[rows, j * LANES:(j + 1) * LANES]


def _mix_a_kernel(h_ref, win_ref, cw_ref, cb_ref, wa_ref, ba_ref, wx_ref, bx_ref, lam_ref,
                  grec_ref, cosq_ref, s1q_ref, s2q_ref, cos_ref, s1_ref, s2_ref, ctail_ref,
                  hinit_ref,
                  yrec_ref, q_ref, k4_ref, v4_ref,
                  hist, state, a1_buf, b1_buf, a2_buf, b2_buf, h2_buf, h3_buf):
    rows = h_ref.shape[0]
    g1 = rows // SUBLANES
    g2 = g1 // SUBLANES

    @pl.when(pl.program_id(1) == 0)
    def _():
        hist[...] = ctail_ref[...]
        state[...] = hinit_ref[...]

    proj = _dot(h_ref[...].astype(BF16), win_ref[...])
    xr = proj[:, :REC_WIDTH]
    gate = proj[:, REC_WIDTH:2 * REC_WIDTH]
    q = proj[:, 2 * REC_WIDTH:2 * REC_WIDTH + ATTN_WIDTH]
    k = proj[:, 2 * REC_WIDTH + ATTN_WIDTH:2 * REC_WIDTH + ATTN_WIDTH + KV_WIDTH]
    v = proj[:, 2 * REC_WIDTH + ATTN_WIDTH + KV_WIDTH:]

    prev8 = hist[...]
    back = lambda d: _delayed(xr, prev8, d)
    a, b = _conv_and_gates(xr, back, cw_ref, cb_ref, wa_ref, ba_ref, wx_ref, bx_ref, lam_ref)
    hist[...] = xr[rows - SUBLANES:, :]

    ends = lambda buf, c, n: buf[c, pl.ds(SUBLANES - 1, n, stride=SUBLANES), :]
    col_seqs = []
    for c in range(REC_WIDTH // LANES):
        lanes = slice(c * LANES, (c + 1) * LANES)
        h0 = state[:, lanes]
        a1, b1 = _scan8(a[:, lanes], b[:, lanes])
        a1_buf[c] = a1
        b1_buf[c] = b1
        a2, b2 = _scan8(ends(a1_buf, c, g1), ends(b1_buf, c, g1))
        a2_buf[c] = a2
        b2_buf[c] = b2
        a3, b3 = _scan8(ends(a2_buf, c, g2), ends(b2_buf, c, g2))
        h3_buf[c, 0:SUBLANES, :] = h0
        h3_buf[c, SUBLANES:, :] = b3 + a3 * h0
        h2_buf[c, 0:SUBLANES, :] = h0
        for j in range(g2):
            sl = slice(j * SUBLANES, (j + 1) * SUBLANES)
            h2_buf[c, pl.ds(SUBLANES + j * SUBLANES, SUBLANES), :] = (
                b2[sl] + a2[sl] * _row_bcast(h3_buf.at[c], SUBLANES - 1 + j))
        pieces = []
        for j in range(g1):
            sl = slice(j * SUBLANES, (j + 1) * SUBLANES)
            pieces.append(b1[sl] + a1[sl] * _row_bcast(h2_buf.at[c], SUBLANES - 1 + j))
        col_seqs.append(jnp.concatenate(pieces, axis=0))
        state[:, lanes] = _row_bcast(h2_buf.at[c], SUBLANES - 1 + g1)
    hseq = jnp.concatenate(col_seqs, axis=1)

    yrec_ref[...] = _gelu_gate_norm(hseq, gate, grec_ref)
    q_ref[...] = _rope(q, cosq_ref[...], s1q_ref[...], s2q_ref[...]).astype(BF16)
    k4_ref[...] = _split_kv(_rope(k, cos_ref[...], s1_ref[...], s2_ref[...]))
    v4_ref[...] = _split_kv(v)


def _mix_a(h, w_in, conv_w, conv_b, wa, ba, wx, bx, lam, g_rec, q_tables, k_tables, ctail,
           hinit, layer, batch, seq):
    rows = MIX_A_ROWS
    nc = seq // rows
    tokens = batch * seq
    tok = lambda width: pl.BlockSpec((rows, width), lambda bi, ci: (bi * nc + ci, 0))
    vec = lambda width: pl.BlockSpec((None, 1, width), lambda bi, ci: (layer, 0, 0))
    gates = pl.BlockSpec((None, 2, GATE_HALF, GATE_HALF), lambda bi, ci: (layer, 0, 0, 0))
    table = pl.BlockSpec((rows, LANES), lambda bi, ci: (ci, 0))
    seed = pl.BlockSpec((SUBLANES, REC_WIDTH), lambda bi, ci: (0, 0))
    g1 = rows // SUBLANES
    g2 = g1 // SUBLANES
    ncol = REC_WIDTH // LANES
    return pl.pallas_call(
        _mix_a_kernel,
        grid=(batch, nc),
        in_specs=[
            tok(D_MODEL),
            pl.BlockSpec((None, D_MODEL, D_IN), lambda bi, ci: (layer, 0, 0),
                         pipeline_mode=pl.Buffered(1)),
            pl.BlockSpec((None, CONV_WIDTH, REC_WIDTH), lambda bi, ci: (layer, 0, 0)),
            vec(REC_WIDTH), gates, vec(REC_WIDTH), gates, vec(REC_WIDTH), vec(REC_WIDTH),
            vec(REC_WIDTH), table, table, table, table, table, table, seed, seed,
        ],
        out_specs=[tok(REC_WIDTH), tok(ATTN_WIDTH), tok(4 * LANES), tok(4 * LANES)],
        out_shape=[jax.ShapeDtypeStruct((tokens, REC_WIDTH), BF16),
                   jax.ShapeDtypeStruct((tokens, ATTN_WIDTH), BF16),
                   jax.ShapeDtypeStruct((tokens, 4 * LANES), BF16),
                   jax.ShapeDtypeStruct((tokens, 4 * LANES), BF16)],
        scratch_shapes=[
            pltpu.VMEM((SUBLANES, REC_WIDTH), F32),
            pltpu.VMEM((SUBLANES, REC_WIDTH), F32),
            pltpu.VMEM((ncol, rows, LANES), F32),
            pltpu.VMEM((ncol, rows, LANES), F32),
            pltpu.VMEM((ncol, g1, LANES), F32),
            pltpu.VMEM((ncol, g1, LANES), F32),
            pltpu.VMEM((ncol, SUBLANES + g1, LANES), F32),
            pltpu.VMEM((ncol, SUBLANES + g2, LANES), F32),
        ],
        compiler_params=pltpu.CompilerParams(
            dimension_semantics=("arbitrary", "arbitrary"), vmem_limit_bytes=VMEM_LIMIT_BYTES),
    )(h, w_in, conv_w, conv_b, wa, ba, wx, bx, lam, g_rec, *q_tables, *k_tables, ctail, hinit)


def _window_probs(s, ok, sink_even, sink_odd):
    rows, half = s.shape[0], s.shape[1] // 2
    se = jnp.where(ok, s[:, :half], NEG_INF)
    so = jnp.where(ok, s[:, half:], NEG_INF)
    m_e = jnp.maximum(jnp.max(se, axis=-1, keepdims=True), sink_even)
    m_o = jnp.maximum(jnp.max(so, axis=-1, keepdims=True), sink_odd)
    pe = jnp.exp(se - m_e)
    po = jnp.exp(so - m_o)
    den_e = jnp.sum(pe, axis=-1, keepdims=True) + jnp.exp(sink_even - m_e)
    den_o = jnp.sum(po, axis=-1, keepdims=True) + jnp.exp(sink_odd - m_o)
    low = lax.broadcasted_iota(jnp.int32, (rows, LANES), 1) < HEAD_DIM
    inv = jnp.where(low, 1.0 / den_e, 1.0 / den_o)
    return jnp.concatenate([pe, po], axis=1).astype(BF16), inv


def _mix_b_kernel(layer, sink_ref, q_ref, k4_ref, v4_ref, kp_ref, vp_ref, km_ref, vm_ref,
                  yrec_ref, h_ref, wout_ref, gattn_ref, lng_ref, lnb_ref, o_ref):
    rows = q_ref.shape[0]
    first = pl.program_id(1) == 0
    span = WINDOW + ATT_BLOCK
    pad = 2 * WINDOW - span - N_META
    qi = lax.broadcasted_iota(jnp.int32, (ATT_BLOCK, 2 * WINDOW), 0)
    kj = lax.broadcasted_iota(jnp.int32, (ATT_BLOCK, 2 * WINDOW), 1)
    is_meta = kj < N_META
    lo = qi + (N_META + pad + 1)
    hi = qi + (N_META + pad + WINDOW)

    def visible(missing):
        floor = jnp.where(first, N_META + pad + missing, 0)
        return is_meta | ((kj >= jnp.maximum(lo, floor)) & (kj <= hi))

    zeros = jnp.zeros((pad, LANES), BF16)

    def window(ref, prev_ref, meta_ref, jb, g):
        end = (jb + 1) * ATT_BLOCK
        parts = []
        for cc in (2 * g, 2 * g + 1):
            parts += [_col(meta_ref, cc), zeros]
            if end < span:
                parts += [_col(prev_ref, cc, slice(end - span + WINDOW, WINDOW)),
                          _col(ref, cc, slice(0, end))]
            else:
                parts.append(_col(ref, cc, slice(end - span, end)))
        return jnp.concatenate(parts, axis=0)

    blocks = []
    for jb in range(rows // ATT_BLOCK):
        cur = slice(jb * ATT_BLOCK, (jb + 1) * ATT_BLOCK)
        ok = visible(max(span - (jb + 1) * ATT_BLOCK, 0))
        cols = []
        for g in range(N_KV_HEADS):
            lhs = jnp.concatenate([_col(q_ref, 2 * g, cur), _col(q_ref, 2 * g + 1, cur)], axis=0)
            s = _dot_nt(lhs, window(k4_ref, kp_ref, km_ref, jb, g))
            p_cols, inv_cols = [], []
            for c in range(2):
                head = 4 * g + 2 * c
                p, inv = _window_probs(s[c * ATT_BLOCK:(c + 1) * ATT_BLOCK], ok,
                                       sink_ref[layer, head], sink_ref[layer, head + 1])
                p_cols.append(p)
                inv_cols.append(inv)
            out = _dot(jnp.concatenate(p_cols, axis=0), window(v4_ref, vp_ref, vm_ref, jb, g))
            out = out * jnp.concatenate(inv_cols, axis=0)
            cols += [out[:ATT_BLOCK], out[ATT_BLOCK:]]
        blocks.append(_rms_norm(jnp.concatenate(cols, axis=1), gattn_ref[...]).astype(BF16))
    y_attn = jnp.concatenate(blocks, axis=0)
    mixed = _dot(jnp.concatenate([yrec_ref[...], y_attn], axis=1), wout_ref[...])
    o_ref[...] = _layer_norm(DEEPNORM_ALPHA * h_ref[...] + mixed, lng_ref[...], lnb_ref[...])


def _mix_b(sinks, q, k4, v4, km4, vm4, yrec, h, w_out, g_attn, ln_g, ln_b, layer, batch, seq):
    rows = MIX_B_ROWS
    nc = seq // rows
    blocks_per_step = rows // WINDOW
    blocks_per_seq = seq // WINDOW
    tokens = batch * seq
    tok = lambda width: pl.BlockSpec((rows, width), lambda bi, ci: (bi * nc + ci, 0))
    prev = pl.BlockSpec(
        (WINDOW, 4 * LANES),
        lambda bi, ci: (jnp.maximum(bi * blocks_per_seq + ci * blocks_per_step - 1, 0), 0))
    meta = pl.BlockSpec((N_META, 4 * LANES), lambda bi, ci: (0, 0))
    vec = lambda width: pl.BlockSpec((None, 1, width), lambda bi, ci: (layer, 0, 0))
    return pl.pallas_call(
        functools.partial(_mix_b_kernel, layer),
        grid=(batch, nc),
        in_specs=[
            pl.BlockSpec(memory_space=pltpu.SMEM),
            tok(ATTN_WIDTH), tok(4 * LANES), tok(4 * LANES), prev, prev, meta, meta,
            tok(REC_WIDTH), tok(D_MODEL),
            pl.BlockSpec((None, D_MIX, D_MODEL), lambda bi, ci: (layer, 0, 0),
                         pipeline_mode=pl.Buffered(1)),
            vec(ATTN_WIDTH), vec(D_MODEL), vec(D_MODEL),
        ],
        out_specs=tok(D_MODEL),
        out_shape=jax.ShapeDtypeStruct((tokens, D_MODEL), F32),
        compiler_params=pltpu.CompilerParams(
            dimension_semantics=("arbitrary", "arbitrary"), vmem_limit_bytes=VMEM_LIMIT_BYTES),
    )(sinks, q, k4, v4, k4, v4, km4, vm4, yrec, h, w_out, g_attn, ln_g, ln_b)


def _meta_mix_kernel(layer, sink_ref, h_ref, win_ref, cw_ref, cb_ref, wa_ref, ba_ref, wx_ref,
                     bx_ref, lam_ref, grec_ref, cosq_ref, s1q_ref, s2q_ref, cos_ref, s1_ref,
                     s2_ref, wout_ref, gattn_ref, lng_ref, lnb_ref,
                     o_ref, ctail_ref, hstate_ref, k4_ref, v4_ref, hbuf):
    rows = N_META
    h_in = h_ref[...]
    proj = _dot(h_in.astype(BF16), win_ref[...])
    xr = proj[:, :REC_WIDTH]
    gate = proj[:, REC_WIDTH:2 * REC_WIDTH]
    q = proj[:, 2 * REC_WIDTH:2 * REC_WIDTH + ATTN_WIDTH]
    k = proj[:, 2 * REC_WIDTH + ATTN_WIDTH:2 * REC_WIDTH + ATTN_WIDTH + KV_WIDTH]
    v = proj[:, 2 * REC_WIDTH + ATTN_WIDTH + KV_WIDTH:]

    back = lambda d: _delayed(xr, jnp.zeros((SUBLANES, REC_WIDTH), F32), d)
    a, b = _conv_and_gates(xr, back, cw_ref, cb_ref, wa_ref, ba_ref, wx_ref, bx_ref, lam_ref)
    ctail_ref[...] = xr[rows - SUBLANES:, :]

    a1, b1 = _scan8(a, b)
    hbuf[0:SUBLANES, :] = b1[:SUBLANES]
    hbuf[SUBLANES:, :] = b1[SUBLANES:] + a1[SUBLANES:] * _row_bcast(hbuf, SUBLANES - 1)
    hseq = hbuf[...]
    hstate_ref[...] = _row_bcast(hbuf, rows - 1)
    y_rec = _gelu_gate_norm(hseq, gate, grec_ref)

    qb = _rope(q, cosq_ref[...], s1q_ref[...], s2q_ref[...]).astype(BF16)
    k4 = _split_kv(_rope(k, cos_ref[...], s1_ref[...], s2_ref[...]))
    v4 = _split_kv(v)
    k4_ref[...] = k4
    v4_ref[...] = v4

    qi = lax.broadcasted_iota(jnp.int32, (rows, 2 * N_META), 0)
    lane32 = lax.broadcasted_iota(jnp.int32, (rows, 2 * N_META), 1)
    meta_even = (lane32 < N_META) & (lane32 <= qi)
    meta_odd = (lane32 >= N_META) & (lane32 - N_META <= qi)
    cols = []
    for g in range(N_KV_HEADS):
        lo_c, hi_c = 2 * g, 2 * g + 1
        k_meta = jnp.concatenate([_col(k4, lo_c), _col(k4, hi_c)], axis=0)
        v_meta = jnp.concatenate([_col(v4, lo_c), _col(v4, hi_c)], axis=0)
        lhs = jnp.concatenate([_col(qb, lo_c), _col(qb, hi_c)], axis=0)
        s_meta = _dot_nt(lhs, k_meta)
        for c in range(2):
            head = 4 * g + 2 * c
            pieces = _pair_probs(None, None, s_meta[c * rows:(c + 1) * rows], meta_even, meta_odd,
                                 sink_ref[layer, head], sink_ref[layer, head + 1])
            cols.append(_pair_out(*pieces, None, v_meta))
    y_attn = _rms_norm(jnp.concatenate(cols, axis=1), gattn_ref[...]).astype(BF16)
    mixed = _dot(jnp.concatenate([y_rec, y_attn], axis=1), wout_ref[...])
    o_ref[...] = _layer_norm(DEEPNORM_ALPHA * h_in + mixed, lng_ref[...], lnb_ref[...])


def _meta_mix(sinks, h, w_in, conv_w, conv_b, wa, ba, wx, bx, lam, g_rec, q_tables, k_tables,
              w_out, g_attn, ln_g, ln_b, layer):
    vec = lambda width: pl.BlockSpec((None, 1, width), lambda i: (layer, 0, 0))
    gates = pl.BlockSpec((None, 2, GATE_HALF, GATE_HALF), lambda i: (layer, 0, 0, 0))
    table = pl.BlockSpec((N_META, LANES), lambda i: (0, 0))
    full = lambda r, w: pl.BlockSpec((r, w), lambda i: (0, 0))
    return pl.pallas_call(
        functools.partial(_meta_mix_kernel, layer),
        grid=(1,),
        in_specs=[
            pl.BlockSpec(memory_space=pltpu.SMEM),
            full(N_META, D_MODEL),
            pl.BlockSpec((None, D_MODEL, D_IN), lambda i: (layer, 0, 0)),
            pl.BlockSpec((None, CONV_WIDTH, REC_WIDTH), lambda i: (layer, 0, 0)),
            vec(REC_WIDTH), gates, vec(REC_WIDTH), gates, vec(REC_WIDTH), vec(REC_WIDTH),
            vec(REC_WIDTH), table, table, table, table, table, table,
            pl.BlockSpec((None, D_MIX, D_MODEL), lambda i: (layer, 0, 0)),
            vec(ATTN_WIDTH), vec(D_MODEL), vec(D_MODEL),
        ],
        out_specs=[full(N_META, D_MODEL), full(SUBLANES, REC_WIDTH), full(SUBLANES, REC_WIDTH),
                   full(N_META, 4 * LANES), full(N_META, 4 * LANES)],
        out_shape=[jax.ShapeDtypeStruct((N_META, D_MODEL), F32),
                   jax.ShapeDtypeStruct((SUBLANES, REC_WIDTH), F32),
                   jax.ShapeDtypeStruct((SUBLANES, REC_WIDTH), F32),
                   jax.ShapeDtypeStruct((N_META, 4 * LANES), BF16),
                   jax.ShapeDtypeStruct((N_META, 4 * LANES), BF16)],
        scratch_shapes=[pltpu.VMEM((N_META, REC_WIDTH), F32)],
        compiler_params=pltpu.CompilerParams(
            dimension_semantics=("arbitrary",), vmem_limit_bytes=VMEM_LIMIT_BYTES),
    )(sinks, h, w_in, conv_w, conv_b, wa, ba, wx, bx, lam, g_rec, *q_tables, *k_tables, w_out,
      g_attn, ln_g, ln_b)


def _block_diag_halves(w):
    per_half = REC_BLOCKS // 2
    out = jnp.zeros((w.shape[0], 2, GATE_HALF, GATE_HALF), w.dtype)
    for blk in range(REC_BLOCKS):
        half, pos = divmod(blk, per_half)
        sl = slice(pos * REC_BLOCK_DIM, (pos + 1) * REC_BLOCK_DIM)
        out = out.at[:, half, sl, sl].set(w[:, blk])
    return out.astype(BF16)


def _rope_tables(first_pos, count):
    pos = first_pos + jnp.arange(count, dtype=F32)
    inv_freq = ROPE_THETA ** (-jnp.arange(0, ROPE_DIM, 2, dtype=F32) / ROPE_DIM)
    ang = pos[:, None] * inv_freq[None, :]
    cos, sin = jnp.cos(ang), jnp.sin(ang)
    half = ROPE_DIM // 2
    pad = HEAD_DIM - ROPE_DIM
    ones = jnp.ones((count, pad), F32)
    zeros_h = jnp.zeros((count, half), F32)
    zeros_p = jnp.zeros((count, pad), F32)
    c_head = jnp.concatenate([cos, cos, ones], axis=1)
    s1_head = jnp.concatenate([-sin, zeros_h, zeros_p], axis=1)
    s2_head = jnp.concatenate([zeros_h, sin, zeros_p], axis=1)
    two = lambda t: jnp.concatenate([t, t], axis=1)
    return two(c_head), two(s1_head), two(s2_head)


def kernel(x, meta_tokens, ffn1_w_gate, ffn1_w_up, ffn1_w_down, ln1_g, ln1_b, w_in, conv_w, conv_b, gate_a_w, gate_a_b, gate_x_w, gate_x_b, lru_lambda, attn_sinks, norm_rec_g, norm_attn_g, w_out, ln2_g, ln2_b, ffn2_w_gate, ffn2_w_up, ffn2_w_down, ln3_g, ln3_b):
    batch, seq, _ = x.shape
    assert seq % MIX_A_ROWS == 0 and seq % MIX_B_ROWS == 0 and (batch * seq) % FFN_ROWS == 0
    assert MIX_A_ROWS == SUBLANES ** 3

    bf = lambda w: w.astype(BF16)
    row = lambda p: p.reshape(p.shape[0], 1, p.shape[1])
    f1g, f1u, f1d = bf(ffn1_w_gate), bf(ffn1_w_up), bf(ffn1_w_down)
    f2g, f2u, f2d = bf(ffn2_w_gate), bf(ffn2_w_up), bf(ffn2_w_down)
    win, wout = bf(w_in), bf(w_out)
    wa, wx = _block_diag_halves(gate_a_w), _block_diag_halves(gate_x_w)
    l1g, l1b, l2g, l2b, l3g, l3b = map(row, (ln1_g, ln1_b, ln2_g, ln2_b, ln3_g, ln3_b))
    cb, ba, bx, lam, grec, gattn = map(
        row, (conv_b, gate_a_b, gate_x_b, lru_lambda, norm_rec_g, norm_attn_g))
    k_tab_m = _rope_tables(0.0, N_META)
    k_tab_x = _rope_tables(float(N_META), seq)
    q_tab_m = tuple(t * Q_SCALE for t in k_tab_m)
    q_tab_x = tuple(t * Q_SCALE for t in k_tab_x)

    h = x.reshape(batch * seq, D_MODEL)
    hm = meta_tokens.astype(x.dtype)
    for l in range(DEPTH):
        hm = _ffn(hm, f1g, f1u, f1d, l1g, l1b, l, N_META)
        h = _ffn(h, f1g, f1u, f1d, l1g, l1b, l, FFN_ROWS)
        hm, ctail, hinit, km4, vm4 = _meta_mix(
            attn_sinks, hm, win, conv_w, cb, wa, ba, wx, bx, lam, grec, q_tab_m, k_tab_m,
            wout, gattn, l2g, l2b, l)
        yrec, q, k4, v4 = _mix_a(h, win, conv_w, cb, wa, ba, wx, bx, lam, grec,
                                 q_tab_x, k_tab_x, ctail, hinit, l, batch, seq)
        h = _mix_b(attn_sinks, q, k4, v4, km4, vm4, yrec, h, wout, gattn, l2g, l2b,
                   l, batch, seq)
        if l + 1 < DEPTH:
            hm = _ffn(hm, f2g, f2u, f2d, l3g, l3b, l, N_META)
        h = _ffn(h, f2g, f2u, f2d, l3g, l3b, l, FFN_ROWS)
    return h.reshape(batch, seq, D_MODEL)
```

```python
import functools

import jax
import jax.numpy as jnp
from jax import lax
from jax.experimental import pallas as pl
from jax.experimental.pallas import tpu as pltpu

F32 = jnp.float32
BF16 = jnp.bfloat16

D_MODEL = 1024
DEPTH = 4
N_META = 16
D_FF = 2816
REC_WIDTH = 512
REC_BLOCKS = 8
REC_BLOCK_DIM = REC_WIDTH // REC_BLOCKS
CONV_WIDTH = 4
LRU_C = 8.0
N_Q_HEADS = 8
N_KV_HEADS = 2
HEAD_DIM = 64
ATTN_WIDTH = N_Q_HEADS * HEAD_DIM
KV_WIDTH = N_KV_HEADS * HEAD_DIM
WINDOW = 128
ROPE_DIM = HEAD_DIM // 4
ROPE_THETA = 500000.0
D_IN = 2 * REC_WIDTH + ATTN_WIDTH + 2 * KV_WIDTH
D_MIX = REC_WIDTH + ATTN_WIDTH
DEEPNORM_ALPHA = (2.0 * DEPTH) ** 0.25
LN_EPS = 1e-5
RMS_EPS = 1e-6
NEG_INF = -1e30
Q_SCALE = HEAD_DIM ** -0.5

LANES = 128
SUBLANES = 8
MXU_DIM = 256
VMEM_LIMIT_BYTES = 56 * 1024 * 1024

FFN_ROWS = 1024
FFN_SUB_ROWS = 512
FF_CHUNK = MXU_DIM
MIX_A_ROWS = 512
MIX_B_ROWS = 512
ATT_BLOCK = 64
OUT_ROWS = 128
OUT_DELAY = 2
QK_AHEAD = 2
GATE_HALF = REC_WIDTH // 2
SERIES_CUTOFF = -1.0 / 64.0


def _layer_norm(y, g, b):
    mu = jnp.mean(y, axis=-1, keepdims=True)
    d = y - mu
    var = jnp.mean(d * d, axis=-1, keepdims=True)
    return d * lax.rsqrt(var + LN_EPS) * g + b


def _rms_norm(y, g):
    ms = jnp.mean(y * y, axis=-1, keepdims=True)
    return y * lax.rsqrt(ms + RMS_EPS) * g


def _dot(a, b):
    return jnp.dot(a, b, preferred_element_type=F32)


def _dot_nt(a, b):
    return lax.dot_general(a, b, (((1,), (1,)), ((), ())), preferred_element_type=F32)


def _ffn_kernel(x_ref, wg_ref, wu_ref, wd_ref, g_ref, b_ref, o_ref, act_ref):
    sub = act_ref.shape[1]
    for r in range(x_ref.shape[0] // sub):
        rs = slice(r * sub, (r + 1) * sub)
        x = x_ref[rs, :]
        xb = x.astype(BF16)
        for c in range(D_FF // FF_CHUNK):
            sl = slice(c * FF_CHUNK, (c + 1) * FF_CHUNK)
            gate = _dot(xb, wg_ref[:, sl])
            up = _dot(xb, wu_ref[:, sl])
            act_ref[r, :, sl] = (gate * jax.nn.sigmoid(gate) * up).astype(BF16)
        y = DEEPNORM_ALPHA * x + 0.5 * _dot(act_ref[r], wd_ref[...])
        o_ref[rs, :] = _layer_norm(y, g_ref[...], b_ref[...])


def _ffn(x, wg, wu, wd, g, b, layer, rows):
    tokens = x.shape[0]
    resident = dict(pipeline_mode=pl.Buffered(1))
    sub = min(rows, FFN_SUB_ROWS)
    return pl.pallas_call(
        _ffn_kernel,
        grid=(tokens // rows,),
        scratch_shapes=[pltpu.VMEM((rows // sub, sub, D_FF), BF16)],
        in_specs=[
            pl.BlockSpec((rows, D_MODEL), lambda i: (i, 0)),
            pl.BlockSpec((None, D_MODEL, D_FF), lambda i: (layer, 0, 0), **resident),
            pl.BlockSpec((None, D_MODEL, D_FF), lambda i: (layer, 0, 0), **resident),
            pl.BlockSpec((None, D_FF, D_MODEL), lambda i: (layer, 0, 0), **resident),
            pl.BlockSpec((None, 1, D_MODEL), lambda i: (layer, 0, 0)),
            pl.BlockSpec((None, 1, D_MODEL), lambda i: (layer, 0, 0)),
        ],
        out_specs=pl.BlockSpec((rows, D_MODEL), lambda i: (i, 0)),
        out_shape=jax.ShapeDtypeStruct((tokens, D_MODEL), F32),
        compiler_params=pltpu.CompilerParams(
            dimension_semantics=("arbitrary",), vmem_limit_bytes=VMEM_LIMIT_BYTES),
    )(x, wg, wu, wd, g, b)


def _softplus(x):
    return jnp.maximum(x, 0.0) + jnp.log1p(jnp.exp(-jnp.abs(x)))


def _conv_and_gates(xr, back, cw_ref, cb_ref, wa_ref, ba_ref, wx_ref, bx_ref, lam_ref):
    xc = cb_ref[...] + back(3) * cw_ref[0:1, :]
    xc = xc + back(2) * cw_ref[1:2, :]
    xc = xc + back(1) * cw_ref[2:3, :]
    xc = xc + xr * cw_ref[3:4, :]
    xcb = xc.astype(BF16)
    lo, hi = xcb[:, :GATE_HALF], xcb[:, GATE_HALF:]
    r_lin = jnp.concatenate([_dot(lo, wa_ref[0]), _dot(hi, wa_ref[1])], axis=1) + ba_ref[...]
    i_lin = jnp.concatenate([_dot(lo, wx_ref[0]), _dot(hi, wx_ref[1])], axis=1) + bx_ref[...]
    r = jax.nn.sigmoid(r_lin)
    i = jax.nn.sigmoid(i_lin)
    log_a = (-LRU_C * _softplus(-lam_ref[...])) * r
    a = jnp.exp(log_a)
    y = 2.0 * log_a
    series = -y * (1.0 + y * (0.5 + y * (1.0 / 6.0)))
    one_minus_a2 = jnp.where(y > SERIES_CUTOFF, series, 1.0 - a * a)
    b = jnp.sqrt(one_minus_a2) * (i * xc)
    return a, b


def _delayed(xr, prev8, d):
    rolled = pltpu.roll(xr, d, 0)
    sub = lax.broadcasted_iota(jnp.int32, prev8.shape, 0)
    head = jnp.where(sub < d, pltpu.roll(prev8, d, 0), rolled[:SUBLANES])
    return jnp.concatenate([head, rolled[SUBLANES:]], axis=0)


def _scan8(a, b):
    rows, width = a.shape
    groups = rows // SUBLANES
    a3 = a.reshape(groups, SUBLANES, width)
    b3 = b.reshape(groups, SUBLANES, width)
    sub = lax.broadcasted_iota(jnp.int32, (groups, SUBLANES, width), 1)
    for s in (1, 2, 4):
        keep = sub >= s
        b_prev = jnp.where(keep, pltpu.roll(b3, s, 1), 0.0)
        a_prev = jnp.where(keep, pltpu.roll(a3, s, 1), 1.0)
        b3 = b3 + a3 * b_prev
        a3 = a3 * a_prev
    return a3.reshape(rows, width), b3.reshape(rows, width)


def _row_bcast(ref, r):
    return jnp.broadcast_to(ref[pl.ds(r, 1), :], (SUBLANES, ref.shape[1]))


def _rope(t, cos, s1, s2):
    cols = []
    for j in range(t.shape[1] // LANES):
        tj = t[:, j * LANES:(j + 1) * LANES]
        up = pltpu.roll(tj, LANES - ROPE_DIM // 2, 1)
        down = pltpu.roll(tj, ROPE_DIM // 2, 1)
        cols.append(tj * cos + up * s1 + down * s2)
    return cols[0] if len(cols) == 1 else jnp.concatenate(cols, axis=1)


def _split_kv(t):
    swapped = pltpu.roll(t, HEAD_DIM, 1)
    low = lax.broadcasted_iota(jnp.int32, t.shape, 1) < HEAD_DIM
    zero = jnp.zeros_like(t)
    parts = [jnp.where(low, t, zero), jnp.where(low, zero, swapped),
             jnp.where(low, swapped, zero), jnp.where(low, zero, t)]
    return jnp.concatenate(parts, axis=1).astype(BF16)


def _gelu_gate_norm(hseq, gate, grec_ref):
    y = hseq * jax.nn.gelu(gate, approximate=True)
    return _rms_norm(y, grec_ref[...]).astype(BF16)


def _pair_probs(s_band, band_ok, s_meta, meta_ok_even, meta_ok_odd, sink_even, sink_odd):
    rows = s_meta.shape[0]
    sme = jnp.where(meta_ok_even, s_meta, NEG_INF)
    smo = jnp.where(meta_ok_odd, s_meta, NEG_INF)
    m_e = jnp.maximum(jnp.max(sme, axis=-1, keepdims=True), sink_even)
    m_o = jnp.maximum(jnp.max(smo, axis=-1, keepdims=True), sink_odd)
    if s_band is not None:
        half = s_band.shape[1] // 2
        se = jnp.where(band_ok, s_band[:, :half], NEG_INF)
        so = jnp.where(band_ok, s_band[:, half:], NEG_INF)
        m_e = jnp.maximum(m_e, jnp.max(se, axis=-1, keepdims=True))
        m_o = jnp.maximum(m_o, jnp.max(so, axis=-1, keepdims=True))
    pme = jnp.exp(sme - m_e)
    pmo = jnp.exp(smo - m_o)
    den_e = jnp.sum(pme, axis=-1, keepdims=True) + jnp.exp(sink_even - m_e)
    den_o = jnp.sum(pmo, axis=-1, keepdims=True) + jnp.exp(sink_odd - m_o)
    p_band = None
    if s_band is not None:
        pe = jnp.exp(se - m_e)
        po = jnp.exp(so - m_o)
        den_e = den_e + jnp.sum(pe, axis=-1, keepdims=True)
        den_o = den_o + jnp.sum(po, axis=-1, keepdims=True)
        p_band = jnp.concatenate([pe, po], axis=1).astype(BF16)
    low = lax.broadcasted_iota(jnp.int32, (rows, LANES), 1) < HEAD_DIM
    inv = jnp.where(low, 1.0 / den_e, 1.0 / den_o)
    return p_band, (pme + pmo).astype(BF16), inv


def _pair_out(p_band, p_meta, inv, v_band, v_meta):
    out = _dot(p_meta, v_meta)
    if p_band is not None:
        out = out + _dot(p_band, v_band)
    return out * inv


def _col(ref_or_val, j, rows=slice(None)):
    return ref_or_val[rows, j * LANES:(j + 1) * LANES]


def _mix_a_kernel(h_ref, win_ref, cw_ref, cb_ref, wa_ref, ba_ref, wx_ref, bx_ref, lam_ref,
                  grec_ref, cosq_ref, s1q_ref, s2q_ref, cos_ref, s1_ref, s2_ref, ctail_ref,
                  hinit_ref,
                  yrec_ref, q_ref, k4_ref, v4_ref,
                  hist, state, a1_buf, b1_buf, a2_buf, b2_buf, h2_buf, h3_buf):
    rows = h_ref.shape[0]
    g1 = rows // SUBLANES
    g2 = g1 // SUBLANES

    @pl.when(pl.program_id(1) == 0)
    def _():
        hist[...] = ctail_ref[...]
        state[...] = hinit_ref[...]

    proj = _dot(h_ref[...].astype(BF16), win_ref[...])
    xr = proj[:, :REC_WIDTH]
    gate = proj[:, REC_WIDTH:2 * REC_WIDTH]
    q = proj[:, 2 * REC_WIDTH:2 * REC_WIDTH + ATTN_WIDTH]
    k = proj[:, 2 * REC_WIDTH + ATTN_WIDTH:2 * REC_WIDTH + ATTN_WIDTH + KV_WIDTH]
    v = proj[:, 2 * REC_WIDTH + ATTN_WIDTH + KV_WIDTH:]

    prev8 = hist[...]
    back = lambda d: _delayed(xr, prev8, d)
    a, b = _conv_and_gates(xr, back, cw_ref, cb_ref, wa_ref, ba_ref, wx_ref, bx_ref, lam_ref)
    hist[...] = xr[rows - SUBLANES:, :]

    ends = lambda buf, c, n: buf[c, pl.ds(SUBLANES - 1, n, stride=SUBLANES), :]
    col_seqs = []
    for c in range(REC_WIDTH // LANES):
        lanes = slice(c * LANES, (c + 1) * LANES)
        h0 = state[:, lanes]
        a1, b1 = _scan8(a[:, lanes], b[:, lanes])
        a1_buf[c] = a1
        b1_buf[c] = b1
        a2, b2 = _scan8(ends(a1_buf, c, g1), ends(b1_buf, c, g1))
        a2_buf[c] = a2
        b2_buf[c] = b2
        a3, b3 = _scan8(ends(a2_buf, c, g2), ends(b2_buf, c, g2))
        h3_buf[c, 0:SUBLANES, :] = h0
        h3_buf[c, SUBLANES:, :] = b3 + a3 * h0
        h2_buf[c, 0:SUBLANES, :] = h0
        for j in range(g2):
            sl = slice(j * SUBLANES, (j + 1) * SUBLANES)
            h2_buf[c, pl.ds(SUBLANES + j * SUBLANES, SUBLANES), :] = (
                b2[sl] + a2[sl] * _row_bcast(h3_buf.at[c], SUBLANES - 1 + j))
        pieces = []
        for j in range(g1):
            sl = slice(j * SUBLANES, (j + 1) * SUBLANES)
            pieces.append(b1[sl] + a1[sl] * _row_bcast(h2_buf.at[c], SUBLANES - 1 + j))
        col_seqs.append(jnp.concatenate(pieces, axis=0))
        state[:, lanes] = _row_bcast(h2_buf.at[c], SUBLANES - 1 + g1)
    hseq = jnp.concatenate(col_seqs, axis=1)

    yrec_ref[...] = _gelu_gate_norm(hseq, gate, grec_ref)
    q_ref[...] = _rope(q, cosq_ref[...], s1q_ref[...], s2q_ref[...]).astype(BF16)
    k4_ref[...] = _split_kv(_rope(k, cos_ref[...], s1_ref[...], s2_ref[...]))
    v4_ref[...] = _split_kv(v)


def _mix_a(h, w_in, conv_w, conv_b, wa, ba, wx, bx, lam, g_rec, q_tables, k_tables, ctail,
           hinit, layer, batch, seq):
    rows = MIX_A_ROWS
    nc = seq // rows
    tokens = batch * seq
    tok = lambda width: pl.BlockSpec((rows, width), lambda bi, ci: (bi * nc + ci, 0))
    vec = lambda width: pl.BlockSpec((None, 1, width), lambda bi, ci: (layer, 0, 0))
    gates = pl.BlockSpec((None, 2, GATE_HALF, GATE_HALF), lambda bi, ci: (layer, 0, 0, 0))
    table = pl.BlockSpec((rows, LANES), lambda bi, ci: (ci, 0))
    seed = pl.BlockSpec((SUBLANES, REC_WIDTH), lambda bi, ci: (0, 0))
    g1 = rows // SUBLANES
    g2 = g1 // SUBLANES
    ncol = REC_WIDTH // LANES
    return pl.pallas_call(
        _mix_a_kernel,
        grid=(batch, nc),
        in_specs=[
            tok(D_MODEL),
            pl.BlockSpec((None, D_MODEL, D_IN), lambda bi, ci: (layer, 0, 0),
                         pipeline_mode=pl.Buffered(1)),
            pl.BlockSpec((None, CONV_WIDTH, REC_WIDTH), lambda bi, ci: (layer, 0, 0)),
            vec(REC_WIDTH), gates, vec(REC_WIDTH), gates, vec(REC_WIDTH), vec(REC_WIDTH),
            vec(REC_WIDTH), table, table, table, table, table, table, seed, seed,
        ],
        out_specs=[tok(REC_WIDTH), tok(ATTN_WIDTH), tok(4 * LANES), tok(4 * LANES)],
        out_shape=[jax.ShapeDtypeStruct((tokens, REC_WIDTH), BF16),
                   jax.ShapeDtypeStruct((tokens, ATTN_WIDTH), BF16),
                   jax.ShapeDtypeStruct((tokens, 4 * LANES), BF16),
                   jax.ShapeDtypeStruct((tokens, 4 * LANES), BF16)],
        scratch_shapes=[
            pltpu.VMEM((SUBLANES, REC_WIDTH), F32),
            pltpu.VMEM((SUBLANES, REC_WIDTH), F32),
            pltpu.VMEM((ncol, rows, LANES), F32),
            pltpu.VMEM((ncol, rows, LANES), F32),
            pltpu.VMEM((ncol, g1, LANES), F32),
            pltpu.VMEM((ncol, g1, LANES), F32),
            pltpu.VMEM((ncol, SUBLANES + g1, LANES), F32),
            pltpu.VMEM((ncol, SUBLANES + g2, LANES), F32),
        ],
        compiler_params=pltpu.CompilerParams(
            dimension_semantics=("arbitrary", "arbitrary"), vmem_limit_bytes=VMEM_LIMIT_BYTES),
    )(h, w_in, conv_w, conv_b, wa, ba, wx, bx, lam, g_rec, *q_tables, *k_tables, ctail, hinit)


def _window_probs(s, ok, sink_even, sink_odd):
    rows, half = s.shape[0], s.shape[1] // 2
    se = jnp.where(ok, s[:, :half], NEG_INF)
    so = jnp.where(ok, s[:, half:], NEG_INF)
    m_e = jnp.maximum(jnp.max(se, axis=-1, keepdims=True), sink_even)
    m_o = jnp.maximum(jnp.max(so, axis=-1, keepdims=True), sink_odd)
    pe = jnp.exp(se - m_e)
    po = jnp.exp(so - m_o)
    den_e = jnp.sum(pe, axis=-1, keepdims=True) + jnp.exp(sink_even - m_e)
    den_o = jnp.sum(po, axis=-1, keepdims=True) + jnp.exp(sink_odd - m_o)
    low = lax.broadcasted_iota(jnp.int32, (rows, LANES), 1) < HEAD_DIM
    inv = jnp.where(low, 1.0 / den_e, 1.0 / den_o)
    return jnp.concatenate([pe, po], axis=1).astype(BF16), inv


def _mix_b_kernel(layer, sink_ref, q_ref, k4_ref, v4_ref, kp_ref, vp_ref, km_ref, vm_ref,
                  yrec_ref, h_ref, wout_ref, gattn_ref, lng_ref, lnb_ref, o_ref):
    rows = q_ref.shape[0]
    first = pl.program_id(1) == 0
    span = WINDOW + ATT_BLOCK
    pad = 2 * WINDOW - span - N_META
    qi = lax.broadcasted_iota(jnp.int32, (ATT_BLOCK, 2 * WINDOW), 0)
    kj = lax.broadcasted_iota(jnp.int32, (ATT_BLOCK, 2 * WINDOW), 1)
    is_meta = kj < N_META
    lo = qi + (N_META + pad + 1)
    hi = qi + (N_META + pad + WINDOW)

    def visible(missing):
        floor = jnp.where(first, N_META + pad + missing, 0)
        return is_meta | ((kj >= jnp.maximum(lo, floor)) & (kj <= hi))

    zeros = jnp.zeros((pad, LANES), BF16)

    def window(ref, prev_ref, meta_ref, jb, g):
        end = (jb + 1) * ATT_BLOCK
        parts = []
        for cc in (2 * g, 2 * g + 1):
            parts += [_col(meta_ref, cc), zeros]
            if end < span:
                parts += [_col(prev_ref, cc, slice(end - span + WINDOW, WINDOW)),
                          _col(ref, cc, slice(0, end))]
            else:
                parts.append(_col(ref, cc, slice(end - span, end)))
        return jnp.concatenate(parts, axis=0)

    items = [(jb, g) for jb in range(rows // ATT_BLOCK) for g in range(N_KV_HEADS)]
    per_group = (OUT_ROWS // ATT_BLOCK) * N_KV_HEADS

    def scores(k):
        jb, g = items[k]
        cur = slice(jb * ATT_BLOCK, (jb + 1) * ATT_BLOCK)
        lhs = jnp.concatenate([_col(q_ref, 2 * g, cur), _col(q_ref, 2 * g + 1, cur)], axis=0)
        return _dot_nt(lhs, window(k4_ref, kp_ref, km_ref, jb, g))

    outs = {}

    def finish(m):
        rs = slice(m * OUT_ROWS, (m + 1) * OUT_ROWS)
        blocks = []
        for jb in range(m * OUT_ROWS // ATT_BLOCK, (m + 1) * OUT_ROWS // ATT_BLOCK):
            y = jnp.concatenate([outs[jb, cc] for cc in range(2 * N_KV_HEADS)], axis=1)
            blocks.append(_rms_norm(y, gattn_ref[...]).astype(BF16))
        y_attn = jnp.concatenate(blocks, axis=0)
        mixed = _dot(jnp.concatenate([yrec_ref[rs, :], y_attn], axis=1), wout_ref[...])
        o_ref[rs, :] = _layer_norm(DEEPNORM_ALPHA * h_ref[rs, :] + mixed,
                                   lng_ref[...], lnb_ref[...])

    ready = [scores(k) for k in range(min(QK_AHEAD, len(items)))]
    done = 0
    for k, (jb, g) in enumerate(items):
        if k + QK_AHEAD < len(items):
            ready.append(scores(k + QK_AHEAD))
        s = ready.pop(0)
        ok = visible(max(span - (jb + 1) * ATT_BLOCK, 0))
        p_cols, inv_cols = [], []
        for c in range(2):
            head = 4 * g + 2 * c
            p, inv = _window_probs(s[c * ATT_BLOCK:(c + 1) * ATT_BLOCK], ok,
                                   sink_ref[layer, head], sink_ref[layer, head + 1])
            p_cols.append(p)
            inv_cols.append(inv)
        out = _dot(jnp.concatenate(p_cols, axis=0), window(v4_ref, vp_ref, vm_ref, jb, g))
        out = out * jnp.concatenate(inv_cols, axis=0)
        outs[jb, 2 * g], outs[jb, 2 * g + 1] = out[:ATT_BLOCK], out[ATT_BLOCK:]
        if k + 1 >= (done + 1) * per_group + OUT_DELAY:
            finish(done)
            done += 1
    for m in range(done, rows // OUT_ROWS):
        finish(m)


def _mix_b(sinks, q, k4, v4, km4, vm4, yrec, h, w_out, g_attn, ln_g, ln_b, layer, batch, seq):
    rows = MIX_B_ROWS
    nc = seq // rows
    blocks_per_step = rows // WINDOW
    blocks_per_seq = seq // WINDOW
    tokens = batch * seq
    tok = lambda width: pl.BlockSpec((rows, width), lambda bi, ci: (bi * nc + ci, 0))
    prev = pl.BlockSpec(
        (WINDOW, 4 * LANES),
        lambda bi, ci: (jnp.maximum(bi * blocks_per_seq + ci * blocks_per_step - 1, 0), 0))
    meta = pl.BlockSpec((N_META, 4 * LANES), lambda bi, ci: (0, 0))
    vec = lambda width: pl.BlockSpec((None, 1, width), lambda bi, ci: (layer, 0, 0))
    return pl.pallas_call(
        functools.partial(_mix_b_kernel, layer),
        grid=(batch, nc),
        in_specs=[
            pl.BlockSpec(memory_space=pltpu.SMEM),
            tok(ATTN_WIDTH), tok(4 * LANES), tok(4 * LANES), prev, prev, meta, meta,
            tok(REC_WIDTH), tok(D_MODEL),
            pl.BlockSpec((None, D_MIX, D_MODEL), lambda bi, ci: (layer, 0, 0),
                         pipeline_mode=pl.Buffered(1)),
            vec(ATTN_WIDTH), vec(D_MODEL), vec(D_MODEL),
        ],
        out_specs=tok(D_MODEL),
        out_shape=jax.ShapeDtypeStruct((tokens, D_MODEL), F32),
        compiler_params=pltpu.CompilerParams(
            dimension_semantics=("arbitrary", "arbitrary"), vmem_limit_bytes=VMEM_LIMIT_BYTES),
    )(sinks, q, k4, v4, k4, v4, km4, vm4, yrec, h, w_out, g_attn, ln_g, ln_b)


def _meta_mix_kernel(layer, sink_ref, h_ref, win_ref, cw_ref, cb_ref, wa_ref, ba_ref, wx_ref,
                     bx_ref, lam_ref, grec_ref, cosq_ref, s1q_ref, s2q_ref, cos_ref, s1_ref,
                     s2_ref, wout_ref, gattn_ref, lng_ref, lnb_ref,
                     o_ref, ctail_ref, hstate_ref, k4_ref, v4_ref, hbuf):
    rows = N_META
    h_in = h_ref[...]
    proj = _dot(h_in.astype(BF16), win_ref[...])
    xr = proj[:, :REC_WIDTH]
    gate = proj[:, REC_WIDTH:2 * REC_WIDTH]
    q = proj[:, 2 * REC_WIDTH:2 * REC_WIDTH + ATTN_WIDTH]
    k = proj[:, 2 * REC_WIDTH + ATTN_WIDTH:2 * REC_WIDTH + ATTN_WIDTH + KV_WIDTH]
    v = proj[:, 2 * REC_WIDTH + ATTN_WIDTH + KV_WIDTH:]

    back = lambda d: _delayed(xr, jnp.zeros((SUBLANES, REC_WIDTH), F32), d)
    a, b = _conv_and_gates(xr, back, cw_ref, cb_ref, wa_ref, ba_ref, wx_ref, bx_ref, lam_ref)
    ctail_ref[...] = xr[rows - SUBLANES:, :]

    a1, b1 = _scan8(a, b)
    hbuf[0:SUBLANES, :] = b1[:SUBLANES]
    hbuf[SUBLANES:, :] = b1[SUBLANES:] + a1[SUBLANES:] * _row_bcast(hbuf, SUBLANES - 1)
    hseq = hbuf[...]
    hstate_ref[...] = _row_bcast(hbuf, rows - 1)
    y_rec = _gelu_gate_norm(hseq, gate, grec_ref)

    qb = _rope(q, cosq_ref[...], s1q_ref[...], s2q_ref[...]).astype(BF16)
    k4 = _split_kv(_rope(k, cos_ref[...], s1_ref[...], s2_ref[...]))
    v4 = _split_kv(v)
    k4_ref[...] = k4
    v4_ref[...] = v4

    qi = lax.broadcasted_iota(jnp.int32, (rows, 2 * N_META), 0)
    lane32 = lax.broadcasted_iota(jnp.int32, (rows, 2 * N_META), 1)
    meta_even = (lane32 < N_META) & (lane32 <= qi)
    meta_odd = (lane32 >= N_META) & (lane32 - N_META <= qi)
    cols = []
    for g in range(N_KV_HEADS):
        lo_c, hi_c = 2 * g, 2 * g + 1
        k_meta = jnp.concatenate([_col(k4, lo_c), _col(k4, hi_c)], axis=0)
        v_meta = jnp.concatenate([_col(v4, lo_c), _col(v4, hi_c)], axis=0)
        lhs = jnp.concatenate([_col(qb, lo_c), _col(qb, hi_c)], axis=0)
        s_meta = _dot_nt(lhs, k_meta)
        for c in range(2):
            head = 4 * g + 2 * c
            pieces = _pair_probs(None, None, s_meta[c * rows:(c + 1) * rows], meta_even, meta_odd,
                                 sink_ref[layer, head], sink_ref[layer, head + 1])
            cols.append(_pair_out(*pieces, None, v_meta))
    y_attn = _rms_norm(jnp.concatenate(cols, axis=1), gattn_ref[...]).astype(BF16)
    mixed = _dot(jnp.concatenate([y_rec, y_attn], axis=1), wout_ref[...])
    o_ref[...] = _layer_norm(DEEPNORM_ALPHA * h_in + mixed, lng_ref[...], lnb_ref[...])


def _meta_mix(sinks, h, w_in, conv_w, conv_b, wa, ba, wx, bx, lam, g_rec, q_tables, k_tables,
              w_out, g_attn, ln_g, ln_b, layer):
    vec = lambda width: pl.BlockSpec((None, 1, width), lambda i: (layer, 0, 0))
    gates = pl.BlockSpec((None, 2, GATE_HALF, GATE_HALF), lambda i: (layer, 0, 0, 0))
    table = pl.BlockSpec((N_META, LANES), lambda i: (0, 0))
    full = lambda r, w: pl.BlockSpec((r, w), lambda i: (0, 0))
    return pl.pallas_call(
        functools.partial(_meta_mix_kernel, layer),
        grid=(1,),
        in_specs=[
            pl.BlockSpec(memory_space=pltpu.SMEM),
            full(N_META, D_MODEL),
            pl.BlockSpec((None, D_MODEL, D_IN), lambda i: (layer, 0, 0)),
            pl.BlockSpec((None, CONV_WIDTH, REC_WIDTH), lambda i: (layer, 0, 0)),
            vec(REC_WIDTH), gates, vec(REC_WIDTH), gates, vec(REC_WIDTH), vec(REC_WIDTH),
            vec(REC_WIDTH), table, table, table, table, table, table,
            pl.BlockSpec((None, D_MIX, D_MODEL), lambda i: (layer, 0, 0)),
            vec(ATTN_WIDTH), vec(D_MODEL), vec(D_MODEL),
        ],
        out_specs=[full(N_META, D_MODEL), full(SUBLANES, REC_WIDTH), full(SUBLANES, REC_WIDTH),
                   full(N_META, 4 * LANES), full(N_META, 4 * LANES)],
        out_shape=[jax.ShapeDtypeStruct((N_META, D_MODEL), F32),
                   jax.ShapeDtypeStruct((SUBLANES, REC_WIDTH), F32),
                   jax.ShapeDtypeStruct((SUBLANES, REC_WIDTH), F32),
                   jax.ShapeDtypeStruct((N_META, 4 * LANES), BF16),
                   jax.ShapeDtypeStruct((N_META, 4 * LANES), BF16)],
        scratch_shapes=[pltpu.VMEM((N_META, REC_WIDTH), F32)],
        compiler_params=pltpu.CompilerParams(
            dimension_semantics=("arbitrary",), vmem_limit_bytes=VMEM_LIMIT_BYTES),
    )(sinks, h, w_in, conv_w, conv_b, wa, ba, wx, bx, lam, g_rec, *q_tables, *k_tables, w_out,
      g_attn, ln_g, ln_b)


def _block_diag_halves(w):
    per_half = REC_BLOCKS // 2
    out = jnp.zeros((w.shape[0], 2, GATE_HALF, GATE_HALF), w.dtype)
    for blk in range(REC_BLOCKS):
        half, pos = divmod(blk, per_half)
        sl = slice(pos * REC_BLOCK_DIM, (pos + 1) * REC_BLOCK_DIM)
        out = out.at[:, half, sl, sl].set(w[:, blk])
    return out.astype(BF16)


def _rope_tables(first_pos, count):
    pos = first_pos + jnp.arange(count, dtype=F32)
    inv_freq = ROPE_THETA ** (-jnp.arange(0, ROPE_DIM, 2, dtype=F32) / ROPE_DIM)
    ang = pos[:, None] * inv_freq[None, :]
    cos, sin = jnp.cos(ang), jnp.sin(ang)
    half = ROPE_DIM // 2
    pad = HEAD_DIM - ROPE_DIM
    ones = jnp.ones((count, pad), F32)
    zeros_h = jnp.zeros((count, half), F32)
    zeros_p = jnp.zeros((count, pad), F32)
    c_head = jnp.concatenate([cos, cos, ones], axis=1)
    s1_head = jnp.concatenate([-sin, zeros_h, zeros_p], axis=1)
    s2_head = jnp.concatenate([zeros_h, sin, zeros_p], axis=1)
    two = lambda t: jnp.concatenate([t, t], axis=1)
    return two(c_head), two(s1_head), two(s2_head)


def kernel(x, meta_tokens, ffn1_w_gate, ffn1_w_up, ffn1_w_down, ln1_g, ln1_b, w_in, conv_w, conv_b, gate_a_w, gate_a_b, gate_x_w, gate_x_b, lru_lambda, attn_sinks, norm_rec_g, norm_attn_g, w_out, ln2_g, ln2_b, ffn2_w_gate, ffn2_w_up, ffn2_w_down, ln3_g, ln3_b):
    batch, seq, _ = x.shape
    assert seq % MIX_A_ROWS == 0 and seq % MIX_B_ROWS == 0 and (batch * seq) % FFN_ROWS == 0
    assert MIX_A_ROWS == SUBLANES ** 3

    bf = lambda w: w.astype(BF16)
    row = lambda p: p.reshape(p.shape[0], 1, p.shape[1])
    f1g, f1u, f1d = bf(ffn1_w_gate), bf(ffn1_w_up), bf(ffn1_w_down)
    f2g, f2u, f2d = bf(ffn2_w_gate), bf(ffn2_w_up), bf(ffn2_w_down)
    win, wout = bf(w_in), bf(w_out)
    wa, wx = _block_diag_halves(gate_a_w), _block_diag_halves(gate_x_w)
    l1g, l1b, l2g, l2b, l3g, l3b = map(row, (ln1_g, ln1_b, ln2_g, ln2_b, ln3_g, ln3_b))
    cb, ba, bx, lam, grec, gattn = map(
        row, (conv_b, gate_a_b, gate_x_b, lru_lambda, norm_rec_g, norm_attn_g))
    k_tab_m = _rope_tables(0.0, N_META)
    k_tab_x = _rope_tables(float(N_META), seq)
    q_tab_m = tuple(t * Q_SCALE for t in k_tab_m)
    q_tab_x = tuple(t * Q_SCALE for t in k_tab_x)

    h = x.reshape(batch * seq, D_MODEL)
    hm = meta_tokens.astype(x.dtype)
    for l in range(DEPTH):
        hm = _ffn(hm, f1g, f1u, f1d, l1g, l1b, l, N_META)
        h = _ffn(h, f1g, f1u, f1d, l1g, l1b, l, FFN_ROWS)
        hm, ctail, hinit, km4, vm4 = _meta_mix(
            attn_sinks, hm, win, conv_w, cb, wa, ba, wx, bx, lam, grec, q_tab_m, k_tab_m,
            wout, gattn, l2g, l2b, l)
        yrec, q, k4, v4 = _mix_a(h, win, conv_w, cb, wa, ba, wx, bx, lam, grec,
                                 q_tab_x, k_tab_x, ctail, hinit, l, batch, seq)
        h = _mix_b(attn_sinks, q, k4, v4, km4, vm4, yrec, h, wout, gattn, l2g, l2b,
                   l, batch, seq)
        if l + 1 < DEPTH:
            hm = _ffn(hm, f2g, f2u, f2d, l3g, l3b, l, N_META)
        h = _ffn(h, f2g, f2u, f2d, l3g, l3b, l, FFN_ROWS)
    return h.reshape(batch, seq, D_MODEL)
```

```python
import functools

import jax
import jax.numpy as jnp
from jax import lax
from jax.experimental import pallas as pl
from jax.experimental.pallas import tpu as pltpu

F32 = jnp.float32
BF16 = jnp.bfloat16

D_MODEL = 1024
DEPTH = 4
N_META = 16
D_FF = 2816
REC_WIDTH = 512
REC_BLOCKS = 8
REC_BLOCK_DIM = REC_WIDTH // REC_BLOCKS
CONV_WIDTH = 4
LRU_C = 8.0
N_Q_HEADS = 8
N_KV_HEADS = 2
HEAD_DIM = 64
ATTN_WIDTH = N_Q_HEADS * HEAD_DIM
KV_WIDTH = N_KV_HEADS * HEAD_DIM
WINDOW = 128
ROPE_DIM = HEAD_DIM // 4
ROPE_THETA = 500000.0
D_IN = 2 * REC_WIDTH + ATTN_WIDTH + 2 * KV_WIDTH
D_MIX = REC_WIDTH + ATTN_WIDTH
DEEPNORM_ALPHA = (2.0 * DEPTH) ** 0.25
LN_EPS = 1e-5
RMS_EPS = 1e-6
NEG_INF = -1e30
Q_SCALE = HEAD_DIM ** -0.5

LANES = 128
SUBLANES = 8
MXU_DIM = 256
VMEM_LIMIT_BYTES = 56 * 1024 * 1024

FFN_ROWS = 1024
FFN_SUB_ROWS = 512
FF_CHUNK = MXU_DIM
MIX_A_ROWS = 512
MIX_B_ROWS = 512
ATT_BLOCK = 64
OUT_ROWS = 128
OUT_DELAY = 2
SEG_PITCH = 72
QK_AHEAD = 2
GATE_HALF = REC_WIDTH // 2
SERIES_CUTOFF = -1.0 / 64.0


def _layer_norm(y, g, b):
    mu = jnp.mean(y, axis=-1, keepdims=True)
    d = y - mu
    var = jnp.mean(d * d, axis=-1, keepdims=True)
    return d * lax.rsqrt(var + LN_EPS) * g + b


def _rms_norm(y, g):
    ms = jnp.mean(y * y, axis=-1, keepdims=True)
    return y * lax.rsqrt(ms + RMS_EPS) * g


def _dot(a, b):
    return jnp.dot(a, b, preferred_element_type=F32)


def _dot_nt(a, b):
    return lax.dot_general(a, b, (((1,), (1,)), ((), ())), preferred_element_type=F32)


def _ffn_kernel(x_ref, wg_ref, wu_ref, wd_ref, g_ref, b_ref, o_ref, act_ref):
    sub = act_ref.shape[1]
    for r in range(x_ref.shape[0] // sub):
        rs = slice(r * sub, (r + 1) * sub)
        x = x_ref[rs, :]
        xb = x.astype(BF16)
        for c in range(D_FF // FF_CHUNK):
            sl = slice(c * FF_CHUNK, (c + 1) * FF_CHUNK)
            gate = _dot(xb, wg_ref[:, sl])
            up = _dot(xb, wu_ref[:, sl])
            act_ref[r, :, sl] = (gate * jax.nn.sigmoid(gate) * up).astype(BF16)
        y = DEEPNORM_ALPHA * x + 0.5 * _dot(act_ref[r], wd_ref[...])
        o_ref[rs, :] = _layer_norm(y, g_ref[...], b_ref[...])


def _ffn(x, wg, wu, wd, g, b, layer, rows):
    tokens = x.shape[0]
    resident = dict(pipeline_mode=pl.Buffered(1))
    sub = min(rows, FFN_SUB_ROWS)
    return pl.pallas_call(
        _ffn_kernel,
        grid=(tokens // rows,),
        scratch_shapes=[pltpu.VMEM((rows // sub, sub, D_FF), BF16)],
        in_specs=[
            pl.BlockSpec((rows, D_MODEL), lambda i: (i, 0)),
            pl.BlockSpec((None, D_MODEL, D_FF), lambda i: (layer, 0, 0), **resident),
            pl.BlockSpec((None, D_MODEL, D_FF), lambda i: (layer, 0, 0), **resident),
            pl.BlockSpec((None, D_FF, D_MODEL), lambda i: (layer, 0, 0), **resident),
            pl.BlockSpec((None, 1, D_MODEL), lambda i: (layer, 0, 0)),
            pl.BlockSpec((None, 1, D_MODEL), lambda i: (layer, 0, 0)),
        ],
        out_specs=pl.BlockSpec((rows, D_MODEL), lambda i: (i, 0)),
        out_shape=jax.ShapeDtypeStruct((tokens, D_MODEL), F32),
        compiler_params=pltpu.CompilerParams(
            dimension_semantics=("arbitrary",), vmem_limit_bytes=VMEM_LIMIT_BYTES),
    )(x, wg, wu, wd, g, b)


def _softplus(x):
    return jnp.maximum(x, 0.0) + jnp.log1p(jnp.exp(-jnp.abs(x)))


def _conv_and_gates(xr, back, cw_ref, cb_ref, wa_ref, ba_ref, wx_ref, bx_ref, lam_ref):
    xc = cb_ref[...] + back(3) * cw_ref[0:1, :]
    xc = xc + back(2) * cw_ref[1:2, :]
    xc = xc + back(1) * cw_ref[2:3, :]
    xc = xc + xr * cw_ref[3:4, :]
    xcb = xc.astype(BF16)
    lo, hi = xcb[:, :GATE_HALF], xcb[:, GATE_HALF:]
    r_lin = jnp.concatenate([_dot(lo, wa_ref[0]), _dot(hi, wa_ref[1])], axis=1) + ba_ref[...]
    i_lin = jnp.concatenate([_dot(lo, wx_ref[0]), _dot(hi, wx_ref[1])], axis=1) + bx_ref[...]
    r = jax.nn.sigmoid(r_lin)
    i = jax.nn.sigmoid(i_lin)
    log_a = (-LRU_C * _softplus(-lam_ref[...])) * r
    a = jnp.exp(log_a)
    y = 2.0 * log_a
    series = -y * (1.0 + y * (0.5 + y * (1.0 / 6.0)))
    one_minus_a2 = jnp.where(y > SERIES_CUTOFF, series, 1.0 - a * a)
    b = jnp.sqrt(one_minus_a2) * (i * xc)
    return a, b


def _delayed(xr, prev8, d):
    rolled = pltpu.roll(xr, d, 0)
    sub = lax.broadcasted_iota(jnp.int32, prev8.shape, 0)
    head = jnp.where(sub < d, pltpu.roll(prev8, d, 0), rolled[:SUBLANES])
    return jnp.concatenate([head, rolled[SUBLANES:]], axis=0)


def _scan8(a, b):
    rows, width = a.shape
    groups = rows // SUBLANES
    a3 = a.reshape(groups, SUBLANES, width)
    b3 = b.reshape(groups, SUBLANES, width)
    sub = lax.broadcasted_iota(jnp.int32, (groups, SUBLANES, width), 1)
    for s in (1, 2, 4):
        keep = sub >= s
        b_prev = jnp.where(keep, pltpu.roll(b3, s, 1), 0.0)
        a_prev = jnp.where(keep, pltpu.roll(a3, s, 1), 1.0)
        b3 = b3 + a3 * b_prev
        a3 = a3 * a_prev
    return a3.reshape(rows, width), b3.reshape(rows, width)


def _row_bcast(ref, r):
    return jnp.broadcast_to(ref[pl.ds(r, 1), :], (SUBLANES, ref.shape[1]))


def _rope(t, cos, s1, s2):
    cols = []
    for j in range(t.shape[1] // LANES):
        tj = t[:, j * LANES:(j + 1) * LANES]
        up = pltpu.roll(tj, LANES - ROPE_DIM // 2, 1)
        down = pltpu.roll(tj, ROPE_DIM // 2, 1)
        cols.append(tj * cos + up * s1 + down * s2)
    return cols[0] if len(cols) == 1 else jnp.concatenate(cols, axis=1)


def _split_kv(t):
    swapped = pltpu.roll(t, HEAD_DIM, 1)
    low = lax.broadcasted_iota(jnp.int32, t.shape, 1) < HEAD_DIM
    zero = jnp.zeros_like(t)
    parts = [jnp.where(low, t, zero), jnp.where(low, zero, swapped),
             jnp.where(low, swapped, zero), jnp.where(low, zero, t)]
    return jnp.concatenate(parts, axis=1).astype(BF16)


def _gelu_gate_norm(hseq, gate, grec_ref):
    y = hseq * jax.nn.gelu(gate, approximate=True)
    return _rms_norm(y, grec_ref[...]).astype(BF16)


def _pair_probs(s_band, band_ok, s_meta, meta_ok_even, meta_ok_odd, sink_even, sink_odd):
    rows = s_meta.shape[0]
    sme = jnp.where(meta_ok_even, s_meta, NEG_INF)
    smo = jnp.where(meta_ok_odd, s_meta, NEG_INF)
    m_e = jnp.maximum(jnp.max(sme, axis=-1, keepdims=True), sink_even)
    m_o = jnp.maximum(jnp.max(smo, axis=-1, keepdims=True), sink_odd)
    if s_band is not None:
        half = s_band.shape[1] // 2
        se = jnp.where(band_ok, s_band[:, :half], NEG_INF)
        so = jnp.where(band_ok, s_band[:, half:], NEG_INF)
        m_e = jnp.maximum(m_e, jnp.max(se, axis=-1, keepdims=True))
        m_o = jnp.maximum(m_o, jnp.max(so, axis=-1, keepdims=True))
    pme = jnp.exp(sme - m_e)
    pmo = jnp.exp(smo - m_o)
    den_e = jnp.sum(pme, axis=-1, keepdims=True) + jnp.exp(sink_even - m_e)
    den_o = jnp.sum(pmo, axis=-1, keepdims=True) + jnp.exp(sink_odd - m_o)
    p_band = None
    if s_band is not None:
        pe = jnp.exp(se - m_e)
        po = jnp.exp(so - m_o)
        den_e = den_e + jnp.sum(pe, axis=-1, keepdims=True)
        den_o = den_o + jnp.sum(po, axis=-1, keepdims=True)
        p_band = jnp.concatenate([pe, po], axis=1).astype(BF16)
    low = lax.broadcasted_iota(jnp.int32, (rows, LANES), 1) < HEAD_DIM
    inv = jnp.where(low, 1.0 / den_e, 1.0 / den_o)
    return p_band, (pme + pmo).astype(BF16), inv


def _pair_out(p_band, p_meta, inv, v_band, v_meta):
    out = _dot(p_meta, v_meta)
    if p_band is not None:
        out = out + _dot(p_band, v_band)
    return out * inv


def _col(ref_or_val, j, rows=slice(None)):
    return ref_or_val[rows, j * LANES:(j + 1) * LANES]


def _mix_a_kernel(h_ref, win_ref, cw_ref, cb_ref, wa_ref, ba_ref, wx_ref, bx_ref, lam_ref,
                  grec_ref, cosq_ref, s1q_ref, s2q_ref, cos_ref, s1_ref, s2_ref, ctail_ref,
                  hinit_ref,
                  yrec_ref, q_ref, k4_ref, v4_ref,
                  hist, state, a_buf, b_buf, h_buf):
    rows = h_ref.shape[0]
    ncol = REC_WIDTH // LANES

    @pl.when(pl.program_id(1) == 0)
    def _():
        hist[...] = ctail_ref[...]
        state[...] = hinit_ref[...]

    proj = _dot(h_ref[...].astype(BF16), win_ref[...])
    xr = proj[:, :REC_WIDTH]
    gate = proj[:, REC_WIDTH:2 * REC_WIDTH]
    q = proj[:, 2 * REC_WIDTH:2 * REC_WIDTH + ATTN_WIDTH]
    k = proj[:, 2 * REC_WIDTH + ATTN_WIDTH:2 * REC_WIDTH + ATTN_WIDTH + KV_WIDTH]
    v = proj[:, 2 * REC_WIDTH + ATTN_WIDTH + KV_WIDTH:]

    prev8 = hist[...]
    back = lambda d: _delayed(xr, prev8, d)
    a, b = _conv_and_gates(xr, back, cw_ref, cb_ref, wa_ref, ba_ref, wx_ref, bx_ref, lam_ref)
    hist[...] = xr[rows - SUBLANES:, :]

    seg = rows // SUBLANES
    for c in range(ncol):
        for s in range(SUBLANES):
            src_rows, lanes = slice(s * seg, (s + 1) * seg), slice(c * LANES, (c + 1) * LANES)
            a_buf[c, pl.ds(s * SEG_PITCH, seg), :] = a[src_rows, lanes]
            b_buf[c, pl.ds(s * SEG_PITCH, seg), :] = b[src_rows, lanes]
    sub = lax.broadcasted_iota(jnp.int32, (SUBLANES, LANES), 0)
    col_seqs = []
    for c in range(ncol):
        lanes = slice(c * LANES, (c + 1) * LANES)
        step = lambda buf, j: buf[c, pl.ds(j, SUBLANES, stride=SEG_PITCH), :]
        prod = step(a_buf, 0)
        loc = step(b_buf, 0)
        prods, locs = [prod], [loc]
        for j in range(1, seg):
            aj = step(a_buf, j)
            loc = aj * loc + step(b_buf, j)
            prod = aj * prod
            prods.append(prod)
            locs.append(loc)
        h0 = state[:, lanes]
        pa, pb = _scan8(prod, loc)
        after = pb + pa * h0
        enter = jnp.where(sub == 0, h0, pltpu.roll(after, 1, 0))
        for j in range(seg):
            h_buf[c, pl.ds(j, SUBLANES, stride=SEG_PITCH), :] = locs[j] + prods[j] * enter
        h_buf[c, pl.ds(seg, SUBLANES), :] = after
        state[:, lanes] = _row_bcast(h_buf.at[c], seg + SUBLANES - 1)
        col_seqs.append(jnp.concatenate(
            [h_buf[c, pl.ds(s * SEG_PITCH, seg), :] for s in range(SUBLANES)], axis=0))
    hseq = jnp.concatenate(col_seqs, axis=1)

    yrec_ref[...] = _gelu_gate_norm(hseq, gate, grec_ref)
    q_ref[...] = _rope(q, cosq_ref[...], s1q_ref[...], s2q_ref[...]).astype(BF16)
    k4_ref[...] = _split_kv(_rope(k, cos_ref[...], s1_ref[...], s2_ref[...]))
    v4_ref[...] = _split_kv(v)


def _mix_a(h, w_in, conv_w, conv_b, wa, ba, wx, bx, lam, g_rec, q_tables, k_tables, ctail,
           hinit, layer, batch, seq):
    rows = MIX_A_ROWS
    nc = seq // rows
    tokens = batch * seq
    tok = lambda width: pl.BlockSpec((rows, width), lambda bi, ci: (bi * nc + ci, 0))
    vec = lambda width: pl.BlockSpec((None, 1, width), lambda bi, ci: (layer, 0, 0))
    gates = pl.BlockSpec((None, 2, GATE_HALF, GATE_HALF), lambda bi, ci: (layer, 0, 0, 0))
    table = pl.BlockSpec((rows, LANES), lambda bi, ci: (ci, 0))
    seed = pl.BlockSpec((SUBLANES, REC_WIDTH), lambda bi, ci: (0, 0))
    ncol = REC_WIDTH // LANES
    return pl.pallas_call(
        _mix_a_kernel,
        grid=(batch, nc),
        in_specs=[
            tok(D_MODEL),
            pl.BlockSpec((None, D_MODEL, D_IN), lambda bi, ci: (layer, 0, 0),
                         pipeline_mode=pl.Buffered(1)),
            pl.BlockSpec((None, CONV_WIDTH, REC_WIDTH), lambda bi, ci: (layer, 0, 0)),
            vec(REC_WIDTH), gates, vec(REC_WIDTH), gates, vec(REC_WIDTH), vec(REC_WIDTH),
            vec(REC_WIDTH), table, table, table, table, table, table, seed, seed,
        ],
        out_specs=[tok(REC_WIDTH), tok(ATTN_WIDTH), tok(4 * LANES), tok(4 * LANES)],
        out_shape=[jax.ShapeDtypeStruct((tokens, REC_WIDTH), BF16),
                   jax.ShapeDtypeStruct((tokens, ATTN_WIDTH), BF16),
                   jax.ShapeDtypeStruct((tokens, 4 * LANES), BF16),
                   jax.ShapeDtypeStruct((tokens, 4 * LANES), BF16)],
        scratch_shapes=[
            pltpu.VMEM((SUBLANES, REC_WIDTH), F32),
            pltpu.VMEM((SUBLANES, REC_WIDTH), F32),
            pltpu.VMEM((ncol, SUBLANES * SEG_PITCH, LANES), F32),
            pltpu.VMEM((ncol, SUBLANES * SEG_PITCH, LANES), F32),
            pltpu.VMEM((ncol, SUBLANES * SEG_PITCH, LANES), F32),
        ],
        compiler_params=pltpu.CompilerParams(
            dimension_semantics=("arbitrary", "arbitrary"), vmem_limit_bytes=VMEM_LIMIT_BYTES),
    )(h, w_in, conv_w, conv_b, wa, ba, wx, bx, lam, g_rec, *q_tables, *k_tables, ctail, hinit)


def _window_probs(s, ok, sink_even, sink_odd):
    rows, half = s.shape[0], s.shape[1] // 2
    se = jnp.where(ok, s[:, :half], NEG_INF)
    so = jnp.where(ok, s[:, half:], NEG_INF)
    m_e = jnp.maximum(jnp.max(se, axis=-1, keepdims=True), sink_even)
    m_o = jnp.maximum(jnp.max(so, axis=-1, keepdims=True), sink_odd)
    pe = jnp.exp(se - m_e)
    po = jnp.exp(so - m_o)
    den_e = jnp.sum(pe, axis=-1, keepdims=True) + jnp.exp(sink_even - m_e)
    den_o = jnp.sum(po, axis=-1, keepdims=True) + jnp.exp(sink_odd - m_o)
    low = lax.broadcasted_iota(jnp.int32, (rows, LANES), 1) < HEAD_DIM
    inv = jnp.where(low, 1.0 / den_e, 1.0 / den_o)
    return jnp.concatenate([pe, po], axis=1).astype(BF16), inv


def _mix_b_kernel(layer, sink_ref, q_ref, k4_ref, v4_ref, kp_ref, vp_ref, km_ref, vm_ref,
                  yrec_ref, h_ref, wout_ref, gattn_ref, lng_ref, lnb_ref, o_ref):
    rows = q_ref.shape[0]
    first = pl.program_id(1) == 0
    span = WINDOW + ATT_BLOCK
    pad = 2 * WINDOW - span - N_META
    qi = lax.broadcasted_iota(jnp.int32, (ATT_BLOCK, 2 * WINDOW), 0)
    kj = lax.broadcasted_iota(jnp.int32, (ATT_BLOCK, 2 * WINDOW), 1)
    is_meta = kj < N_META
    lo = qi + (N_META + pad + 1)
    hi = qi + (N_META + pad + WINDOW)

    def visible(missing):
        floor = jnp.where(first, N_META + pad + missing, 0)
        return is_meta | ((kj >= jnp.maximum(lo, floor)) & (kj <= hi))

    zeros = jnp.zeros((pad, LANES), BF16)

    def window(ref, prev_ref, meta_ref, jb, g):
        end = (jb + 1) * ATT_BLOCK
        parts = []
        for cc in (2 * g, 2 * g + 1):
            parts += [_col(meta_ref, cc), zeros]
            if end < span:
                parts += [_col(prev_ref, cc, slice(end - span + WINDOW, WINDOW)),
                          _col(ref, cc, slice(0, end))]
            else:
                parts.append(_col(ref, cc, slice(end - span, end)))
        return jnp.concatenate(parts, axis=0)

    items = [(jb, g) for jb in range(rows // ATT_BLOCK) for g in range(N_KV_HEADS)]
    per_group = (OUT_ROWS // ATT_BLOCK) * N_KV_HEADS

    def scores(k):
        jb, g = items[k]
        cur = slice(jb * ATT_BLOCK, (jb + 1) * ATT_BLOCK)
        lhs = jnp.concatenate([_col(q_ref, 2 * g, cur), _col(q_ref, 2 * g + 1, cur)], axis=0)
        return _dot_nt(lhs, window(k4_ref, kp_ref, km_ref, jb, g))

    outs = {}

    def finish(m):
        rs = slice(m * OUT_ROWS, (m + 1) * OUT_ROWS)
        blocks = []
        for jb in range(m * OUT_ROWS // ATT_BLOCK, (m + 1) * OUT_ROWS // ATT_BLOCK):
            y = jnp.concatenate([outs[jb, cc] for cc in range(2 * N_KV_HEADS)], axis=1)
            blocks.append(_rms_norm(y, gattn_ref[...]).astype(BF16))
        y_attn = jnp.concatenate(blocks, axis=0)
        mixed = _dot(jnp.concatenate([yrec_ref[rs, :], y_attn], axis=1), wout_ref[...])
        o_ref[rs, :] = _layer_norm(DEEPNORM_ALPHA * h_ref[rs, :] + mixed,
                                   lng_ref[...], lnb_ref[...])

    ready = [scores(k) for k in range(min(QK_AHEAD, len(items)))]
    done = 0
    for k, (jb, g) in enumerate(items):
        if k + QK_AHEAD < len(items):
            ready.append(scores(k + QK_AHEAD))
        s = ready.pop(0)
        ok = visible(max(span - (jb + 1) * ATT_BLOCK, 0))
        p_cols, inv_cols = [], []
        for c in range(2):
            head = 4 * g + 2 * c
            p, inv = _window_probs(s[c * ATT_BLOCK:(c + 1) * ATT_BLOCK], ok,
                                   sink_ref[layer, head], sink_ref[layer, head + 1])
            p_cols.append(p)
            inv_cols.append(inv)
        out = _dot(jnp.concatenate(p_cols, axis=0), window(v4_ref, vp_ref, vm_ref, jb, g))
        out = out * jnp.concatenate(inv_cols, axis=0)
        outs[jb, 2 * g], outs[jb, 2 * g + 1] = out[:ATT_BLOCK], out[ATT_BLOCK:]
        if k + 1 >= (done + 1) * per_group + OUT_DELAY:
            finish(done)
            done += 1
    for m in range(done, rows // OUT_ROWS):
        finish(m)


def _mix_b(sinks, q, k4, v4, km4, vm4, yrec, h, w_out, g_attn, ln_g, ln_b, layer, batch, seq):
    rows = MIX_B_ROWS
    nc = seq // rows
    blocks_per_step = rows // WINDOW
    blocks_per_seq = seq // WINDOW
    tokens = batch * seq
    tok = lambda width: pl.BlockSpec((rows, width), lambda bi, ci: (bi * nc + ci, 0))
    prev = pl.BlockSpec(
        (WINDOW, 4 * LANES),
        lambda bi, ci: (jnp.maximum(bi * blocks_per_seq + ci * blocks_per_step - 1, 0), 0))
    meta = pl.BlockSpec((N_META, 4 * LANES), lambda bi, ci: (0, 0))
    vec = lambda width: pl.BlockSpec((None, 1, width), lambda bi, ci: (layer, 0, 0))
    return pl.pallas_call(
        functools.partial(_mix_b_kernel, layer),
        grid=(batch, nc),
        in_specs=[
            pl.BlockSpec(memory_space=pltpu.SMEM),
            tok(ATTN_WIDTH), tok(4 * LANES), tok(4 * LANES), prev, prev, meta, meta,
            tok(REC_WIDTH), tok(D_MODEL),
            pl.BlockSpec((None, D_MIX, D_MODEL), lambda bi, ci: (layer, 0, 0),
                         pipeline_mode=pl.Buffered(1)),
            vec(ATTN_WIDTH), vec(D_MODEL), vec(D_MODEL),
        ],
        out_specs=tok(D_MODEL),
        out_shape=jax.ShapeDtypeStruct((tokens, D_MODEL), F32),
        compiler_params=pltpu.CompilerParams(
            dimension_semantics=("arbitrary", "arbitrary"), vmem_limit_bytes=VMEM_LIMIT_BYTES),
    )(sinks, q, k4, v4, k4, v4, km4, vm4, yrec, h, w_out, g_attn, ln_g, ln_b)


def _meta_mix_kernel(layer, sink_ref, h_ref, win_ref, cw_ref, cb_ref, wa_ref, ba_ref, wx_ref,
                     bx_ref, lam_ref, grec_ref, cosq_ref, s1q_ref, s2q_ref, cos_ref, s1_ref,
                     s2_ref, wout_ref, gattn_ref, lng_ref, lnb_ref,
                     o_ref, ctail_ref, hstate_ref, k4_ref, v4_ref, hbuf):
    rows = N_META
    h_in = h_ref[...]
    proj = _dot(h_in.astype(BF16), win_ref[...])
    xr = proj[:, :REC_WIDTH]
    gate = proj[:, REC_WIDTH:2 * REC_WIDTH]
    q = proj[:, 2 * REC_WIDTH:2 * REC_WIDTH + ATTN_WIDTH]
    k = proj[:, 2 * REC_WIDTH + ATTN_WIDTH:2 * REC_WIDTH + ATTN_WIDTH + KV_WIDTH]
    v = proj[:, 2 * REC_WIDTH + ATTN_WIDTH + KV_WIDTH:]

    back = lambda d: _delayed(xr, jnp.zeros((SUBLANES, REC_WIDTH), F32), d)
    a, b = _conv_and_gates(xr, back, cw_ref, cb_ref, wa_ref, ba_ref, wx_ref, bx_ref, lam_ref)
    ctail_ref[...] = xr[rows - SUBLANES:, :]

    a1, b1 = _scan8(a, b)
    hbuf[0:SUBLANES, :] = b1[:SUBLANES]
    hbuf[SUBLANES:, :] = b1[SUBLANES:] + a1[SUBLANES:] * _row_bcast(hbuf, SUBLANES - 1)
    hseq = hbuf[...]
    hstate_ref[...] = _row_bcast(hbuf, rows - 1)
    y_rec = _gelu_gate_norm(hseq, gate, grec_ref)

    qb = _rope(q, cosq_ref[...], s1q_ref[...], s2q_ref[...]).astype(BF16)
    k4 = _split_kv(_rope(k, cos_ref[...], s1_ref[...], s2_ref[...]))
    v4 = _split_kv(v)
    k4_ref[...] = k4
    v4_ref[...] = v4

    qi = lax.broadcasted_iota(jnp.int32, (rows, 2 * N_META), 0)
    lane32 = lax.broadcasted_iota(jnp.int32, (rows, 2 * N_META), 1)
    meta_even = (lane32 < N_META) & (lane32 <= qi)
    meta_odd = (lane32 >= N_META) & (lane32 - N_META <= qi)
    cols = []
    for g in range(N_KV_HEADS):
        lo_c, hi_c = 2 * g, 2 * g + 1
        k_meta = jnp.concatenate([_col(k4, lo_c), _col(k4, hi_c)], axis=0)
        v_meta = jnp.concatenate([_col(v4, lo_c), _col(v4, hi_c)], axis=0)
        lhs = jnp.concatenate([_col(qb, lo_c), _col(qb, hi_c)], axis=0)
        s_meta = _dot_nt(lhs, k_meta)
        for c in range(2):
            head = 4 * g + 2 * c
            pieces = _pair_probs(None, None, s_meta[c * rows:(c + 1) * rows], meta_even, meta_odd,
                                 sink_ref[layer, head], sink_ref[layer, head + 1])
            cols.append(_pair_out(*pieces, None, v_meta))
    y_attn = _rms_norm(jnp.concatenate(cols, axis=1), gattn_ref[...]).astype(BF16)
    mixed = _dot(jnp.concatenate([y_rec, y_attn], axis=1), wout_ref[...])
    o_ref[...] = _layer_norm(DEEPNORM_ALPHA * h_in + mixed, lng_ref[...], lnb_ref[...])


def _meta_mix(sinks, h, w_in, conv_w, conv_b, wa, ba, wx, bx, lam, g_rec, q_tables, k_tables,
              w_out, g_attn, ln_g, ln_b, layer):
    vec = lambda width: pl.BlockSpec((None, 1, width), lambda i: (layer, 0, 0))
    gates = pl.BlockSpec((None, 2, GATE_HALF, GATE_HALF), lambda i: (layer, 0, 0, 0))
    table = pl.BlockSpec((N_META, LANES), lambda i: (0, 0))
    full = lambda r, w: pl.BlockSpec((r, w), lambda i: (0, 0))
    return pl.pallas_call(
        functools.partial(_meta_mix_kernel, layer),
        grid=(1,),
        in_specs=[
            pl.BlockSpec(memory_space=pltpu.SMEM),
            full(N_META, D_MODEL),
            pl.BlockSpec((None, D_MODEL, D_IN), lambda i: (layer, 0, 0)),
            pl.BlockSpec((None, CONV_WIDTH, REC_WIDTH), lambda i: (layer, 0, 0)),
            vec(REC_WIDTH), gates, vec(REC_WIDTH), gates, vec(REC_WIDTH), vec(REC_WIDTH),
            vec(REC_WIDTH), table, table, table, table, table, table,
            pl.BlockSpec((None, D_MIX, D_MODEL), lambda i: (layer, 0, 0)),
            vec(ATTN_WIDTH), vec(D_MODEL), vec(D_MODEL),
        ],
        out_specs=[full(N_META, D_MODEL), full(SUBLANES, REC_WIDTH), full(SUBLANES, REC_WIDTH),
                   full(N_META, 4 * LANES), full(N_META, 4 * LANES)],
        out_shape=[jax.ShapeDtypeStruct((N_META, D_MODEL), F32),
                   jax.ShapeDtypeStruct((SUBLANES, REC_WIDTH), F32),
                   jax.ShapeDtypeStruct((SUBLANES, REC_WIDTH), F32),
                   jax.ShapeDtypeStruct((N_META, 4 * LANES), BF16),
                   jax.ShapeDtypeStruct((N_META, 4 * LANES), BF16)],
        scratch_shapes=[pltpu.VMEM((N_META, REC_WIDTH), F32)],
        compiler_params=pltpu.CompilerParams(
            dimension_semantics=("arbitrary",), vmem_limit_bytes=VMEM_LIMIT_BYTES),
    )(sinks, h, w_in, conv_w, conv_b, wa, ba, wx, bx, lam, g_rec, *q_tables, *k_tables, w_out,
      g_attn, ln_g, ln_b)


def _block_diag_halves(w):
    per_half = REC_BLOCKS // 2
    out = jnp.zeros((w.shape[0], 2, GATE_HALF, GATE_HALF), w.dtype)
    for blk in range(REC_BLOCKS):
        half, pos = divmod(blk, per_half)
        sl = slice(pos * REC_BLOCK_DIM, (pos + 1) * REC_BLOCK_DIM)
        out = out.at[:, half, sl, sl].set(w[:, blk])
    return out.astype(BF16)


def _rope_tables(first_pos, count):
    pos = first_pos + jnp.arange(count, dtype=F32)
    inv_freq = ROPE_THETA ** (-jnp.arange(0, ROPE_DIM, 2, dtype=F32) / ROPE_DIM)
    ang = pos[:, None] * inv_freq[None, :]
    cos, sin = jnp.cos(ang), jnp.sin(ang)
    half = ROPE_DIM // 2
    pad = HEAD_DIM - ROPE_DIM
    ones = jnp.ones((count, pad), F32)
    zeros_h = jnp.zeros((count, half), F32)
    zeros_p = jnp.zeros((count, pad), F32)
    c_head = jnp.concatenate([cos, cos, ones], axis=1)
    s1_head = jnp.concatenate([-sin, zeros_h, zeros_p], axis=1)
    s2_head = jnp.concatenate([zeros_h, sin, zeros_p], axis=1)
    two = lambda t: jnp.concatenate([t, t], axis=1)
    return two(c_head), two(s1_head), two(s2_head)


def kernel(x, meta_tokens, ffn1_w_gate, ffn1_w_up, ffn1_w_down, ln1_g, ln1_b, w_in, conv_w, conv_b, gate_a_w, gate_a_b, gate_x_w, gate_x_b, lru_lambda, attn_sinks, norm_rec_g, norm_attn_g, w_out, ln2_g, ln2_b, ffn2_w_gate, ffn2_w_up, ffn2_w_down, ln3_g, ln3_b):
    batch, seq, _ = x.shape
    assert seq % MIX_A_ROWS == 0 and seq % MIX_B_ROWS == 0 and (batch * seq) % FFN_ROWS == 0
    assert MIX_A_ROWS // SUBLANES <= SEG_PITCH and SEG_PITCH % SUBLANES == 0

    bf = lambda w: w.astype(BF16)
    row = lambda p: p.reshape(p.shape[0], 1, p.shape[1])
    f1g, f1u, f1d = bf(ffn1_w_gate), bf(ffn1_w_up), bf(ffn1_w_down)
    f2g, f2u, f2d = bf(ffn2_w_gate), bf(ffn2_w_up), bf(ffn2_w_down)
    win, wout = bf(w_in), bf(w_out)
    wa, wx = _block_diag_halves(gate_a_w), _block_diag_halves(gate_x_w)
    l1g, l1b, l2g, l2b, l3g, l3b = map(row, (ln1_g, ln1_b, ln2_g, ln2_b, ln3_g, ln3_b))
    cb, ba, bx, lam, grec, gattn = map(
        row, (conv_b, gate_a_b, gate_x_b, lru_lambda, norm_rec_g, norm_attn_g))
    k_tab_m = _rope_tables(0.0, N_META)
    k_tab_x = _rope_tables(float(N_META), seq)
    q_tab_m = tuple(t * Q_SCALE for t in k_tab_m)
    q_tab_x = tuple(t * Q_SCALE for t in k_tab_x)

    h = x.reshape(batch * seq, D_MODEL)
    hm = meta_tokens.astype(x.dtype)
    for l in range(DEPTH):
        hm = _ffn(hm, f1g, f1u, f1d, l1g, l1b, l, N_META)
        h = _ffn(h, f1g, f1u, f1d, l1g, l1b, l, FFN_ROWS)
        hm, ctail, hinit, km4, vm4 = _meta_mix(
            attn_sinks, hm, win, conv_w, cb, wa, ba, wx, bx, lam, grec, q_tab_m, k_tab_m,
            wout, gattn, l2g, l2b, l)
        yrec, q, k4, v4 = _mix_a(h, win, conv_w, cb, wa, ba, wx, bx, lam, grec,
                                 q_tab_x, k_tab_x, ctail, hinit, l, batch, seq)
        h = _mix_b(attn_sinks, q, k4, v4, km4, vm4, yrec, h, wout, gattn, l2g, l2b,
                   l, batch, seq)
        if l + 1 < DEPTH:
            hm = _ffn(hm, f2g, f2u, f2d, l3g, l3b, l, N_META)
        h = _ffn(h, f2g, f2u, f2d, l3g, l3b, l, FFN_ROWS)
    return h.reshape(batch, seq, D_MODEL)
```

```python
import functools

import jax
import jax.numpy as jnp
from jax import lax
from jax.experimental import pallas as pl
from jax.experimental.pallas import tpu as pltpu

F32 = jnp.float32
BF16 = jnp.bfloat16

D_MODEL = 1024
DEPTH = 4
N_META = 16
D_FF = 2816
REC_WIDTH = 512
REC_BLOCKS = 8
REC_BLOCK_DIM = REC_WIDTH // REC_BLOCKS
CONV_WIDTH = 4
LRU_C = 8.0
N_Q_HEADS = 8
N_KV_HEADS = 2
HEAD_DIM = 64
ATTN_WIDTH = N_Q_HEADS * HEAD_DIM
KV_WIDTH = N_KV_HEADS * HEAD_DIM
WINDOW = 128
ROPE_DIM = HEAD_DIM // 4
ROPE_THETA = 500000.0
D_IN = 2 * REC_WIDTH + ATTN_WIDTH + 2 * KV_WIDTH
D_MIX = REC_WIDTH + ATTN_WIDTH
DEEPNORM_ALPHA = (2.0 * DEPTH) ** 0.25
LN_EPS = 1e-5
RMS_EPS = 1e-6
NEG_INF = -1e30
Q_SCALE = HEAD_DIM ** -0.5

LANES = 128
SUBLANES = 8
MXU_DIM = 256
VMEM_LIMIT_BYTES = 56 * 1024 * 1024

FFN_ROWS = 1024
FFN_SUB_ROWS = 512
FF_CHUNK = MXU_DIM
MIX_A_ROWS = 512
MIX_B_ROWS = 512
ATT_BLOCK = 64
OUT_ROWS = 256
OUT_DELAY = 4
SEG_PITCH = 72
QK_AHEAD = 2
GATE_HALF = REC_WIDTH // 2
SERIES_CUTOFF = -1.0 / 64.0


def _layer_norm(y, g, b):
    mu = jnp.mean(y, axis=-1, keepdims=True)
    d = y - mu
    var = jnp.mean(d * d, axis=-1, keepdims=True)
    return d * lax.rsqrt(var + LN_EPS) * g + b


def _rms_norm(y, g):
    ms = jnp.mean(y * y, axis=-1, keepdims=True)
    return y * lax.rsqrt(ms + RMS_EPS) * g


def _dot(a, b):
    return jnp.dot(a, b, preferred_element_type=F32)


def _dot_nt(a, b):
    return lax.dot_general(a, b, (((1,), (1,)), ((), ())), preferred_element_type=F32)


def _ffn_kernel(x_ref, wg_ref, wu_ref, wd_ref, g_ref, b_ref, o_ref, act_ref):
    sub = act_ref.shape[1]
    for r in range(x_ref.shape[0] // sub):
        rs = slice(r * sub, (r + 1) * sub)
        x = x_ref[rs, :]
        xb = x.astype(BF16)
        for c in range(D_FF // FF_CHUNK):
            sl = slice(c * FF_CHUNK, (c + 1) * FF_CHUNK)
            gate = _dot(xb, wg_ref[:, sl])
            up = _dot(xb, wu_ref[:, sl])
            act_ref[r, :, sl] = (gate * jax.nn.sigmoid(gate) * up).astype(BF16)
        y = DEEPNORM_ALPHA * x + 0.5 * _dot(act_ref[r], wd_ref[...])
        o_ref[rs, :] = _layer_norm(y, g_ref[...], b_ref[...])


def _ffn(x, wg, wu, wd, g, b, layer, rows):
    tokens = x.shape[0]
    resident = dict(pipeline_mode=pl.Buffered(1))
    sub = min(rows, FFN_SUB_ROWS)
    return pl.pallas_call(
        _ffn_kernel,
        grid=(tokens // rows,),
        scratch_shapes=[pltpu.VMEM((rows // sub, sub, D_FF), BF16)],
        in_specs=[
            pl.BlockSpec((rows, D_MODEL), lambda i: (i, 0)),
            pl.BlockSpec((None, D_MODEL, D_FF), lambda i: (layer, 0, 0), **resident),
            pl.BlockSpec((None, D_MODEL, D_FF), lambda i: (layer, 0, 0), **resident),
            pl.BlockSpec((None, D_FF, D_MODEL), lambda i: (layer, 0, 0), **resident),
            pl.BlockSpec((None, 1, D_MODEL), lambda i: (layer, 0, 0)),
            pl.BlockSpec((None, 1, D_MODEL), lambda i: (layer, 0, 0)),
        ],
        out_specs=pl.BlockSpec((rows, D_MODEL), lambda i: (i, 0)),
        out_shape=jax.ShapeDtypeStruct((tokens, D_MODEL), F32),
        compiler_params=pltpu.CompilerParams(
            dimension_semantics=("arbitrary",), vmem_limit_bytes=VMEM_LIMIT_BYTES),
    )(x, wg, wu, wd, g, b)


def _softplus(x):
    return jnp.maximum(x, 0.0) + jnp.log1p(jnp.exp(-jnp.abs(x)))


def _conv(xr, back, cw_ref, cb_ref):
    xc = cb_ref[...] + back(3) * cw_ref[0:1, :]
    xc = xc + back(2) * cw_ref[1:2, :]
    xc = xc + back(1) * cw_ref[2:3, :]
    return xc + xr * cw_ref[3:4, :]


def _gate_terms(xc, wa_ref, ba_ref, wx_ref, bx_ref, lam_ref):
    xcb = xc.astype(BF16)
    lo, hi = xcb[:, :GATE_HALF], xcb[:, GATE_HALF:]
    r_lin = jnp.concatenate([_dot(lo, wa_ref[0]), _dot(hi, wa_ref[1])], axis=1) + ba_ref[...]
    i_lin = jnp.concatenate([_dot(lo, wx_ref[0]), _dot(hi, wx_ref[1])], axis=1) + bx_ref[...]
    r = jax.nn.sigmoid(r_lin)
    i = jax.nn.sigmoid(i_lin)
    log_a = (-LRU_C * _softplus(-lam_ref[...])) * r
    a = jnp.exp(log_a)
    y = 2.0 * log_a
    series = -y * (1.0 + y * (0.5 + y * (1.0 / 6.0)))
    one_minus_a2 = jnp.where(y > SERIES_CUTOFF, series, 1.0 - a * a)
    b = jnp.sqrt(one_minus_a2) * (i * xc)
    return a, b


def _delayed(xr, prev8, d):
    rolled = pltpu.roll(xr, d, 0)
    sub = lax.broadcasted_iota(jnp.int32, prev8.shape, 0)
    head = jnp.where(sub < d, pltpu.roll(prev8, d, 0), rolled[:SUBLANES])
    return jnp.concatenate([head, rolled[SUBLANES:]], axis=0)


def _scan8(a, b):
    rows, width = a.shape
    groups = rows // SUBLANES
    a3 = a.reshape(groups, SUBLANES, width)
    b3 = b.reshape(groups, SUBLANES, width)
    sub = lax.broadcasted_iota(jnp.int32, (groups, SUBLANES, width), 1)
    for s in (1, 2, 4):
        keep = sub >= s
        b_prev = jnp.where(keep, pltpu.roll(b3, s, 1), 0.0)
        a_prev = jnp.where(keep, pltpu.roll(a3, s, 1), 1.0)
        b3 = b3 + a3 * b_prev
        a3 = a3 * a_prev
    return a3.reshape(rows, width), b3.reshape(rows, width)


def _row_bcast(ref, r):
    return jnp.broadcast_to(ref[pl.ds(r, 1), :], (SUBLANES, ref.shape[1]))


def _rope(t, cos, s1, s2):
    cols = []
    for j in range(t.shape[1] // LANES):
        tj = t[:, j * LANES:(j + 1) * LANES]
        up = pltpu.roll(tj, LANES - ROPE_DIM // 2, 1)
        down = pltpu.roll(tj, ROPE_DIM // 2, 1)
        cols.append(tj * cos + up * s1 + down * s2)
    return cols[0] if len(cols) == 1 else jnp.concatenate(cols, axis=1)


def _split_kv(t):
    swapped = pltpu.roll(t, HEAD_DIM, 1)
    low = lax.broadcasted_iota(jnp.int32, t.shape, 1) < HEAD_DIM
    zero = jnp.zeros_like(t)
    parts = [jnp.where(low, t, zero), jnp.where(low, zero, swapped),
             jnp.where(low, swapped, zero), jnp.where(low, zero, t)]
    return jnp.concatenate(parts, axis=1).astype(BF16)


def _gelu_gate_norm(hseq, gate, grec_ref):
    y = hseq * jax.nn.gelu(gate, approximate=True)
    return _rms_norm(y, grec_ref[...]).astype(BF16)


def _pair_probs(s_band, band_ok, s_meta, meta_ok_even, meta_ok_odd, sink_even, sink_odd):
    rows = s_meta.shape[0]
    sme = jnp.where(meta_ok_even, s_meta, NEG_INF)
    smo = jnp.where(meta_ok_odd, s_meta, NEG_INF)
    m_e = jnp.maximum(jnp.max(sme, axis=-1, keepdims=True), sink_even)
    m_o = jnp.maximum(jnp.max(smo, axis=-1, keepdims=True), sink_odd)
    if s_band is not None:
        half = s_band.shape[1] // 2
        se = jnp.where(band_ok, s_band[:, :half], NEG_INF)
        so = jnp.where(band_ok, s_band[:, half:], NEG_INF)
        m_e = jnp.maximum(m_e, jnp.max(se, axis=-1, keepdims=True))
        m_o = jnp.maximum(m_o, jnp.max(so, axis=-1, keepdims=True))
    pme = jnp.exp(sme - m_e)
    pmo = jnp.exp(smo - m_o)
    den_e = jnp.sum(pme, axis=-1, keepdims=True) + jnp.exp(sink_even - m_e)
    den_o = jnp.sum(pmo, axis=-1, keepdims=True) + jnp.exp(sink_odd - m_o)
    p_band = None
    if s_band is not None:
        pe = jnp.exp(se - m_e)
        po = jnp.exp(so - m_o)
        den_e = den_e + jnp.sum(pe, axis=-1, keepdims=True)
        den_o = den_o + jnp.sum(po, axis=-1, keepdims=True)
        p_band = jnp.concatenate([pe, po], axis=1).astype(BF16)
    low = lax.broadcasted_iota(jnp.int32, (rows, LANES), 1) < HEAD_DIM
    inv = jnp.where(low, 1.0 / den_e, 1.0 / den_o)
    return p_band, (pme + pmo).astype(BF16), inv


def _pair_out(p_band, p_meta, inv, v_band, v_meta):
    out = _dot(p_meta, v_meta)
    if p_band is not None:
        out = out + _dot(p_band, v_band)
    return out * inv


def _col(ref_or_val, j, rows=slice(None)):
    return ref_or_val[rows, j * LANES:(j + 1) * LANES]


def _mix_a_kernel(h_ref, win_ref, cw_ref, cb_ref, wa_ref, ba_ref, wx_ref, bx_ref, lam_ref,
                  grec_ref, cosq_ref, s1q_ref, s2q_ref, cos_ref, s1_ref, s2_ref, ctail_ref,
                  hinit_ref,
                  yrec_ref, q_ref, k4_ref, v4_ref,
                  hist, state, a_buf, b_buf, h_buf):
    rows = h_ref.shape[0]
    ncol = REC_WIDTH // LANES

    @pl.when(pl.program_id(1) == 0)
    def _():
        hist[...] = ctail_ref[...]
        state[...] = hinit_ref[...]

    hb = h_ref[...].astype(BF16)
    project = lambda lo, hi: _dot(hb, win_ref[:, lo:hi])
    xr = project(0, REC_WIDTH)
    gate = project(REC_WIDTH, 2 * REC_WIDTH)
    prev8 = hist[...]
    xc = _conv(xr, lambda d: _delayed(xr, prev8, d), cw_ref, cb_ref)
    hist[...] = xr[rows - SUBLANES:, :]
    a, b = _gate_terms(xc, wa_ref, ba_ref, wx_ref, bx_ref, lam_ref)
    q = project(2 * REC_WIDTH, 2 * REC_WIDTH + ATTN_WIDTH)
    kv = project(2 * REC_WIDTH + ATTN_WIDTH, D_IN)
    k, v = kv[:, :KV_WIDTH], kv[:, KV_WIDTH:]

    seg = rows // SUBLANES
    for c in range(ncol):
        for s in range(SUBLANES):
            src_rows, lanes = slice(s * seg, (s + 1) * seg), slice(c * LANES, (c + 1) * LANES)
            a_buf[c, pl.ds(s * SEG_PITCH, seg), :] = a[src_rows, lanes]
            b_buf[c, pl.ds(s * SEG_PITCH, seg), :] = b[src_rows, lanes]
    sub = lax.broadcasted_iota(jnp.int32, (SUBLANES, LANES), 0)
    col_seqs = []
    for c in range(ncol):
        lanes = slice(c * LANES, (c + 1) * LANES)
        step = lambda buf, j: buf[c, pl.ds(j, SUBLANES, stride=SEG_PITCH), :]
        prod = step(a_buf, 0)
        loc = step(b_buf, 0)
        prods, locs = [prod], [loc]
        for j in range(1, seg):
            aj = step(a_buf, j)
            loc = aj * loc + step(b_buf, j)
            prod = aj * prod
            prods.append(prod)
            locs.append(loc)
        h0 = state[:, lanes]
        pa, pb = _scan8(prod, loc)
        after = pb + pa * h0
        enter = jnp.where(sub == 0, h0, pltpu.roll(after, 1, 0))
        for j in range(seg):
            h_buf[c, pl.ds(j, SUBLANES, stride=SEG_PITCH), :] = locs[j] + prods[j] * enter
        h_buf[c, pl.ds(seg, SUBLANES), :] = after
        state[:, lanes] = _row_bcast(h_buf.at[c], seg + SUBLANES - 1)
        col_seqs.append(jnp.concatenate(
            [h_buf[c, pl.ds(s * SEG_PITCH, seg), :] for s in range(SUBLANES)], axis=0))
    hseq = jnp.concatenate(col_seqs, axis=1)

    yrec_ref[...] = _gelu_gate_norm(hseq, gate, grec_ref)
    q_ref[...] = _rope(q, cosq_ref[...], s1q_ref[...], s2q_ref[...]).astype(BF16)
    k4_ref[...] = _split_kv(_rope(k, cos_ref[...], s1_ref[...], s2_ref[...]))
    v4_ref[...] = _split_kv(v)


def _mix_a(h, w_in, conv_w, conv_b, wa, ba, wx, bx, lam, g_rec, q_tables, k_tables, ctail,
           hinit, layer, batch, seq):
    rows = MIX_A_ROWS
    nc = seq // rows
    tokens = batch * seq
    tok = lambda width: pl.BlockSpec((rows, width), lambda bi, ci: (bi * nc + ci, 0))
    vec = lambda width: pl.BlockSpec((None, 1, width), lambda bi, ci: (layer, 0, 0))
    gates = pl.BlockSpec((None, 2, GATE_HALF, GATE_HALF), lambda bi, ci: (layer, 0, 0, 0))
    table = pl.BlockSpec((rows, LANES), lambda bi, ci: (ci, 0))
    seed = pl.BlockSpec((SUBLANES, REC_WIDTH), lambda bi, ci: (0, 0))
    ncol = REC_WIDTH // LANES
    return pl.pallas_call(
        _mix_a_kernel,
        grid=(batch, nc),
        in_specs=[
            tok(D_MODEL),
            pl.BlockSpec((None, D_MODEL, D_IN), lambda bi, ci: (layer, 0, 0),
                         pipeline_mode=pl.Buffered(1)),
            pl.BlockSpec((None, CONV_WIDTH, REC_WIDTH), lambda bi, ci: (layer, 0, 0)),
            vec(REC_WIDTH), gates, vec(REC_WIDTH), gates, vec(REC_WIDTH), vec(REC_WIDTH),
            vec(REC_WIDTH), table, table, table, table, table, table, seed, seed,
        ],
        out_specs=[tok(REC_WIDTH), tok(ATTN_WIDTH), tok(4 * LANES), tok(4 * LANES)],
        out_shape=[jax.ShapeDtypeStruct((tokens, REC_WIDTH), BF16),
                   jax.ShapeDtypeStruct((tokens, ATTN_WIDTH), BF16),
                   jax.ShapeDtypeStruct((tokens, 4 * LANES), BF16),
                   jax.ShapeDtypeStruct((tokens, 4 * LANES), BF16)],
        scratch_shapes=[
            pltpu.VMEM((SUBLANES, REC_WIDTH), F32),
            pltpu.VMEM((SUBLANES, REC_WIDTH), F32),
            pltpu.VMEM((ncol, SUBLANES * SEG_PITCH, LANES), F32),
            pltpu.VMEM((ncol, SUBLANES * SEG_PITCH, LANES), F32),
            pltpu.VMEM((ncol, SUBLANES * SEG_PITCH, LANES), F32),
        ],
        compiler_params=pltpu.CompilerParams(
            dimension_semantics=("arbitrary", "arbitrary"), vmem_limit_bytes=VMEM_LIMIT_BYTES),
    )(h, w_in, conv_w, conv_b, wa, ba, wx, bx, lam, g_rec, *q_tables, *k_tables, ctail, hinit)


def _window_probs(s, ok, sink_even, sink_odd):
    rows, half = s.shape[0], s.shape[1] // 2
    se = jnp.where(ok, s[:, :half], NEG_INF)
    so = jnp.where(ok, s[:, half:], NEG_INF)
    m_e = jnp.maximum(jnp.max(se, axis=-1, keepdims=True), sink_even)
    m_o = jnp.maximum(jnp.max(so, axis=-1, keepdims=True), sink_odd)
    pe = jnp.exp(se - m_e)
    po = jnp.exp(so - m_o)
    den_e = jnp.sum(pe, axis=-1, keepdims=True) + jnp.exp(sink_even - m_e)
    den_o = jnp.sum(po, axis=-1, keepdims=True) + jnp.exp(sink_odd - m_o)
    low = lax.broadcasted_iota(jnp.int32, (rows, LANES), 1) < HEAD_DIM
    inv = jnp.where(low, 1.0 / den_e, 1.0 / den_o)
    return jnp.concatenate([pe, po], axis=1).astype(BF16), inv


def _mix_b_kernel(layer, sink_ref, q_ref, k4_ref, v4_ref, kp_ref, vp_ref, km_ref, vm_ref,
                  yrec_ref, h_ref, wout_ref, gattn_ref, lng_ref, lnb_ref, o_ref):
    rows = q_ref.shape[0]
    first = pl.program_id(1) == 0
    span = WINDOW + ATT_BLOCK
    pad = 2 * WINDOW - span - N_META
    qi = lax.broadcasted_iota(jnp.int32, (ATT_BLOCK, 2 * WINDOW), 0)
    kj = lax.broadcasted_iota(jnp.int32, (ATT_BLOCK, 2 * WINDOW), 1)
    is_meta = kj < N_META
    lo = qi + (N_META + pad + 1)
    hi = qi + (N_META + pad + WINDOW)

    def visible(missing):
        floor = jnp.where(first, N_META + pad + missing, 0)
        return is_meta | ((kj >= jnp.maximum(lo, floor)) & (kj <= hi))

    zeros = jnp.zeros((pad, LANES), BF16)

    def window(ref, prev_ref, meta_ref, jb, g):
        end = (jb + 1) * ATT_BLOCK
        parts = []
        for cc in (2 * g, 2 * g + 1):
            parts += [_col(meta_ref, cc), zeros]
            if end < span:
                parts += [_col(prev_ref, cc, slice(end - span + WINDOW, WINDOW)),
                          _col(ref, cc, slice(0, end))]
            else:
                parts.append(_col(ref, cc, slice(end - span, end)))
        return jnp.concatenate(parts, axis=0)

    items = [(jb, g) for jb in range(rows // ATT_BLOCK) for g in range(N_KV_HEADS)]
    per_group = (OUT_ROWS // ATT_BLOCK) * N_KV_HEADS

    def scores(k):
        jb, g = items[k]
        cur = slice(jb * ATT_BLOCK, (jb + 1) * ATT_BLOCK)
        lhs = jnp.concatenate([_col(q_ref, 2 * g, cur), _col(q_ref, 2 * g + 1, cur)], axis=0)
        return _dot_nt(lhs, window(k4_ref, kp_ref, km_ref, jb, g))

    outs = {}

    def finish(m):
        rs = slice(m * OUT_ROWS, (m + 1) * OUT_ROWS)
        blocks = []
        for jb in range(m * OUT_ROWS // ATT_BLOCK, (m + 1) * OUT_ROWS // ATT_BLOCK):
            y = jnp.concatenate([outs[jb, cc] for cc in range(2 * N_KV_HEADS)], axis=1)
            blocks.append(_rms_norm(y, gattn_ref[...]).astype(BF16))
        y_attn = jnp.concatenate(blocks, axis=0)
        mixed = _dot(jnp.concatenate([yrec_ref[rs, :], y_attn], axis=1), wout_ref[...])
        o_ref[rs, :] = _layer_norm(DEEPNORM_ALPHA * h_ref[rs, :] + mixed,
                                   lng_ref[...], lnb_ref[...])

    ready = [scores(k) for k in range(min(QK_AHEAD, len(items)))]
    done = 0
    for k, (jb, g) in enumerate(items):
        if k + QK_AHEAD < len(items):
            ready.append(scores(k + QK_AHEAD))
        s = ready.pop(0)
        ok = visible(max(span - (jb + 1) * ATT_BLOCK, 0))
        p_cols, inv_cols = [], []
        for c in range(2):
            head = 4 * g + 2 * c
            p, inv = _window_probs(s[c * ATT_BLOCK:(c + 1) * ATT_BLOCK], ok,
                                   sink_ref[layer, head], sink_ref[layer, head + 1])
            p_cols.append(p)
            inv_cols.append(inv)
        out = _dot(jnp.concatenate(p_cols, axis=0), window(v4_ref, vp_ref, vm_ref, jb, g))
        out = out * jnp.concatenate(inv_cols, axis=0)
        outs[jb, 2 * g], outs[jb, 2 * g + 1] = out[:ATT_BLOCK], out[ATT_BLOCK:]
        if k + 1 >= (done + 1) * per_group + OUT_DELAY:
            finish(done)
            done += 1
    for m in range(done, rows // OUT_ROWS):
        finish(m)


def _mix_b(sinks, q, k4, v4, km4, vm4, yrec, h, w_out, g_attn, ln_g, ln_b, layer, batch, seq):
    rows = MIX_B_ROWS
    nc = seq // rows
    blocks_per_step = rows // WINDOW
    blocks_per_seq = seq // WINDOW
    tokens = batch * seq
    tok = lambda width: pl.BlockSpec((rows, width), lambda bi, ci: (bi * nc + ci, 0))
    prev = pl.BlockSpec(
        (WINDOW, 4 * LANES),
        lambda bi, ci: (jnp.maximum(bi * blocks_per_seq + ci * blocks_per_step - 1, 0), 0))
    meta = pl.BlockSpec((N_META, 4 * LANES), lambda bi, ci: (0, 0))
    vec = lambda width: pl.BlockSpec((None, 1, width), lambda bi, ci: (layer, 0, 0))
    return pl.pallas_call(
        functools.partial(_mix_b_kernel, layer),
        grid=(batch, nc),
        in_specs=[
            pl.BlockSpec(memory_space=pltpu.SMEM),
            tok(ATTN_WIDTH), tok(4 * LANES), tok(4 * LANES), prev, prev, meta, meta,
            tok(REC_WIDTH), tok(D_MODEL),
            pl.BlockSpec((None, D_MIX, D_MODEL), lambda bi, ci: (layer, 0, 0),
                         pipeline_mode=pl.Buffered(1)),
            vec(ATTN_WIDTH), vec(D_MODEL), vec(D_MODEL),
        ],
        out_specs=tok(D_MODEL),
        out_shape=jax.ShapeDtypeStruct((tokens, D_MODEL), F32),
        compiler_params=pltpu.CompilerParams(
            dimension_semantics=("arbitrary", "arbitrary"), vmem_limit_bytes=VMEM_LIMIT_BYTES),
    )(sinks, q, k4, v4, k4, v4, km4, vm4, yrec, h, w_out, g_attn, ln_g, ln_b)


def _meta_mix_kernel(layer, sink_ref, h_ref, win_ref, cw_ref, cb_ref, wa_ref, ba_ref, wx_ref,
                     bx_ref, lam_ref, grec_ref, cosq_ref, s1q_ref, s2q_ref, cos_ref, s1_ref,
                     s2_ref, wout_ref, gattn_ref, lng_ref, lnb_ref,
                     o_ref, ctail_ref, hstate_ref, k4_ref, v4_ref, hbuf):
    rows = N_META
    h_in = h_ref[...]
    proj = _dot(h_in.astype(BF16), win_ref[...])
    xr = proj[:, :REC_WIDTH]
    gate = proj[:, REC_WIDTH:2 * REC_WIDTH]
    q = proj[:, 2 * REC_WIDTH:2 * REC_WIDTH + ATTN_WIDTH]
    k = proj[:, 2 * REC_WIDTH + ATTN_WIDTH:2 * REC_WIDTH + ATTN_WIDTH + KV_WIDTH]
    v = proj[:, 2 * REC_WIDTH + ATTN_WIDTH + KV_WIDTH:]

    back = lambda d: _delayed(xr, jnp.zeros((SUBLANES, REC_WIDTH), F32), d)
    a, b = _gate_terms(_conv(xr, back, cw_ref, cb_ref), wa_ref, ba_ref, wx_ref, bx_ref, lam_ref)
    ctail_ref[...] = xr[rows - SUBLANES:, :]

    a1, b1 = _scan8(a, b)
    hbuf[0:SUBLANES, :] = b1[:SUBLANES]
    hbuf[SUBLANES:, :] = b1[SUBLANES:] + a1[SUBLANES:] * _row_bcast(hbuf, SUBLANES - 1)
    hseq = hbuf[...]
    hstate_ref[...] = _row_bcast(hbuf, rows - 1)
    y_rec = _gelu_gate_norm(hseq, gate, grec_ref)

    qb = _rope(q, cosq_ref[...], s1q_ref[...], s2q_ref[...]).astype(BF16)
    k4 = _split_kv(_rope(k, cos_ref[...], s1_ref[...], s2_ref[...]))
    v4 = _split_kv(v)
    k4_ref[...] = k4
    v4_ref[...] = v4

    qi = lax.broadcasted_iota(jnp.int32, (rows, 2 * N_META), 0)
    lane32 = lax.broadcasted_iota(jnp.int32, (rows, 2 * N_META), 1)
    meta_even = (lane32 < N_META) & (lane32 <= qi)
    meta_odd = (lane32 >= N_META) & (lane32 - N_META <= qi)
    cols = []
    for g in range(N_KV_HEADS):
        lo_c, hi_c = 2 * g, 2 * g + 1
        k_meta = jnp.concatenate([_col(k4, lo_c), _col(k4, hi_c)], axis=0)
        v_meta = jnp.concatenate([_col(v4, lo_c), _col(v4, hi_c)], axis=0)
        lhs = jnp.concatenate([_col(qb, lo_c), _col(qb, hi_c)], axis=0)
        s_meta = _dot_nt(lhs, k_meta)
        for c in range(2):
            head = 4 * g + 2 * c
            pieces = _pair_probs(None, None, s_meta[c * rows:(c + 1) * rows], meta_even, meta_odd,
                                 sink_ref[layer, head], sink_ref[layer, head + 1])
            cols.append(_pair_out(*pieces, None, v_meta))
    y_attn = _rms_norm(jnp.concatenate(cols, axis=1), gattn_ref[...]).astype(BF16)
    mixed = _dot(jnp.concatenate([y_rec, y_attn], axis=1), wout_ref[...])
    o_ref[...] = _layer_norm(DEEPNORM_ALPHA * h_in + mixed, lng_ref[...], lnb_ref[...])


def _meta_mix(sinks, h, w_in, conv_w, conv_b, wa, ba, wx, bx, lam, g_rec, q_tables, k_tables,
              w_out, g_attn, ln_g, ln_b, layer):
    vec = lambda width: pl.BlockSpec((None, 1, width), lambda i: (layer, 0, 0))
    gates = pl.BlockSpec((None, 2, GATE_HALF, GATE_HALF), lambda i: (layer, 0, 0, 0))
    table = pl.BlockSpec((N_META, LANES), lambda i: (0, 0))
    full = lambda r, w: pl.BlockSpec((r, w), lambda i: (0, 0))
    return pl.pallas_call(
        functools.partial(_meta_mix_kernel, layer),
        grid=(1,),
        in_specs=[
            pl.BlockSpec(memory_space=pltpu.SMEM),
            full(N_META, D_MODEL),
            pl.BlockSpec((None, D_MODEL, D_IN), lambda i: (layer, 0, 0)),
            pl.BlockSpec((None, CONV_WIDTH, REC_WIDTH), lambda i: (layer, 0, 0)),
            vec(REC_WIDTH), gates, vec(REC_WIDTH), gates, vec(REC_WIDTH), vec(REC_WIDTH),
            vec(REC_WIDTH), table, table, table, table, table, table,
            pl.BlockSpec((None, D_MIX, D_MODEL), lambda i: (layer, 0, 0)),
            vec(ATTN_WIDTH), vec(D_MODEL), vec(D_MODEL),
        ],
        out_specs=[full(N_META, D_MODEL), full(SUBLANES, REC_WIDTH), full(SUBLANES, REC_WIDTH),
                   full(N_META, 4 * LANES), full(N_META, 4 * LANES)],
        out_shape=[jax.ShapeDtypeStruct((N_META, D_MODEL), F32),
                   jax.ShapeDtypeStruct((SUBLANES, REC_WIDTH), F32),
                   jax.ShapeDtypeStruct((SUBLANES, REC_WIDTH), F32),
                   jax.ShapeDtypeStruct((N_META, 4 * LANES), BF16),
                   jax.ShapeDtypeStruct((N_META, 4 * LANES), BF16)],
        scratch_shapes=[pltpu.VMEM((N_META, REC_WIDTH), F32)],
        compiler_params=pltpu.CompilerParams(
            dimension_semantics=("arbitrary",), vmem_limit_bytes=VMEM_LIMIT_BYTES),
    )(sinks, h, w_in, conv_w, conv_b, wa, ba, wx, bx, lam, g_rec, *q_tables, *k_tables, w_out,
      g_attn, ln_g, ln_b)


def _block_diag_halves(w):
    per_half = REC_BLOCKS // 2
    out = jnp.zeros((w.shape[0], 2, GATE_HALF, GATE_HALF), w.dtype)
    for blk in range(REC_BLOCKS):
        half, pos = divmod(blk, per_half)
        sl = slice(pos * REC_BLOCK_DIM, (pos + 1) * REC_BLOCK_DIM)
        out = out.at[:, half, sl, sl].set(w[:, blk])
    return out.astype(BF16)


def _rope_tables(first_pos, count):
    pos = first_pos + jnp.arange(count, dtype=F32)
    inv_freq = ROPE_THETA ** (-jnp.arange(0, ROPE_DIM, 2, dtype=F32) / ROPE_DIM)
    ang = pos[:, None] * inv_freq[None, :]
    cos, sin = jnp.cos(ang), jnp.sin(ang)
    half = ROPE_DIM // 2
    pad = HEAD_DIM - ROPE_DIM
    ones = jnp.ones((count, pad), F32)
    zeros_h = jnp.zeros((count, half), F32)
    zeros_p = jnp.zeros((count, pad), F32)
    c_head = jnp.concatenate([cos, cos, ones], axis=1)
    s1_head = jnp.concatenate([-sin, zeros_h, zeros_p], axis=1)
    s2_head = jnp.concatenate([zeros_h, sin, zeros_p], axis=1)
    two = lambda t: jnp.concatenate([t, t], axis=1)
    return two(c_head), two(s1_head), two(s2_head)


def kernel(x, meta_tokens, ffn1_w_gate, ffn1_w_up, ffn1_w_down, ln1_g, ln1_b, w_in, conv_w, conv_b, gate_a_w, gate_a_b, gate_x_w, gate_x_b, lru_lambda, attn_sinks, norm_rec_g, norm_attn_g, w_out, ln2_g, ln2_b, ffn2_w_gate, ffn2_w_up, ffn2_w_down, ln3_g, ln3_b):
    batch, seq, _ = x.shape
    assert seq % MIX_A_ROWS == 0 and seq % MIX_B_ROWS == 0 and (batch * seq) % FFN_ROWS == 0
    assert MIX_A_ROWS // SUBLANES <= SEG_PITCH and SEG_PITCH % SUBLANES == 0

    bf = lambda w: w.astype(BF16)
    row = lambda p: p.reshape(p.shape[0], 1, p.shape[1])
    f1g, f1u, f1d = bf(ffn1_w_gate), bf(ffn1_w_up), bf(ffn1_w_down)
    f2g, f2u, f2d = bf(ffn2_w_gate), bf(ffn2_w_up), bf(ffn2_w_down)
    win, wout = bf(w_in), bf(w_out)
    wa, wx = _block_diag_halves(gate_a_w), _block_diag_halves(gate_x_w)
    l1g, l1b, l2g, l2b, l3g, l3b = map(row, (ln1_g, ln1_b, ln2_g, ln2_b, ln3_g, ln3_b))
    cb, ba, bx, lam, grec, gattn = map(
        row, (conv_b, gate_a_b, gate_x_b, lru_lambda, norm_rec_g, norm_attn_g))
    k_tab_m = _rope_tables(0.0, N_META)
    k_tab_x = _rope_tables(float(N_META), seq)
    q_tab_m = tuple(t * Q_SCALE for t in k_tab_m)
    q_tab_x = tuple(t * Q_SCALE for t in k_tab_x)

    h = x.reshape(batch * seq, D_MODEL)
    hm = meta_tokens.astype(x.dtype)
    for l in range(DEPTH):
        hm = _ffn(hm, f1g, f1u, f1d, l1g, l1b, l, N_META)
        h = _ffn(h, f1g, f1u, f1d, l1g, l1b, l, FFN_ROWS)
        hm, ctail, hinit, km4, vm4 = _meta_mix(
            attn_sinks, hm, win, conv_w, cb, wa, ba, wx, bx, lam, grec, q_tab_m, k_tab_m,
            wout, gattn, l2g, l2b, l)
        yrec, q, k4, v4 = _mix_a(h, win, conv_w, cb, wa, ba, wx, bx, lam, grec,
                                 q_tab_x, k_tab_x, ctail, hinit, l, batch, seq)
        h = _mix_b(attn_sinks, q, k4, v4, km4, vm4, yrec, h, wout, gattn, l2g, l2b,
                   l, batch, seq)
        if l + 1 < DEPTH:
            hm = _ffn(hm, f2g, f2u, f2d, l3g, l3b, l, N_META)
        h = _ffn(h, f2g, f2u, f2d, l3g, l3b, l, FFN_ROWS)
    return h.reshape(batch, seq, D_MODEL)
```

```python
import functools
import math

import jax
import jax.numpy as jnp
from jax import lax
from jax.experimental import pallas as pl
from jax.experimental.pallas import tpu as pltpu

F32 = jnp.float32
BF16 = jnp.bfloat16

D_MODEL = 1024
DEPTH = 4
N_META = 16
D_FF = 2816
REC_WIDTH = 512
REC_BLOCKS = 8
REC_BLOCK_DIM = REC_WIDTH // REC_BLOCKS
CONV_WIDTH = 4
LRU_C = 8.0
N_Q_HEADS = 8
N_KV_HEADS = 2
HEAD_DIM = 64
ATTN_WIDTH = N_Q_HEADS * HEAD_DIM
KV_WIDTH = N_KV_HEADS * HEAD_DIM
WINDOW = 128
ROPE_DIM = HEAD_DIM // 4
ROPE_THETA = 500000.0
D_IN = 2 * REC_WIDTH + ATTN_WIDTH + 2 * KV_WIDTH
D_MIX = REC_WIDTH + ATTN_WIDTH
DEEPNORM_ALPHA = (2.0 * DEPTH) ** 0.25
LN_EPS = 1e-5
RMS_EPS = 1e-6
NEG_INF = -1e30
Q_SCALE = HEAD_DIM ** -0.5
LOG2_E = 1.4426950408889634

LANES = 128
SUBLANES = 8
MXU_DIM = 256
VMEM_LIMIT_BYTES = 56 * 1024 * 1024

FFN_ROWS = 1024
FFN_SUB_ROWS = 512
FF_CHUNK = MXU_DIM
MIX_A_ROWS = 512
MIX_B_ROWS = 512
ATT_BLOCK = 64
OUT_ROWS = 256
OUT_DELAY = 4
SEG_PITCH = 72
QK_AHEAD = 2
GATE_HALF = REC_WIDTH // 2
SERIES_CUTOFF = -1.0 / 64.0


def _layer_norm(y, g, b):
    mu = jnp.mean(y, axis=-1, keepdims=True)
    d = y - mu
    var = jnp.mean(d * d, axis=-1, keepdims=True)
    return d * lax.rsqrt(var + LN_EPS) * g + b


def _rms_norm(y, g):
    ms = jnp.mean(y * y, axis=-1, keepdims=True)
    return y * lax.rsqrt(ms + RMS_EPS) * g


def _sigmoid(x):
    return 1.0 / (1.0 + jnp.exp2(x * (-LOG2_E)))


def _gelu_tanh(x):
    c = -2.0 * (2.0 / math.pi) ** 0.5 * LOG2_E
    return x / (1.0 + jnp.exp2(x * (c + (c * 0.044715) * (x * x))))


def _dot(a, b):
    return jnp.dot(a, b, preferred_element_type=F32)


def _dot_nt(a, b):
    return lax.dot_general(a, b, (((1,), (1,)), ((), ())), preferred_element_type=F32)


def _ffn_kernel(x_ref, wg_ref, wu_ref, wd_ref, g_ref, b_ref, o_ref, act_ref):
    sub = act_ref.shape[1]
    tiles = [slice(r * sub, (r + 1) * sub) for r in range(x_ref.shape[0] // sub)]
    for r, rs in enumerate(tiles):
        xb = x_ref[rs, :].astype(BF16)
        for c in range(D_FF // FF_CHUNK):
            sl = slice(c * FF_CHUNK, (c + 1) * FF_CHUNK)
            gate = _dot(xb, wg_ref[:, sl])
            up = _dot(xb, wu_ref[:, sl])
            act_ref[r, :, sl] = (gate * _sigmoid(gate) * up).astype(BF16)
    down = [_dot(act_ref[r], wd_ref[...]) for r in range(len(tiles))]
    for rs, d in zip(tiles, down):
        y = DEEPNORM_ALPHA * x_ref[rs, :] + 0.5 * d
        o_ref[rs, :] = _layer_norm(y, g_ref[...], b_ref[...])


def _ffn(x, wg, wu, wd, g, b, layer, rows):
    tokens = x.shape[0]
    resident = dict(pipeline_mode=pl.Buffered(1))
    sub = min(rows, FFN_SUB_ROWS)
    return pl.pallas_call(
        _ffn_kernel,
        grid=(tokens // rows,),
        scratch_shapes=[pltpu.VMEM((rows // sub, sub, D_FF), BF16)],
        in_specs=[
            pl.BlockSpec((rows, D_MODEL), lambda i: (i, 0)),
            pl.BlockSpec((None, D_MODEL, D_FF), lambda i: (layer, 0, 0), **resident),
            pl.BlockSpec((None, D_MODEL, D_FF), lambda i: (layer, 0, 0), **resident),
            pl.BlockSpec((None, D_FF, D_MODEL), lambda i: (layer, 0, 0), **resident),
            pl.BlockSpec((None, 1, D_MODEL), lambda i: (layer, 0, 0)),
            pl.BlockSpec((None, 1, D_MODEL), lambda i: (layer, 0, 0)),
        ],
        out_specs=pl.BlockSpec((rows, D_MODEL), lambda i: (i, 0)),
        out_shape=jax.ShapeDtypeStruct((tokens, D_MODEL), F32),
        compiler_params=pltpu.CompilerParams(
            dimension_semantics=("arbitrary",), vmem_limit_bytes=VMEM_LIMIT_BYTES),
    )(x, wg, wu, wd, g, b)


def _softplus(x):
    return jnp.maximum(x, 0.0) + jnp.log1p(jnp.exp(-jnp.abs(x)))


def _conv(xr, back, cw_ref, cb_ref):
    xc = cb_ref[...] + back(3) * cw_ref[0:1, :]
    xc = xc + back(2) * cw_ref[1:2, :]
    xc = xc + back(1) * cw_ref[2:3, :]
    return xc + xr * cw_ref[3:4, :]


def _gate_terms(xc, wa_ref, ba_ref, wx_ref, bx_ref, lam_ref):
    xcb = xc.astype(BF16)
    lo, hi = xcb[:, :GATE_HALF], xcb[:, GATE_HALF:]
    r_lin = jnp.concatenate([_dot(lo, wa_ref[0]), _dot(hi, wa_ref[1])], axis=1) + ba_ref[...]
    i_lin = jnp.concatenate([_dot(lo, wx_ref[0]), _dot(hi, wx_ref[1])], axis=1) + bx_ref[...]
    r = _sigmoid(r_lin)
    i = _sigmoid(i_lin)
    log_a = (-LRU_C * _softplus(-lam_ref[...])) * r
    a = jnp.exp(log_a)
    y = 2.0 * log_a
    series = -y * (1.0 + y * (0.5 + y * (1.0 / 6.0)))
    one_minus_a2 = jnp.where(y > SERIES_CUTOFF, series, 1.0 - a * a)
    b = jnp.sqrt(one_minus_a2) * (i * xc)
    return a, b


def _delayed(xr, prev8, d):
    rolled = pltpu.roll(xr, d, 0)
    sub = lax.broadcasted_iota(jnp.int32, prev8.shape, 0)
    head = jnp.where(sub < d, pltpu.roll(prev8, d, 0), rolled[:SUBLANES])
    return jnp.concatenate([head, rolled[SUBLANES:]], axis=0)


def _scan8(a, b):
    rows, width = a.shape
    groups = rows // SUBLANES
    a3 = a.reshape(groups, SUBLANES, width)
    b3 = b.reshape(groups, SUBLANES, width)
    sub = lax.broadcasted_iota(jnp.int32, (groups, SUBLANES, width), 1)
    for s in (1, 2, 4):
        keep = sub >= s
        b_prev = jnp.where(keep, pltpu.roll(b3, s, 1), 0.0)
        a_prev = jnp.where(keep, pltpu.roll(a3, s, 1), 1.0)
        b3 = b3 + a3 * b_prev
        a3 = a3 * a_prev
    return a3.reshape(rows, width), b3.reshape(rows, width)


def _row_bcast(ref, r):
    return jnp.broadcast_to(ref[pl.ds(r, 1), :], (SUBLANES, ref.shape[1]))


def _rope(t, cos, s1, s2):
    cols = []
    for j in range(t.shape[1] // LANES):
        tj = t[:, j * LANES:(j + 1) * LANES]
        up = pltpu.roll(tj, LANES - ROPE_DIM // 2, 1)
        down = pltpu.roll(tj, ROPE_DIM // 2, 1)
        cols.append(tj * cos + up * s1 + down * s2)
    return cols[0] if len(cols) == 1 else jnp.concatenate(cols, axis=1)


def _split_kv(t):
    swapped = pltpu.roll(t, HEAD_DIM, 1)
    low = lax.broadcasted_iota(jnp.int32, t.shape, 1) < HEAD_DIM
    zero = jnp.zeros_like(t)
    parts = [jnp.where(low, t, zero), jnp.where(low, zero, swapped),
             jnp.where(low, swapped, zero), jnp.where(low, zero, t)]
    return jnp.concatenate(parts, axis=1).astype(BF16)


def _gelu_gate_norm(hseq, gate, grec_ref):
    y = hseq * _gelu_tanh(gate)
    return _rms_norm(y, grec_ref[...]).astype(BF16)


def _pair_probs(s_band, band_ok, s_meta, meta_ok_even, meta_ok_odd, sink_even, sink_odd):
    rows = s_meta.shape[0]
    sme = jnp.where(meta_ok_even, s_meta, NEG_INF)
    smo = jnp.where(meta_ok_odd, s_meta, NEG_INF)
    m_e = jnp.maximum(jnp.max(sme, axis=-1, keepdims=True), sink_even)
    m_o = jnp.maximum(jnp.max(smo, axis=-1, keepdims=True), sink_odd)
    if s_band is not None:
        half = s_band.shape[1] // 2
        se = jnp.where(band_ok, s_band[:, :half], NEG_INF)
        so = jnp.where(band_ok, s_band[:, half:], NEG_INF)
        m_e = jnp.maximum(m_e, jnp.max(se, axis=-1, keepdims=True))
        m_o = jnp.maximum(m_o, jnp.max(so, axis=-1, keepdims=True))
    pme = jnp.exp(sme - m_e)
    pmo = jnp.exp(smo - m_o)
    den_e = jnp.sum(pme, axis=-1, keepdims=True) + jnp.exp(sink_even - m_e)
    den_o = jnp.sum(pmo, axis=-1, keepdims=True) + jnp.exp(sink_odd - m_o)
    p_band = None
    if s_band is not None:
        pe = jnp.exp(se - m_e)
        po = jnp.exp(so - m_o)
        den_e = den_e + jnp.sum(pe, axis=-1, keepdims=True)
        den_o = den_o + jnp.sum(po, axis=-1, keepdims=True)
        p_band = jnp.concatenate([pe, po], axis=1).astype(BF16)
    low = lax.broadcasted_iota(jnp.int32, (rows, LANES), 1) < HEAD_DIM
    inv = jnp.where(low, 1.0 / den_e, 1.0 / den_o)
    return p_band, (pme + pmo).astype(BF16), inv


def _pair_out(p_band, p_meta, inv, v_band, v_meta):
    out = _dot(p_meta, v_meta)
    if p_band is not None:
        out = out + _dot(p_band, v_band)
    return out * inv


def _col(ref_or_val, j, rows=slice(None)):
    return ref_or_val[rows, j * LANES:(j + 1) * LANES]


def _mix_a_kernel(h_ref, win_ref, cw_ref, cb_ref, wa_ref, ba_ref, wx_ref, bx_ref, lam_ref,
                  grec_ref, cosq_ref, s1q_ref, s2q_ref, cos_ref, s1_ref, s2_ref, ctail_ref,
                  hinit_ref,
                  yrec_ref, q_ref, k4_ref, v4_ref,
                  hist, state, a_buf, b_buf, h_buf):
    rows = h_ref.shape[0]
    ncol = REC_WIDTH // LANES

    @pl.when(pl.program_id(1) == 0)
    def _():
        hist[...] = ctail_ref[...]
        state[...] = hinit_ref[...]

    proj = _dot(h_ref[...].astype(BF16), win_ref[...])
    xr = proj[:, :REC_WIDTH]
    gate = proj[:, REC_WIDTH:2 * REC_WIDTH]
    q = proj[:, 2 * REC_WIDTH:2 * REC_WIDTH + ATTN_WIDTH]
    k = proj[:, 2 * REC_WIDTH + ATTN_WIDTH:2 * REC_WIDTH + ATTN_WIDTH + KV_WIDTH]
    v = proj[:, 2 * REC_WIDTH + ATTN_WIDTH + KV_WIDTH:]

    prev8 = hist[...]
    xc = _conv(xr, lambda d: _delayed(xr, prev8, d), cw_ref, cb_ref)
    hist[...] = xr[rows - SUBLANES:, :]
    a, b = _gate_terms(xc, wa_ref, ba_ref, wx_ref, bx_ref, lam_ref)

    seg = rows // SUBLANES
    for c in range(ncol):
        for s in range(SUBLANES):
            src_rows, lanes = slice(s * seg, (s + 1) * seg), slice(c * LANES, (c + 1) * LANES)
            a_buf[c, pl.ds(s * SEG_PITCH, seg), :] = a[src_rows, lanes]
            b_buf[c, pl.ds(s * SEG_PITCH, seg), :] = b[src_rows, lanes]
    sub = lax.broadcasted_iota(jnp.int32, (SUBLANES, LANES), 0)
    col_seqs = []
    for c in range(ncol):
        lanes = slice(c * LANES, (c + 1) * LANES)
        step = lambda buf, j: buf[c, pl.ds(j, SUBLANES, stride=SEG_PITCH), :]
        prod = step(a_buf, 0)
        loc = step(b_buf, 0)
        prods, locs = [prod], [loc]
        for j in range(1, seg):
            aj = step(a_buf, j)
            loc = aj * loc + step(b_buf, j)
            prod = aj * prod
            prods.append(prod)
            locs.append(loc)
        h0 = state[:, lanes]
        pa, pb = _scan8(prod, loc)
        after = pb + pa * h0
        enter = jnp.where(sub == 0, h0, pltpu.roll(after, 1, 0))
        for j in range(seg):
            h_buf[c, pl.ds(j, SUBLANES, stride=SEG_PITCH), :] = locs[j] + prods[j] * enter
        h_buf[c, pl.ds(seg, SUBLANES), :] = after
        state[:, lanes] = _row_bcast(h_buf.at[c], seg + SUBLANES - 1)
        col_seqs.append(jnp.concatenate(
            [h_buf[c, pl.ds(s * SEG_PITCH, seg), :] for s in range(SUBLANES)], axis=0))
    hseq = jnp.concatenate(col_seqs, axis=1)

    yrec_ref[...] = _gelu_gate_norm(hseq, gate, grec_ref)
    q_ref[...] = _rope(q, cosq_ref[...], s1q_ref[...], s2q_ref[...]).astype(BF16)
    k4_ref[...] = _split_kv(_rope(k, cos_ref[...], s1_ref[...], s2_ref[...]))
    v4_ref[...] = _split_kv(v)


def _mix_a(h, w_in, conv_w, conv_b, wa, ba, wx, bx, lam, g_rec, q_tables, k_tables, ctail,
           hinit, layer, batch, seq):
    rows = MIX_A_ROWS
    nc = seq // rows
    tokens = batch * seq
    tok = lambda width: pl.BlockSpec((rows, width), lambda bi, ci: (bi * nc + ci, 0))
    vec = lambda width: pl.BlockSpec((None, 1, width), lambda bi, ci: (layer, 0, 0))
    gates = pl.BlockSpec((None, 2, GATE_HALF, GATE_HALF), lambda bi, ci: (layer, 0, 0, 0))
    table = pl.BlockSpec((rows, LANES), lambda bi, ci: (ci, 0))
    seed = pl.BlockSpec((SUBLANES, REC_WIDTH), lambda bi, ci: (0, 0))
    ncol = REC_WIDTH // LANES
    return pl.pallas_call(
        _mix_a_kernel,
        grid=(batch, nc),
        in_specs=[
            tok(D_MODEL),
            pl.BlockSpec((None, D_MODEL, D_IN), lambda bi, ci: (layer, 0, 0),
                         pipeline_mode=pl.Buffered(1)),
            pl.BlockSpec((None, CONV_WIDTH, REC_WIDTH), lambda bi, ci: (layer, 0, 0)),
            vec(REC_WIDTH), gates, vec(REC_WIDTH), gates, vec(REC_WIDTH), vec(REC_WIDTH),
            vec(REC_WIDTH), table, table, table, table, table, table, seed, seed,
        ],
        out_specs=[tok(REC_WIDTH), tok(ATTN_WIDTH), tok(4 * LANES), tok(4 * LANES)],
        out_shape=[jax.ShapeDtypeStruct((tokens, REC_WIDTH), BF16),
                   jax.ShapeDtypeStruct((tokens, ATTN_WIDTH), BF16),
                   jax.ShapeDtypeStruct((tokens, 4 * LANES), BF16),
                   jax.ShapeDtypeStruct((tokens, 4 * LANES), BF16)],
        scratch_shapes=[
            pltpu.VMEM((SUBLANES, REC_WIDTH), F32),
            pltpu.VMEM((SUBLANES, REC_WIDTH), F32),
            pltpu.VMEM((ncol, SUBLANES * SEG_PITCH, LANES), F32),
            pltpu.VMEM((ncol, SUBLANES * SEG_PITCH, LANES), F32),
            pltpu.VMEM((ncol, SUBLANES * SEG_PITCH, LANES), F32),
        ],
        compiler_params=pltpu.CompilerParams(
            dimension_semantics=("arbitrary", "arbitrary"), vmem_limit_bytes=VMEM_LIMIT_BYTES),
    )(h, w_in, conv_w, conv_b, wa, ba, wx, bx, lam, g_rec, *q_tables, *k_tables, ctail, hinit)


def _window_probs(s, ok, sink_even, sink_odd):
    rows, half = s.shape[0], s.shape[1] // 2
    se = jnp.where(ok, s[:, :half], NEG_INF)
    so = jnp.where(ok, s[:, half:], NEG_INF)
    m_e = jnp.maximum(jnp.max(se, axis=-1, keepdims=True), sink_even)
    m_o = jnp.maximum(jnp.max(so, axis=-1, keepdims=True), sink_odd)
    pe = jnp.exp(se - m_e)
    po = jnp.exp(so - m_o)
    den_e = jnp.sum(pe, axis=-1, keepdims=True) + jnp.exp(sink_even - m_e)
    den_o = jnp.sum(po, axis=-1, keepdims=True) + jnp.exp(sink_odd - m_o)
    low = lax.broadcasted_iota(jnp.int32, (rows, LANES), 1) < HEAD_DIM
    inv = jnp.where(low, 1.0 / den_e, 1.0 / den_o)
    return jnp.concatenate([pe, po], axis=1).astype(BF16), inv


def _mix_b_kernel(layer, sink_ref, q_ref, k4_ref, v4_ref, kp_ref, vp_ref, km_ref, vm_ref,
                  yrec_ref, h_ref, wout_ref, gattn_ref, lng_ref, lnb_ref, o_ref):
    rows = q_ref.shape[0]
    first = pl.program_id(1) == 0
    span = WINDOW + ATT_BLOCK
    pad = 2 * WINDOW - span - N_META
    qi = lax.broadcasted_iota(jnp.int32, (ATT_BLOCK, 2 * WINDOW), 0)
    kj = lax.broadcasted_iota(jnp.int32, (ATT_BLOCK, 2 * WINDOW), 1)
    is_meta = kj < N_META
    lo = qi + (N_META + pad + 1)
    hi = qi + (N_META + pad + WINDOW)

    def visible(missing):
        floor = jnp.where(first, N_META + pad + missing, 0)
        return is_meta | ((kj >= jnp.maximum(lo, floor)) & (kj <= hi))

    zeros = jnp.zeros((pad, LANES), BF16)

    def window(ref, prev_ref, meta_ref, jb, g):
        end = (jb + 1) * ATT_BLOCK
        parts = []
        for cc in (2 * g, 2 * g + 1):
            parts += [_col(meta_ref, cc), zeros]
            if end < span:
                parts += [_col(prev_ref, cc, slice(end - span + WINDOW, WINDOW)),
                          _col(ref, cc, slice(0, end))]
            else:
                parts.append(_col(ref, cc, slice(end - span, end)))
        return jnp.concatenate(parts, axis=0)

    items = [(jb, g) for jb in range(rows // ATT_BLOCK) for g in range(N_KV_HEADS)]
    per_group = (OUT_ROWS // ATT_BLOCK) * N_KV_HEADS

    def scores(k):
        jb, g = items[k]
        cur = slice(jb * ATT_BLOCK, (jb + 1) * ATT_BLOCK)
        lhs = jnp.concatenate([_col(q_ref, 2 * g, cur), _col(q_ref, 2 * g + 1, cur)], axis=0)
        return _dot_nt(lhs, window(k4_ref, kp_ref, km_ref, jb, g))

    outs = {}

    def finish(m):
        rs = slice(m * OUT_ROWS, (m + 1) * OUT_ROWS)
        blocks = []
        for jb in range(m * OUT_ROWS // ATT_BLOCK, (m + 1) * OUT_ROWS // ATT_BLOCK):
            y = jnp.concatenate([outs[jb, cc] for cc in range(2 * N_KV_HEADS)], axis=1)
            blocks.append(_rms_norm(y, gattn_ref[...]).astype(BF16))
        y_attn = jnp.concatenate(blocks, axis=0)
        mixed = _dot(jnp.concatenate([yrec_ref[rs, :], y_attn], axis=1), wout_ref[...])
        o_ref[rs, :] = _layer_norm(DEEPNORM_ALPHA * h_ref[rs, :] + mixed,
                                   lng_ref[...], lnb_ref[...])

    ready = [scores(k) for k in range(min(QK_AHEAD, len(items)))]
    done = 0
    for k, (jb, g) in enumerate(items):
        if k + QK_AHEAD < len(items):
            ready.append(scores(k + QK_AHEAD))
        s = ready.pop(0)
        ok = visible(max(span - (jb + 1) * ATT_BLOCK, 0))
        p_cols, inv_cols = [], []
        for c in range(2):
            head = 4 * g + 2 * c
            p, inv = _window_probs(s[c * ATT_BLOCK:(c + 1) * ATT_BLOCK], ok,
                                   sink_ref[layer, head], sink_ref[layer, head + 1])
            p_cols.append(p)
            inv_cols.append(inv)
        out = _dot(jnp.concatenate(p_cols, axis=0), window(v4_ref, vp_ref, vm_ref, jb, g))
        out = out * jnp.concatenate(inv_cols, axis=0)
        outs[jb, 2 * g], outs[jb, 2 * g + 1] = out[:ATT_BLOCK], out[ATT_BLOCK:]
        if k + 1 >= (done + 1) * per_group + OUT_DELAY:
            finish(done)
            done += 1
    for m in range(done, rows // OUT_ROWS):
        finish(m)


def _mix_b(sinks, q, k4, v4, km4, vm4, yrec, h, w_out, g_attn, ln_g, ln_b, layer, batch, seq):
    rows = MIX_B_ROWS
    nc = seq // rows
    blocks_per_step = rows // WINDOW
    blocks_per_seq = seq // WINDOW
    tokens = batch * seq
    tok = lambda width: pl.BlockSpec((rows, width), lambda bi, ci: (bi * nc + ci, 0))
    prev = pl.BlockSpec(
        (WINDOW, 4 * LANES),
        lambda bi, ci: (jnp.maximum(bi * blocks_per_seq + ci * blocks_per_step - 1, 0), 0))
    meta = pl.BlockSpec((N_META, 4 * LANES), lambda bi, ci: (0, 0))
    vec = lambda width: pl.BlockSpec((None, 1, width), lambda bi, ci: (layer, 0, 0))
    return pl.pallas_call(
        functools.partial(_mix_b_kernel, layer),
        grid=(batch, nc),
        in_specs=[
            pl.BlockSpec(memory_space=pltpu.SMEM),
            tok(ATTN_WIDTH), tok(4 * LANES), tok(4 * LANES), prev, prev, meta, meta,
            tok(REC_WIDTH), tok(D_MODEL),
            pl.BlockSpec((None, D_MIX, D_MODEL), lambda bi, ci: (layer, 0, 0),
                         pipeline_mode=pl.Buffered(1)),
            vec(ATTN_WIDTH), vec(D_MODEL), vec(D_MODEL),
        ],
        out_specs=tok(D_MODEL),
        out_shape=jax.ShapeDtypeStruct((tokens, D_MODEL), F32),
        compiler_params=pltpu.CompilerParams(
            dimension_semantics=("arbitrary", "arbitrary"), vmem_limit_bytes=VMEM_LIMIT_BYTES),
    )(sinks, q, k4, v4, k4, v4, km4, vm4, yrec, h, w_out, g_attn, ln_g, ln_b)


def _meta_mix_kernel(layer, sink_ref, h_ref, win_ref, cw_ref, cb_ref, wa_ref, ba_ref, wx_ref,
                     bx_ref, lam_ref, grec_ref, cosq_ref, s1q_ref, s2q_ref, cos_ref, s1_ref,
                     s2_ref, wout_ref, gattn_ref, lng_ref, lnb_ref,
                     o_ref, ctail_ref, hstate_ref, k4_ref, v4_ref, hbuf):
    rows = N_META
    h_in = h_ref[...]
    proj = _dot(h_in.astype(BF16), win_ref[...])
    xr = proj[:, :REC_WIDTH]
    gate = proj[:, REC_WIDTH:2 * REC_WIDTH]
    q = proj[:, 2 * REC_WIDTH:2 * REC_WIDTH + ATTN_WIDTH]
    k = proj[:, 2 * REC_WIDTH + ATTN_WIDTH:2 * REC_WIDTH + ATTN_WIDTH + KV_WIDTH]
    v = proj[:, 2 * REC_WIDTH + ATTN_WIDTH + KV_WIDTH:]

    back = lambda d: _delayed(xr, jnp.zeros((SUBLANES, REC_WIDTH), F32), d)
    a, b = _gate_terms(_conv(xr, back, cw_ref, cb_ref), wa_ref, ba_ref, wx_ref, bx_ref, lam_ref)
    ctail_ref[...] = xr[rows - SUBLANES:, :]

    a1, b1 = _scan8(a, b)
    hbuf[0:SUBLANES, :] = b1[:SUBLANES]
    hbuf[SUBLANES:, :] = b1[SUBLANES:] + a1[SUBLANES:] * _row_bcast(hbuf, SUBLANES - 1)
    hseq = hbuf[...]
    hstate_ref[...] = _row_bcast(hbuf, rows - 1)
    y_rec = _gelu_gate_norm(hseq, gate, grec_ref)

    qb = _rope(q, cosq_ref[...], s1q_ref[...], s2q_ref[...]).astype(BF16)
    k4 = _split_kv(_rope(k, cos_ref[...], s1_ref[...], s2_ref[...]))
    v4 = _split_kv(v)
    k4_ref[...] = k4
    v4_ref[...] = v4

    qi = lax.broadcasted_iota(jnp.int32, (rows, 2 * N_META), 0)
    lane32 = lax.broadcasted_iota(jnp.int32, (rows, 2 * N_META), 1)
    meta_even = (lane32 < N_META) & (lane32 <= qi)
    meta_odd = (lane32 >= N_META) & (lane32 - N_META <= qi)
    cols = []
    for g in range(N_KV_HEADS):
        lo_c, hi_c = 2 * g, 2 * g + 1
        k_meta = jnp.concatenate([_col(k4, lo_c), _col(k4, hi_c)], axis=0)
        v_meta = jnp.concatenate([_col(v4, lo_c), _col(v4, hi_c)], axis=0)
        lhs = jnp.concatenate([_col(qb, lo_c), _col(qb, hi_c)], axis=0)
        s_meta = _dot_nt(lhs, k_meta)
        for c in range(2):
            head = 4 * g + 2 * c
            pieces = _pair_probs(None, None, s_meta[c * rows:(c + 1) * rows], meta_even, meta_odd,
                                 sink_ref[layer, head], sink_ref[layer, head + 1])
            cols.append(_pair_out(*pieces, None, v_meta))
    y_attn = _rms_norm(jnp.concatenate(cols, axis=1), gattn_ref[...]).astype(BF16)
    mixed = _dot(jnp.concatenate([y_rec, y_attn], axis=1), wout_ref[...])
    o_ref[...] = _layer_norm(DEEPNORM_ALPHA * h_in + mixed, lng_ref[...], lnb_ref[...])


def _meta_mix(sinks, h, w_in, conv_w, conv_b, wa, ba, wx, bx, lam, g_rec, q_tables, k_tables,
              w_out, g_attn, ln_g, ln_b, layer):
    vec = lambda width: pl.BlockSpec((None, 1, width), lambda i: (layer, 0, 0))
    gates = pl.BlockSpec((None, 2, GATE_HALF, GATE_HALF), lambda i: (layer, 0, 0, 0))
    table = pl.BlockSpec((N_META, LANES), lambda i: (0, 0))
    full = lambda r, w: pl.BlockSpec((r, w), lambda i: (0, 0))
    return pl.pallas_call(
        functools.partial(_meta_mix_kernel, layer),
        grid=(1,),
        in_specs=[
            pl.BlockSpec(memory_space=pltpu.SMEM),
            full(N_META, D_MODEL),
            pl.BlockSpec((None, D_MODEL, D_IN), lambda i: (layer, 0, 0)),
            pl.BlockSpec((None, CONV_WIDTH, REC_WIDTH), lambda i: (layer, 0, 0)),
            vec(REC_WIDTH), gates, vec(REC_WIDTH), gates, vec(REC_WIDTH), vec(REC_WIDTH),
            vec(REC_WIDTH), table, table, table, table, table, table,
            pl.BlockSpec((None, D_MIX, D_MODEL), lambda i: (layer, 0, 0)),
            vec(ATTN_WIDTH), vec(D_MODEL), vec(D_MODEL),
        ],
        out_specs=[full(N_META, D_MODEL), full(SUBLANES, REC_WIDTH), full(SUBLANES, REC_WIDTH),
                   full(N_META, 4 * LANES), full(N_META, 4 * LANES)],
        out_shape=[jax.ShapeDtypeStruct((N_META, D_MODEL), F32),
                   jax.ShapeDtypeStruct((SUBLANES, REC_WIDTH), F32),
                   jax.ShapeDtypeStruct((SUBLANES, REC_WIDTH), F32),
                   jax.ShapeDtypeStruct((N_META, 4 * LANES), BF16),
                   jax.ShapeDtypeStruct((N_META, 4 * LANES), BF16)],
        scratch_shapes=[pltpu.VMEM((N_META, REC_WIDTH), F32)],
        compiler_params=pltpu.CompilerParams(
            dimension_semantics=("arbitrary",), vmem_limit_bytes=VMEM_LIMIT_BYTES),
    )(sinks, h, w_in, conv_w, conv_b, wa, ba, wx, bx, lam, g_rec, *q_tables, *k_tables, w_out,
      g_attn, ln_g, ln_b)


def _block_diag_halves(w):
    per_half = REC_BLOCKS // 2
    out = jnp.zeros((w.shape[0], 2, GATE_HALF, GATE_HALF), w.dtype)
    for blk in range(REC_BLOCKS):
        half, pos = divmod(blk, per_half)
        sl = slice(pos * REC_BLOCK_DIM, (pos + 1) * REC_BLOCK_DIM)
        out = out.at[:, half, sl, sl].set(w[:, blk])
    return out.astype(BF16)


def _rope_tables(first_pos, count):
    pos = first_pos + jnp.arange(count, dtype=F32)
    inv_freq = ROPE_THETA ** (-jnp.arange(0, ROPE_DIM, 2, dtype=F32) / ROPE_DIM)
    ang = pos[:, None] * inv_freq[None, :]
    cos, sin = jnp.cos(ang), jnp.sin(ang)
    half = ROPE_DIM // 2
    pad = HEAD_DIM - ROPE_DIM
    ones = jnp.ones((count, pad), F32)
    zeros_h = jnp.zeros((count, half), F32)
    zeros_p = jnp.zeros((count, pad), F32)
    c_head = jnp.concatenate([cos, cos, ones], axis=1)
    s1_head = jnp.concatenate([-sin, zeros_h, zeros_p], axis=1)
    s2_head = jnp.concatenate([zeros_h, sin, zeros_p], axis=1)
    two = lambda t: jnp.concatenate([t, t], axis=1)
    return two(c_head), two(s1_head), two(s2_head)


def kernel(x, meta_tokens, ffn1_w_gate, ffn1_w_up, ffn1_w_down, ln1_g, ln1_b, w_in, conv_w, conv_b, gate_a_w, gate_a_b, gate_x_w, gate_x_b, lru_lambda, attn_sinks, norm_rec_g, norm_attn_g, w_out, ln2_g, ln2_b, ffn2_w_gate, ffn2_w_up, ffn2_w_down, ln3_g, ln3_b):
    batch, seq, _ = x.shape
    assert seq % MIX_A_ROWS == 0 and seq % MIX_B_ROWS == 0 and (batch * seq) % FFN_ROWS == 0
    assert MIX_A_ROWS // SUBLANES <= SEG_PITCH and SEG_PITCH % SUBLANES == 0

    bf = lambda w: w.astype(BF16)
    row = lambda p: p.reshape(p.shape[0], 1, p.shape[1])
    f1g, f1u, f1d = bf(ffn1_w_gate), bf(ffn1_w_up), bf(ffn1_w_down)
    f2g, f2u, f2d = bf(ffn2_w_gate), bf(ffn2_w_up), bf(ffn2_w_down)
    win, wout = bf(w_in), bf(w_out)
    wa, wx = _block_diag_halves(gate_a_w), _block_diag_halves(gate_x_w)
    l1g, l1b, l2g, l2b, l3g, l3b = map(row, (ln1_g, ln1_b, ln2_g, ln2_b, ln3_g, ln3_b))
    cb, ba, bx, lam, grec, gattn = map(
        row, (conv_b, gate_a_b, gate_x_b, lru_lambda, norm_rec_g, norm_attn_g))
    k_tab_m = _rope_tables(0.0, N_META)
    k_tab_x = _rope_tables(float(N_META), seq)
    q_tab_m = tuple(t * Q_SCALE for t in k_tab_m)
    q_tab_x = tuple(t * Q_SCALE for t in k_tab_x)

    h = x.reshape(batch * seq, D_MODEL)
    hm = meta_tokens.astype(x.dtype)
    for l in range(DEPTH):
        hm = _ffn(hm, f1g, f1u, f1d, l1g, l1b, l, N_META)
        h = _ffn(h, f1g, f1u, f1d, l1g, l1b, l, FFN_ROWS)
        hm, ctail, hinit, km4, vm4 = _meta_mix(
            attn_sinks, hm, win, conv_w, cb, wa, ba, wx, bx, lam, grec, q_tab_m, k_tab_m,
            wout, gattn, l2g, l2b, l)
        yrec, q, k4, v4 = _mix_a(h, win, conv_w, cb, wa, ba, wx, bx, lam, grec,
                                 q_tab_x, k_tab_x, ctail, hinit, l, batch, seq)
        h = _mix_b(attn_sinks, q, k4, v4, km4, vm4, yrec, h, wout, gattn, l2g, l2b,
                   l, batch, seq)
        if l + 1 < DEPTH:
            hm = _ffn(hm, f2g, f2u, f2d, l3g, l3b, l, N_META)
        h = _ffn(h, f2g, f2u, f2d, l3g, l3b, l, FFN_ROWS)
    return h.reshape(batch, seq, D_MODEL)
```

```python
import functools
import math

import jax
import jax.numpy as jnp
from jax import lax
from jax.experimental import pallas as pl
from jax.experimental.pallas import tpu as pltpu

F32 = jnp.float32
BF16 = jnp.bfloat16

D_MODEL = 1024
DEPTH = 4
N_META = 16
D_FF = 2816
REC_WIDTH = 512
REC_BLOCKS = 8
REC_BLOCK_DIM = REC_WIDTH // REC_BLOCKS
CONV_WIDTH = 4
LRU_C = 8.0
N_Q_HEADS = 8
N_KV_HEADS = 2
HEAD_DIM = 64
ATTN_WIDTH = N_Q_HEADS * HEAD_DIM
KV_WIDTH = N_KV_HEADS * HEAD_DIM
WINDOW = 128
ROPE_DIM = HEAD_DIM // 4
ROPE_THETA = 500000.0
D_IN = 2 * REC_WIDTH + ATTN_WIDTH + 2 * KV_WIDTH
D_MIX = REC_WIDTH + ATTN_WIDTH
DEEPNORM_ALPHA = (2.0 * DEPTH) ** 0.25
LN_EPS = 1e-5
RMS_EPS = 1e-6
NEG_INF = -1e30
Q_SCALE = HEAD_DIM ** -0.5
LOG2_E = 1.4426950408889634

LANES = 128
SUBLANES = 8
MXU_DIM = 256
VMEM_LIMIT_BYTES = 56 * 1024 * 1024

FFN_ROWS = 1024
FFN_SUB_ROWS = 256
FF_CHUNK = MXU_DIM
MIX_A_ROWS = 512
MIX_B_ROWS = 512
ATT_BLOCK = 64
OUT_ROWS = 256
OUT_DELAY = 4
SEG_PITCH = 72
QK_AHEAD = 2
GATE_HALF = REC_WIDTH // 2
SERIES_CUTOFF = -1.0 / 64.0


def _layer_norm(y, g, b):
    mu = jnp.mean(y, axis=-1, keepdims=True)
    d = y - mu
    var = jnp.mean(d * d, axis=-1, keepdims=True)
    return d * lax.rsqrt(var + LN_EPS) * g + b


def _rms_norm(y, g):
    ms = jnp.mean(y * y, axis=-1, keepdims=True)
    return y * lax.rsqrt(ms + RMS_EPS) * g


def _sigmoid(x):
    return 1.0 / (1.0 + jnp.exp2(x * (-LOG2_E)))


def _gelu_tanh(x):
    c = -2.0 * (2.0 / math.pi) ** 0.5 * LOG2_E
    return x / (1.0 + jnp.exp2(x * (c + (c * 0.044715) * (x * x))))


def _dot(a, b):
    return jnp.dot(a, b, preferred_element_type=F32)


def _dot_nt(a, b):
    return lax.dot_general(a, b, (((1,), (1,)), ((), ())), preferred_element_type=F32)


def _ffn_kernel(x_ref, wg_ref, wu_ref, wd_ref, g_ref, b_ref, o_ref, act_ref):
    sub = act_ref.shape[1]
    tiles = [slice(r * sub, (r + 1) * sub) for r in range(x_ref.shape[0] // sub)]
    for r, rs in enumerate(tiles):
        xb = x_ref[rs, :].astype(BF16)
        for c in range(D_FF // FF_CHUNK):
            sl = slice(c * FF_CHUNK, (c + 1) * FF_CHUNK)
            gate = _dot(xb, wg_ref[:, sl])
            up = _dot(xb, wu_ref[:, sl])
            act_ref[r, :, sl] = (gate * _sigmoid(gate) * up).astype(BF16)
    down = [_dot(act_ref[r], wd_ref[...]) for r in range(len(tiles))]
    for rs, d in zip(tiles, down):
        y = DEEPNORM_ALPHA * x_ref[rs, :] + 0.5 * d
        o_ref[rs, :] = _layer_norm(y, g_ref[...], b_ref[...])


def _ffn(x, wg, wu, wd, g, b, layer, rows):
    tokens = x.shape[0]
    resident = dict(pipeline_mode=pl.Buffered(1))
    sub = min(rows, FFN_SUB_ROWS)
    return pl.pallas_call(
        _ffn_kernel,
        grid=(tokens // rows,),
        scratch_shapes=[pltpu.VMEM((rows // sub, sub, D_FF), BF16)],
        in_specs=[
            pl.BlockSpec((rows, D_MODEL), lambda i: (i, 0)),
            pl.BlockSpec((None, D_MODEL, D_FF), lambda i: (layer, 0, 0), **resident),
            pl.BlockSpec((None, D_MODEL, D_FF), lambda i: (layer, 0, 0), **resident),
            pl.BlockSpec((None, D_FF, D_MODEL), lambda i: (layer, 0, 0), **resident),
            pl.BlockSpec((None, 1, D_MODEL), lambda i: (layer, 0, 0)),
            pl.BlockSpec((None, 1, D_MODEL), lambda i: (layer, 0, 0)),
        ],
        out_specs=pl.BlockSpec((rows, D_MODEL), lambda i: (i, 0)),
        out_shape=jax.ShapeDtypeStruct((tokens, D_MODEL), F32),
        compiler_params=pltpu.CompilerParams(
            dimension_semantics=("arbitrary",), vmem_limit_bytes=VMEM_LIMIT_BYTES),
    )(x, wg, wu, wd, g, b)


def _softplus(x):
    return jnp.maximum(x, 0.0) + jnp.log1p(jnp.exp(-jnp.abs(x)))


def _conv(xr, back, cw_ref, cb_ref):
    xc = cb_ref[...] + back(3) * cw_ref[0:1, :]
    xc = xc + back(2) * cw_ref[1:2, :]
    xc = xc + back(1) * cw_ref[2:3, :]
    return xc + xr * cw_ref[3:4, :]


def _gate_terms(xc, wa_ref, ba_ref, wx_ref, bx_ref, lam_ref):
    xcb = xc.astype(BF16)
    lo, hi = xcb[:, :GATE_HALF], xcb[:, GATE_HALF:]
    r_lin = jnp.concatenate([_dot(lo, wa_ref[0]), _dot(hi, wa_ref[1])], axis=1) + ba_ref[...]
    i_lin = jnp.concatenate([_dot(lo, wx_ref[0]), _dot(hi, wx_ref[1])], axis=1) + bx_ref[...]
    r = _sigmoid(r_lin)
    i = _sigmoid(i_lin)
    log_a = (-LRU_C * _softplus(-lam_ref[...])) * r
    a = jnp.exp(log_a)
    y = 2.0 * log_a
    series = -y * (1.0 + y * (0.5 + y * (1.0 / 6.0)))
    one_minus_a2 = jnp.where(y > SERIES_CUTOFF, series, 1.0 - a * a)
    b = jnp.sqrt(one_minus_a2) * (i * xc)
    return a, b


def _delayed(xr, prev8, d):
    rolled = pltpu.roll(xr, d, 0)
    sub = lax.broadcasted_iota(jnp.int32, prev8.shape, 0)
    head = jnp.where(sub < d, pltpu.roll(prev8, d, 0), rolled[:SUBLANES])
    return jnp.concatenate([head, rolled[SUBLANES:]], axis=0)


def _scan8(a, b):
    rows, width = a.shape
    groups = rows // SUBLANES
    a3 = a.reshape(groups, SUBLANES, width)
    b3 = b.reshape(groups, SUBLANES, width)
    sub = lax.broadcasted_iota(jnp.int32, (groups, SUBLANES, width), 1)
    for s in (1, 2, 4):
        keep = sub >= s
        b_prev = jnp.where(keep, pltpu.roll(b3, s, 1), 0.0)
        a_prev = jnp.where(keep, pltpu.roll(a3, s, 1), 1.0)
        b3 = b3 + a3 * b_prev
        a3 = a3 * a_prev
    return a3.reshape(rows, width), b3.reshape(rows, width)


def _row_bcast(ref, r):
    return jnp.broadcast_to(ref[pl.ds(r, 1), :], (SUBLANES, ref.shape[1]))


def _rope(t, cos, s1, s2):
    cols = []
    for j in range(t.shape[1] // LANES):
        tj = t[:, j * LANES:(j + 1) * LANES]
        up = pltpu.roll(tj, LANES - ROPE_DIM // 2, 1)
        down = pltpu.roll(tj, ROPE_DIM // 2, 1)
        cols.append(tj * cos + up * s1 + down * s2)
    return cols[0] if len(cols) == 1 else jnp.concatenate(cols, axis=1)


def _split_kv(t):
    swapped = pltpu.roll(t, HEAD_DIM, 1)
    low = lax.broadcasted_iota(jnp.int32, t.shape, 1) < HEAD_DIM
    zero = jnp.zeros_like(t)
    parts = [jnp.where(low, t, zero), jnp.where(low, zero, swapped),
             jnp.where(low, swapped, zero), jnp.where(low, zero, t)]
    return jnp.concatenate(parts, axis=1).astype(BF16)


def _gelu_gate_norm(hseq, gate, grec_ref):
    y = hseq * _gelu_tanh(gate)
    return _rms_norm(y, grec_ref[...]).astype(BF16)


def _pair_probs(s_band, band_ok, s_meta, meta_ok_even, meta_ok_odd, sink_even, sink_odd):
    rows = s_meta.shape[0]
    sme = jnp.where(meta_ok_even, s_meta, NEG_INF)
    smo = jnp.where(meta_ok_odd, s_meta, NEG_INF)
    m_e = jnp.maximum(jnp.max(sme, axis=-1, keepdims=True), sink_even)
    m_o = jnp.maximum(jnp.max(smo, axis=-1, keepdims=True), sink_odd)
    if s_band is not None:
        half = s_band.shape[1] // 2
        se = jnp.where(band_ok, s_band[:, :half], NEG_INF)
        so = jnp.where(band_ok, s_band[:, half:], NEG_INF)
        m_e = jnp.maximum(m_e, jnp.max(se, axis=-1, keepdims=True))
        m_o = jnp.maximum(m_o, jnp.max(so, axis=-1, keepdims=True))
    pme = jnp.exp(sme - m_e)
    pmo = jnp.exp(smo - m_o)
    den_e = jnp.sum(pme, axis=-1, keepdims=True) + jnp.exp(sink_even - m_e)
    den_o = jnp.sum(pmo, axis=-1, keepdims=True) + jnp.exp(sink_odd - m_o)
    p_band = None
    if s_band is not None:
        pe = jnp.exp(se - m_e)
        po = jnp.exp(so - m_o)
        den_e = den_e + jnp.sum(pe, axis=-1, keepdims=True)
        den_o = den_o + jnp.sum(po, axis=-1, keepdims=True)
        p_band = jnp.concatenate([pe, po], axis=1).astype(BF16)
    low = lax.broadcasted_iota(jnp.int32, (rows, LANES), 1) < HEAD_DIM
    inv = jnp.where(low, 1.0 / den_e, 1.0 / den_o)
    return p_band, (pme + pmo).astype(BF16), inv


def _pair_out(p_band, p_meta, inv, v_band, v_meta):
    out = _dot(p_meta, v_meta)
    if p_band is not None:
        out = out + _dot(p_band, v_band)
    return out * inv


def _col(ref_or_val, j, rows=slice(None)):
    return ref_or_val[rows, j * LANES:(j + 1) * LANES]


def _mix_a_kernel(h_ref, win_ref, cw_ref, cb_ref, wa_ref, ba_ref, wx_ref, bx_ref, lam_ref,
                  grec_ref, cosq_ref, s1q_ref, s2q_ref, cos_ref, s1_ref, s2_ref, ctail_ref,
                  hinit_ref,
                  yrec_ref, q_ref, k4_ref, v4_ref,
                  hist, state, a_buf, b_buf, h_buf):
    rows = h_ref.shape[0]
    ncol = REC_WIDTH // LANES

    @pl.when(pl.program_id(1) == 0)
    def _():
        hist[...] = ctail_ref[...]
        state[...] = hinit_ref[...]

    proj = _dot(h_ref[...].astype(BF16), win_ref[...])
    xr = proj[:, :REC_WIDTH]
    gate = proj[:, REC_WIDTH:2 * REC_WIDTH]
    q = proj[:, 2 * REC_WIDTH:2 * REC_WIDTH + ATTN_WIDTH]
    k = proj[:, 2 * REC_WIDTH + ATTN_WIDTH:2 * REC_WIDTH + ATTN_WIDTH + KV_WIDTH]
    v = proj[:, 2 * REC_WIDTH + ATTN_WIDTH + KV_WIDTH:]

    prev8 = hist[...]
    xc = _conv(xr, lambda d: _delayed(xr, prev8, d), cw_ref, cb_ref)
    hist[...] = xr[rows - SUBLANES:, :]
    a, b = _gate_terms(xc, wa_ref, ba_ref, wx_ref, bx_ref, lam_ref)

    seg = rows // SUBLANES
    for c in range(ncol):
        for s in range(SUBLANES):
            src_rows, lanes = slice(s * seg, (s + 1) * seg), slice(c * LANES, (c + 1) * LANES)
            a_buf[c, pl.ds(s * SEG_PITCH, seg), :] = a[src_rows, lanes]
            b_buf[c, pl.ds(s * SEG_PITCH, seg), :] = b[src_rows, lanes]
    sub = lax.broadcasted_iota(jnp.int32, (SUBLANES, LANES), 0)
    col_seqs = []
    for c in range(ncol):
        lanes = slice(c * LANES, (c + 1) * LANES)
        step = lambda buf, j: buf[c, pl.ds(j, SUBLANES, stride=SEG_PITCH), :]
        prod = step(a_buf, 0)
        loc = step(b_buf, 0)
        prods, locs = [prod], [loc]
        for j in range(1, seg):
            aj = step(a_buf, j)
            loc = aj * loc + step(b_buf, j)
            prod = aj * prod
            prods.append(prod)
            locs.append(loc)
        h0 = state[:, lanes]
        pa, pb = _scan8(prod, loc)
        after = pb + pa * h0
        enter = jnp.where(sub == 0, h0, pltpu.roll(after, 1, 0))
        for j in range(seg):
            h_buf[c, pl.ds(j, SUBLANES, stride=SEG_PITCH), :] = locs[j] + prods[j] * enter
        h_buf[c, pl.ds(seg, SUBLANES), :] = after
        state[:, lanes] = _row_bcast(h_buf.at[c], seg + SUBLANES - 1)
        col_seqs.append(jnp.concatenate(
            [h_buf[c, pl.ds(s * SEG_PITCH, seg), :] for s in range(SUBLANES)], axis=0))
    hseq = jnp.concatenate(col_seqs, axis=1)

    yrec_ref[...] = _gelu_gate_norm(hseq, gate, grec_ref)
    q_ref[...] = _rope(q, cosq_ref[...], s1q_ref[...], s2q_ref[...]).astype(BF16)
    k4_ref[...] = _split_kv(_rope(k, cos_ref[...], s1_ref[...], s2_ref[...]))
    v4_ref[...] = _split_kv(v)


def _mix_a(h, w_in, conv_w, conv_b, wa, ba, wx, bx, lam, g_rec, q_tables, k_tables, ctail,
           hinit, layer, batch, seq):
    rows = MIX_A_ROWS
    nc = seq // rows
    tokens = batch * seq
    tok = lambda width: pl.BlockSpec((rows, width), lambda bi, ci: (bi * nc + ci, 0))
    vec = lambda width: pl.BlockSpec((None, 1, width), lambda bi, ci: (layer, 0, 0))
    gates = pl.BlockSpec((None, 2, GATE_HALF, GATE_HALF), lambda bi, ci: (layer, 0, 0, 0))
    table = pl.BlockSpec((rows, LANES), lambda bi, ci: (ci, 0))
    seed = pl.BlockSpec((SUBLANES, REC_WIDTH), lambda bi, ci: (0, 0))
    ncol = REC_WIDTH // LANES
    return pl.pallas_call(
        _mix_a_kernel,
        grid=(batch, nc),
        in_specs=[
            tok(D_MODEL),
            pl.BlockSpec((None, D_MODEL, D_IN), lambda bi, ci: (layer, 0, 0),
                         pipeline_mode=pl.Buffered(1)),
            pl.BlockSpec((None, CONV_WIDTH, REC_WIDTH), lambda bi, ci: (layer, 0, 0)),
            vec(REC_WIDTH), gates, vec(REC_WIDTH), gates, vec(REC_WIDTH), vec(REC_WIDTH),
            vec(REC_WIDTH), table, table, table, table, table, table, seed, seed,
        ],
        out_specs=[tok(REC_WIDTH), tok(ATTN_WIDTH), tok(4 * LANES), tok(4 * LANES)],
        out_shape=[jax.ShapeDtypeStruct((tokens, REC_WIDTH), BF16),
                   jax.ShapeDtypeStruct((tokens, ATTN_WIDTH), BF16),
                   jax.ShapeDtypeStruct((tokens, 4 * LANES), BF16),
                   jax.ShapeDtypeStruct((tokens, 4 * LANES), BF16)],
        scratch_shapes=[
            pltpu.VMEM((SUBLANES, REC_WIDTH), F32),
            pltpu.VMEM((SUBLANES, REC_WIDTH), F32),
            pltpu.VMEM((ncol, SUBLANES * SEG_PITCH, LANES), F32),
            pltpu.VMEM((ncol, SUBLANES * SEG_PITCH, LANES), F32),
            pltpu.VMEM((ncol, SUBLANES * SEG_PITCH, LANES), F32),
        ],
        compiler_params=pltpu.CompilerParams(
            dimension_semantics=("arbitrary", "arbitrary"), vmem_limit_bytes=VMEM_LIMIT_BYTES),
    )(h, w_in, conv_w, conv_b, wa, ba, wx, bx, lam, g_rec, *q_tables, *k_tables, ctail, hinit)


def _window_probs(s, ok, sink_even, sink_odd):
    rows, half = s.shape[0], s.shape[1] // 2
    se = jnp.where(ok, s[:, :half], NEG_INF)
    so = jnp.where(ok, s[:, half:], NEG_INF)
    m_e = jnp.maximum(jnp.max(se, axis=-1, keepdims=True), sink_even)
    m_o = jnp.maximum(jnp.max(so, axis=-1, keepdims=True), sink_odd)
    pe = jnp.exp(se - m_e)
    po = jnp.exp(so - m_o)
    den_e = jnp.sum(pe, axis=-1, keepdims=True) + jnp.exp(sink_even - m_e)
    den_o = jnp.sum(po, axis=-1, keepdims=True) + jnp.exp(sink_odd - m_o)
    low = lax.broadcasted_iota(jnp.int32, (rows, LANES), 1) < HEAD_DIM
    inv = jnp.where(low, 1.0 / den_e, 1.0 / den_o)
    return jnp.concatenate([pe, po], axis=1).astype(BF16), inv


def _mix_b_kernel(layer, sink_ref, q_ref, k4_ref, v4_ref, kp_ref, vp_ref, km_ref, vm_ref,
                  yrec_ref, h_ref, wout_ref, gattn_ref, lng_ref, lnb_ref, o_ref):
    rows = q_ref.shape[0]
    first = pl.program_id(1) == 0
    span = WINDOW + ATT_BLOCK
    pad = 2 * WINDOW - span - N_META
    qi = lax.broadcasted_iota(jnp.int32, (ATT_BLOCK, 2 * WINDOW), 0)
    kj = lax.broadcasted_iota(jnp.int32, (ATT_BLOCK, 2 * WINDOW), 1)
    is_meta = kj < N_META
    lo = qi + (N_META + pad + 1)
    hi = qi + (N_META + pad + WINDOW)

    def visible(missing):
        floor = jnp.where(first, N_META + pad + missing, 0)
        return is_meta | ((kj >= jnp.maximum(lo, floor)) & (kj <= hi))

    zeros = jnp.zeros((pad, LANES), BF16)

    def window(ref, prev_ref, meta_ref, jb, g):
        end = (jb + 1) * ATT_BLOCK
        parts = []
        for cc in (2 * g, 2 * g + 1):
            parts += [_col(meta_ref, cc), zeros]
            if end < span:
                parts += [_col(prev_ref, cc, slice(end - span + WINDOW, WINDOW)),
                          _col(ref, cc, slice(0, end))]
            else:
                parts.append(_col(ref, cc, slice(end - span, end)))
        return jnp.concatenate(parts, axis=0)

    items = [(jb, g) for jb in range(rows // ATT_BLOCK) for g in range(N_KV_HEADS)]
    per_group = (OUT_ROWS // ATT_BLOCK) * N_KV_HEADS

    def scores(k):
        jb, g = items[k]
        cur = slice(jb * ATT_BLOCK, (jb + 1) * ATT_BLOCK)
        lhs = jnp.concatenate([_col(q_ref, 2 * g, cur), _col(q_ref, 2 * g + 1, cur)], axis=0)
        return _dot_nt(lhs, window(k4_ref, kp_ref, km_ref, jb, g))

    outs = {}

    def finish(m):
        rs = slice(m * OUT_ROWS, (m + 1) * OUT_ROWS)
        blocks = []
        for jb in range(m * OUT_ROWS // ATT_BLOCK, (m + 1) * OUT_ROWS // ATT_BLOCK):
            y = jnp.concatenate([outs[jb, cc] for cc in range(2 * N_KV_HEADS)], axis=1)
            blocks.append(_rms_norm(y, gattn_ref[...]).astype(BF16))
        y_attn = jnp.concatenate(blocks, axis=0)
        mixed = _dot(jnp.concatenate([yrec_ref[rs, :], y_attn], axis=1), wout_ref[...])
        o_ref[rs, :] = _layer_norm(DEEPNORM_ALPHA * h_ref[rs, :] + mixed,
                                   lng_ref[...], lnb_ref[...])

    ready = [scores(k) for k in range(min(QK_AHEAD, len(items)))]
    done = 0
    for k, (jb, g) in enumerate(items):
        if k + QK_AHEAD < len(items):
            ready.append(scores(k + QK_AHEAD))
        s = ready.pop(0)
        ok = visible(max(span - (jb + 1) * ATT_BLOCK, 0))
        p_cols, inv_cols = [], []
        for c in range(2):
            head = 4 * g + 2 * c
            p, inv = _window_probs(s[c * ATT_BLOCK:(c + 1) * ATT_BLOCK], ok,
                                   sink_ref[layer, head], sink_ref[layer, head + 1])
            p_cols.append(p)
            inv_cols.append(inv)
        out = _dot(jnp.concatenate(p_cols, axis=0), window(v4_ref, vp_ref, vm_ref, jb, g))
        out = out * jnp.concatenate(inv_cols, axis=0)
        outs[jb, 2 * g], outs[jb, 2 * g + 1] = out[:ATT_BLOCK], out[ATT_BLOCK:]
        if k + 1 >= (done + 1) * per_group + OUT_DELAY:
            finish(done)
            done += 1
    for m in range(done, rows // OUT_ROWS):
        finish(m)


def _mix_b(sinks, q, k4, v4, km4, vm4, yrec, h, w_out, g_attn, ln_g, ln_b, layer, batch, seq):
    rows = MIX_B_ROWS
    nc = seq // rows
    blocks_per_step = rows // WINDOW
    blocks_per_seq = seq // WINDOW
    tokens = batch * seq
    tok = lambda width: pl.BlockSpec((rows, width), lambda bi, ci: (bi * nc + ci, 0))
    prev = pl.BlockSpec(
        (WINDOW, 4 * LANES),
        lambda bi, ci: (jnp.maximum(bi * blocks_per_seq + ci * blocks_per_step - 1, 0), 0))
    meta = pl.BlockSpec((N_META, 4 * LANES), lambda bi, ci: (0, 0))
    vec = lambda width: pl.BlockSpec((None, 1, width), lambda bi, ci: (layer, 0, 0))
    return pl.pallas_call(
        functools.partial(_mix_b_kernel, layer),
        grid=(batch, nc),
        in_specs=[
            pl.BlockSpec(memory_space=pltpu.SMEM),
            tok(ATTN_WIDTH), tok(4 * LANES), tok(4 * LANES), prev, prev, meta, meta,
            tok(REC_WIDTH), tok(D_MODEL),
            pl.BlockSpec((None, D_MIX, D_MODEL), lambda bi, ci: (layer, 0, 0),
                         pipeline_mode=pl.Buffered(1)),
            vec(ATTN_WIDTH), vec(D_MODEL), vec(D_MODEL),
        ],
        out_specs=tok(D_MODEL),
        out_shape=jax.ShapeDtypeStruct((tokens, D_MODEL), F32),
        compiler_params=pltpu.CompilerParams(
            dimension_semantics=("arbitrary", "arbitrary"), vmem_limit_bytes=VMEM_LIMIT_BYTES),
    )(sinks, q, k4, v4, k4, v4, km4, vm4, yrec, h, w_out, g_attn, ln_g, ln_b)


def _meta_mix_kernel(layer, sink_ref, h_ref, win_ref, cw_ref, cb_ref, wa_ref, ba_ref, wx_ref,
                     bx_ref, lam_ref, grec_ref, cosq_ref, s1q_ref, s2q_ref, cos_ref, s1_ref,
                     s2_ref, wout_ref, gattn_ref, lng_ref, lnb_ref,
                     o_ref, ctail_ref, hstate_ref, k4_ref, v4_ref, hbuf):
    rows = N_META
    h_in = h_ref[...]
    proj = _dot(h_in.astype(BF16), win_ref[...])
    xr = proj[:, :REC_WIDTH]
    gate = proj[:, REC_WIDTH:2 * REC_WIDTH]
    q = proj[:, 2 * REC_WIDTH:2 * REC_WIDTH + ATTN_WIDTH]
    k = proj[:, 2 * REC_WIDTH + ATTN_WIDTH:2 * REC_WIDTH + ATTN_WIDTH + KV_WIDTH]
    v = proj[:, 2 * REC_WIDTH + ATTN_WIDTH + KV_WIDTH:]

    back = lambda d: _delayed(xr, jnp.zeros((SUBLANES, REC_WIDTH), F32), d)
    a, b = _gate_terms(_conv(xr, back, cw_ref, cb_ref), wa_ref, ba_ref, wx_ref, bx_ref, lam_ref)
    ctail_ref[...] = xr[rows - SUBLANES:, :]

    a1, b1 = _scan8(a, b)
    hbuf[0:SUBLANES, :] = b1[:SUBLANES]
    hbuf[SUBLANES:, :] = b1[SUBLANES:] + a1[SUBLANES:] * _row_bcast(hbuf, SUBLANES - 1)
    hseq = hbuf[...]
    hstate_ref[...] = _row_bcast(hbuf, rows - 1)
    y_rec = _gelu_gate_norm(hseq, gate, grec_ref)

    qb = _rope(q, cosq_ref[...], s1q_ref[...], s2q_ref[...]).astype(BF16)
    k4 = _split_kv(_rope(k, cos_ref[...], s1_ref[...], s2_ref[...]))
    v4 = _split_kv(v)
    k4_ref[...] = k4
    v4_ref[...] = v4

    qi = lax.broadcasted_iota(jnp.int32, (rows, 2 * N_META), 0)
    lane32 = lax.broadcasted_iota(jnp.int32, (rows, 2 * N_META), 1)
    meta_even = (lane32 < N_META) & (lane32 <= qi)
    meta_odd = (lane32 >= N_META) & (lane32 - N_META <= qi)
    cols = []
    for g in range(N_KV_HEADS):
        lo_c, hi_c = 2 * g, 2 * g + 1
        k_meta = jnp.concatenate([_col(k4, lo_c), _col(k4, hi_c)], axis=0)
        v_meta = jnp.concatenate([_col(v4, lo_c), _col(v4, hi_c)], axis=0)
        lhs = jnp.concatenate([_col(qb, lo_c), _col(qb, hi_c)], axis=0)
        s_meta = _dot_nt(lhs, k_meta)
        for c in range(2):
            head = 4 * g + 2 * c
            pieces = _pair_probs(None, None, s_meta[c * rows:(c + 1) * rows], meta_even, meta_odd,
                                 sink_ref[layer, head], sink_ref[layer, head + 1])
            cols.append(_pair_out(*pieces, None, v_meta))
    y_attn = _rms_norm(jnp.concatenate(cols, axis=1), gattn_ref[...]).astype(BF16)
    mixed = _dot(jnp.concatenate([y_rec, y_attn], axis=1), wout_ref[...])
    o_ref[...] = _layer_norm(DEEPNORM_ALPHA * h_in + mixed, lng_ref[...], lnb_ref[...])


def _meta_mix(sinks, h, w_in, conv_w, conv_b, wa, ba, wx, bx, lam, g_rec, q_tables, k_tables,
              w_out, g_attn, ln_g, ln_b, layer):
    vec = lambda width: pl.BlockSpec((None, 1, width), lambda i: (layer, 0, 0))
    gates = pl.BlockSpec((None, 2, GATE_HALF, GATE_HALF), lambda i: (layer, 0, 0, 0))
    table = pl.BlockSpec((N_META, LANES), lambda i: (0, 0))
    full = lambda r, w: pl.BlockSpec((r, w), lambda i: (0, 0))
    return pl.pallas_call(
        functools.partial(_meta_mix_kernel, layer),
        grid=(1,),
        in_specs=[
            pl.BlockSpec(memory_space=pltpu.SMEM),
            full(N_META, D_MODEL),
            pl.BlockSpec((None, D_MODEL, D_IN), lambda i: (layer, 0, 0)),
            pl.BlockSpec((None, CONV_WIDTH, REC_WIDTH), lambda i: (layer, 0, 0)),
            vec(REC_WIDTH), gates, vec(REC_WIDTH), gates, vec(REC_WIDTH), vec(REC_WIDTH),
            vec(REC_WIDTH), table, table, table, table, table, table,
            pl.BlockSpec((None, D_MIX, D_MODEL), lambda i: (layer, 0, 0)),
            vec(ATTN_WIDTH), vec(D_MODEL), vec(D_MODEL),
        ],
        out_specs=[full(N_META, D_MODEL), full(SUBLANES, REC_WIDTH), full(SUBLANES, REC_WIDTH),
                   full(N_META, 4 * LANES), full(N_META, 4 * LANES)],
        out_shape=[jax.ShapeDtypeStruct((N_META, D_MODEL), F32),
                   jax.ShapeDtypeStruct((SUBLANES, REC_WIDTH), F32),
                   jax.ShapeDtypeStruct((SUBLANES, REC_WIDTH), F32),
                   jax.ShapeDtypeStruct((N_META, 4 * LANES), BF16),
                   jax.ShapeDtypeStruct((N_META, 4 * LANES), BF16)],
        scratch_shapes=[pltpu.VMEM((N_META, REC_WIDTH), F32)],
        compiler_params=pltpu.CompilerParams(
            dimension_semantics=("arbitrary",), vmem_limit_bytes=VMEM_LIMIT_BYTES),
    )(sinks, h, w_in, conv_w, conv_b, wa, ba, wx, bx, lam, g_rec, *q_tables, *k_tables, w_out,
      g_attn, ln_g, ln_b)


def _block_diag_halves(w):
    per_half = REC_BLOCKS // 2
    out = jnp.zeros((w.shape[0], 2, GATE_HALF, GATE_HALF), w.dtype)
    for blk in range(REC_BLOCKS):
        half, pos = divmod(blk, per_half)
        sl = slice(pos * REC_BLOCK_DIM, (pos + 1) * REC_BLOCK_DIM)
        out = out.at[:, half, sl, sl].set(w[:, blk])
    return out.astype(BF16)


def _rope_tables(first_pos, count):
    pos = first_pos + jnp.arange(count, dtype=F32)
    inv_freq = ROPE_THETA ** (-jnp.arange(0, ROPE_DIM, 2, dtype=F32) / ROPE_DIM)
    ang = pos[:, None] * inv_freq[None, :]
    cos, sin = jnp.cos(ang), jnp.sin(ang)
    half = ROPE_DIM // 2
    pad = HEAD_DIM - ROPE_DIM
    ones = jnp.ones((count, pad), F32)
    zeros_h = jnp.zeros((count, half), F32)
    zeros_p = jnp.zeros((count, pad), F32)
    c_head = jnp.concatenate([cos, cos, ones], axis=1)
    s1_head = jnp.concatenate([-sin, zeros_h, zeros_p], axis=1)
    s2_head = jnp.concatenate([zeros_h, sin, zeros_p], axis=1)
    two = lambda t: jnp.concatenate([t, t], axis=1)
    return two(c_head), two(s1_head), two(s2_head)


def kernel(x, meta_tokens, ffn1_w_gate, ffn1_w_up, ffn1_w_down, ln1_g, ln1_b, w_in, conv_w, conv_b, gate_a_w, gate_a_b, gate_x_w, gate_x_b, lru_lambda, attn_sinks, norm_rec_g, norm_attn_g, w_out, ln2_g, ln2_b, ffn2_w_gate, ffn2_w_up, ffn2_w_down, ln3_g, ln3_b):
    batch, seq, _ = x.shape
    assert seq % MIX_A_ROWS == 0 and seq % MIX_B_ROWS == 0 and (batch * seq) % FFN_ROWS == 0
    assert MIX_A_ROWS // SUBLANES <= SEG_PITCH and SEG_PITCH % SUBLANES == 0

    bf = lambda w: w.astype(BF16)
    row = lambda p: p.reshape(p.shape[0], 1, p.shape[1])
    f1g, f1u, f1d = bf(ffn1_w_gate), bf(ffn1_w_up), bf(ffn1_w_down)
    f2g, f2u, f2d = bf(ffn2_w_gate), bf(ffn2_w_up), bf(ffn2_w_down)
    win, wout = bf(w_in), bf(w_out)
    wa, wx = _block_diag_halves(gate_a_w), _block_diag_halves(gate_x_w)
    l1g, l1b, l2g, l2b, l3g, l3b = map(row, (ln1_g, ln1_b, ln2_g, ln2_b, ln3_g, ln3_b))
    cb, ba, bx, lam, grec, gattn = map(
        row, (conv_b, gate_a_b, gate_x_b, lru_lambda, norm_rec_g, norm_attn_g))
    k_tab_m = _rope_tables(0.0, N_META)
    k_tab_x = _rope_tables(float(N_META), seq)
    q_tab_m = tuple(t * Q_SCALE for t in k_tab_m)
    q_tab_x = tuple(t * Q_SCALE for t in k_tab_x)

    h = x.reshape(batch * seq, D_MODEL)
    hm = meta_tokens.astype(x.dtype)
    for l in range(DEPTH):
        hm = _ffn(hm, f1g, f1u, f1d, l1g, l1b, l, N_META)
        h = _ffn(h, f1g, f1u, f1d, l1g, l1b, l, FFN_ROWS)
        hm, ctail, hinit, km4, vm4 = _meta_mix(
            attn_sinks, hm, win, conv_w, cb, wa, ba, wx, bx, lam, grec, q_tab_m, k_tab_m,
            wout, gattn, l2g, l2b, l)
        yrec, q, k4, v4 = _mix_a(h, win, conv_w, cb, wa, ba, wx, bx, lam, grec,
                                 q_tab_x, k_tab_x, ctail, hinit, l, batch, seq)
        h = _mix_b(attn_sinks, q, k4, v4, km4, vm4, yrec, h, wout, gattn, l2g, l2b,
                   l, batch, seq)
        if l + 1 < DEPTH:
            hm = _ffn(hm, f2g, f2u, f2d, l3g, l3b, l, N_META)
        h = _ffn(h, f2g, f2u, f2d, l3g, l3b, l, FFN_ROWS)
    return h.reshape(batch, seq, D_MODEL)
```

```python
import functools
import math

import jax
import jax.numpy as jnp
from jax import lax
from jax.experimental import pallas as pl
from jax.experimental.pallas import tpu as pltpu

F32 = jnp.float32
BF16 = jnp.bfloat16

D_MODEL = 1024
DEPTH = 4
N_META = 16
D_FF = 2816
REC_WIDTH = 512
REC_BLOCKS = 8
REC_BLOCK_DIM = REC_WIDTH // REC_BLOCKS
CONV_WIDTH = 4
LRU_C = 8.0
N_Q_HEADS = 8
N_KV_HEADS = 2
HEAD_DIM = 64
ATTN_WIDTH = N_Q_HEADS * HEAD_DIM
KV_WIDTH = N_KV_HEADS * HEAD_DIM
WINDOW = 128
ROPE_DIM = HEAD_DIM // 4
ROPE_THETA = 500000.0
D_IN = 2 * REC_WIDTH + ATTN_WIDTH + 2 * KV_WIDTH
D_MIX = REC_WIDTH + ATTN_WIDTH
DEEPNORM_ALPHA = (2.0 * DEPTH) ** 0.25
LN_EPS = 1e-5
RMS_EPS = 1e-6
NEG_INF = -1e30
Q_SCALE = HEAD_DIM ** -0.5
LOG2_E = 1.4426950408889634

LANES = 128
SUBLANES = 8
MXU_DIM = 256
VMEM_LIMIT_BYTES = 56 * 1024 * 1024

FFN_ROWS = 1024
FFN_SUB_ROWS = 256
FF_CHUNK = MXU_DIM
MIX_A_ROWS = 1024
MIX_B_ROWS = 1024
ATT_BLOCK = 64
OUT_ROWS = 256
OUT_DELAY = 4
SEG_PITCH = 136
QK_AHEAD = 2
GATE_HALF = REC_WIDTH // 2
SERIES_CUTOFF = -1.0 / 64.0


def _layer_norm(y, g, b):
    mu = jnp.mean(y, axis=-1, keepdims=True)
    d = y - mu
    var = jnp.mean(d * d, axis=-1, keepdims=True)
    return d * lax.rsqrt(var + LN_EPS) * g + b


def _rms_norm(y, g):
    ms = jnp.mean(y * y, axis=-1, keepdims=True)
    return y * lax.rsqrt(ms + RMS_EPS) * g


def _sigmoid(x):
    return 1.0 / (1.0 + jnp.exp2(x * (-LOG2_E)))


def _gelu_tanh(x):
    c = -2.0 * (2.0 / math.pi) ** 0.5 * LOG2_E
    return x / (1.0 + jnp.exp2(x * (c + (c * 0.044715) * (x * x))))


def _dot(a, b):
    return jnp.dot(a, b, preferred_element_type=F32)


def _dot_nt(a, b):
    return lax.dot_general(a, b, (((1,), (1,)), ((), ())), preferred_element_type=F32)


def _ffn_kernel(x_ref, wg_ref, wu_ref, wd_ref, g_ref, b_ref, o_ref, act_ref):
    sub = act_ref.shape[1]
    tiles = [slice(r * sub, (r + 1) * sub) for r in range(x_ref.shape[0] // sub)]
    for r, rs in enumerate(tiles):
        xb = x_ref[rs, :].astype(BF16)
        for c in range(D_FF // FF_CHUNK):
            sl = slice(c * FF_CHUNK, (c + 1) * FF_CHUNK)
            gate = _dot(xb, wg_ref[:, sl])
            up = _dot(xb, wu_ref[:, sl])
            act_ref[r, :, sl] = (gate * _sigmoid(gate) * up).astype(BF16)
    down = [_dot(act_ref[r], wd_ref[...]) for r in range(len(tiles))]
    for rs, d in zip(tiles, down):
        y = DEEPNORM_ALPHA * x_ref[rs, :] + 0.5 * d
        o_ref[rs, :] = _layer_norm(y, g_ref[...], b_ref[...])


def _ffn(x, wg, wu, wd, g, b, layer, rows):
    tokens = x.shape[0]
    resident = dict(pipeline_mode=pl.Buffered(1))
    sub = min(rows, FFN_SUB_ROWS)
    return pl.pallas_call(
        _ffn_kernel,
        grid=(tokens // rows,),
        scratch_shapes=[pltpu.VMEM((rows // sub, sub, D_FF), BF16)],
        in_specs=[
            pl.BlockSpec((rows, D_MODEL), lambda i: (i, 0)),
            pl.BlockSpec((None, D_MODEL, D_FF), lambda i: (layer, 0, 0), **resident),
            pl.BlockSpec((None, D_MODEL, D_FF), lambda i: (layer, 0, 0), **resident),
            pl.BlockSpec((None, D_FF, D_MODEL), lambda i: (layer, 0, 0), **resident),
            pl.BlockSpec((None, 1, D_MODEL), lambda i: (layer, 0, 0)),
            pl.BlockSpec((None, 1, D_MODEL), lambda i: (layer, 0, 0)),
        ],
        out_specs=pl.BlockSpec((rows, D_MODEL), lambda i: (i, 0)),
        out_shape=jax.ShapeDtypeStruct((tokens, D_MODEL), F32),
        compiler_params=pltpu.CompilerParams(
            dimension_semantics=("arbitrary",), vmem_limit_bytes=VMEM_LIMIT_BYTES),
    )(x, wg, wu, wd, g, b)


def _softplus(x):
    return jnp.maximum(x, 0.0) + jnp.log1p(jnp.exp(-jnp.abs(x)))


def _conv(xr, back, cw_ref, cb_ref):
    xc = cb_ref[...] + back(3) * cw_ref[0:1, :]
    xc = xc + back(2) * cw_ref[1:2, :]
    xc = xc + back(1) * cw_ref[2:3, :]
    return xc + xr * cw_ref[3:4, :]


def _gate_terms(xc, wa_ref, ba_ref, wx_ref, bx_ref, lam_ref):
    xcb = xc.astype(BF16)
    lo, hi = xcb[:, :GATE_HALF], xcb[:, GATE_HALF:]
    r_lin = jnp.concatenate([_dot(lo, wa_ref[0]), _dot(hi, wa_ref[1])], axis=1) + ba_ref[...]
    i_lin = jnp.concatenate([_dot(lo, wx_ref[0]), _dot(hi, wx_ref[1])], axis=1) + bx_ref[...]
    r = _sigmoid(r_lin)
    i = _sigmoid(i_lin)
    log_a = (-LRU_C * _softplus(-lam_ref[...])) * r
    a = jnp.exp(log_a)
    y = 2.0 * log_a
    series = -y * (1.0 + y * (0.5 + y * (1.0 / 6.0)))
    one_minus_a2 = jnp.where(y > SERIES_CUTOFF, series, 1.0 - a * a)
    b = jnp.sqrt(one_minus_a2) * (i * xc)
    return a, b


def _delayed(xr, prev8, d):
    rolled = pltpu.roll(xr, d, 0)
    sub = lax.broadcasted_iota(jnp.int32, prev8.shape, 0)
    head = jnp.where(sub < d, pltpu.roll(prev8, d, 0), rolled[:SUBLANES])
    return jnp.concatenate([head, rolled[SUBLANES:]], axis=0)


def _scan8(a, b):
    rows, width = a.shape
    groups = rows // SUBLANES
    a3 = a.reshape(groups, SUBLANES, width)
    b3 = b.reshape(groups, SUBLANES, width)
    sub = lax.broadcasted_iota(jnp.int32, (groups, SUBLANES, width), 1)
    for s in (1, 2, 4):
        keep = sub >= s
        b_prev = jnp.where(keep, pltpu.roll(b3, s, 1), 0.0)
        a_prev = jnp.where(keep, pltpu.roll(a3, s, 1), 1.0)
        b3 = b3 + a3 * b_prev
        a3 = a3 * a_prev
    return a3.reshape(rows, width), b3.reshape(rows, width)


def _row_bcast(ref, r):
    return jnp.broadcast_to(ref[pl.ds(r, 1), :], (SUBLANES, ref.shape[1]))


def _rope(t, cos, s1, s2):
    cols = []
    for j in range(t.shape[1] // LANES):
        tj = t[:, j * LANES:(j + 1) * LANES]
        up = pltpu.roll(tj, LANES - ROPE_DIM // 2, 1)
        down = pltpu.roll(tj, ROPE_DIM // 2, 1)
        cols.append(tj * cos + up * s1 + down * s2)
    return cols[0] if len(cols) == 1 else jnp.concatenate(cols, axis=1)


def _split_kv(t):
    swapped = pltpu.roll(t, HEAD_DIM, 1)
    low = lax.broadcasted_iota(jnp.int32, t.shape, 1) < HEAD_DIM
    zero = jnp.zeros_like(t)
    parts = [jnp.where(low, t, zero), jnp.where(low, zero, swapped),
             jnp.where(low, swapped, zero), jnp.where(low, zero, t)]
    return jnp.concatenate(parts, axis=1).astype(BF16)


def _gelu_gate_norm(hseq, gate, grec_ref):
    y = hseq * _gelu_tanh(gate)
    return _rms_norm(y, grec_ref[...]).astype(BF16)


def _pair_probs(s_band, band_ok, s_meta, meta_ok_even, meta_ok_odd, sink_even, sink_odd):
    rows = s_meta.shape[0]
    sme = jnp.where(meta_ok_even, s_meta, NEG_INF)
    smo = jnp.where(meta_ok_odd, s_meta, NEG_INF)
    m_e = jnp.maximum(jnp.max(sme, axis=-1, keepdims=True), sink_even)
    m_o = jnp.maximum(jnp.max(smo, axis=-1, keepdims=True), sink_odd)
    if s_band is not None:
        half = s_band.shape[1] // 2
        se = jnp.where(band_ok, s_band[:, :half], NEG_INF)
        so = jnp.where(band_ok, s_band[:, half:], NEG_INF)
        m_e = jnp.maximum(m_e, jnp.max(se, axis=-1, keepdims=True))
        m_o = jnp.maximum(m_o, jnp.max(so, axis=-1, keepdims=True))
    pme = jnp.exp(sme - m_e)
    pmo = jnp.exp(smo - m_o)
    den_e = jnp.sum(pme, axis=-1, keepdims=True) + jnp.exp(sink_even - m_e)
    den_o = jnp.sum(pmo, axis=-1, keepdims=True) + jnp.exp(sink_odd - m_o)
    p_band = None
    if s_band is not None:
        pe = jnp.exp(se - m_e)
        po = jnp.exp(so - m_o)
        den_e = den_e + jnp.sum(pe, axis=-1, keepdims=True)
        den_o = den_o + jnp.sum(po, axis=-1, keepdims=True)
        p_band = jnp.concatenate([pe, po], axis=1).astype(BF16)
    low = lax.broadcasted_iota(jnp.int32, (rows, LANES), 1) < HEAD_DIM
    inv = jnp.where(low, 1.0 / den_e, 1.0 / den_o)
    return p_band, (pme + pmo).astype(BF16), inv


def _pair_out(p_band, p_meta, inv, v_band, v_meta):
    out = _dot(p_meta, v_meta)
    if p_band is not None:
        out = out + _dot(p_band, v_band)
    return out * inv


def _col(ref_or_val, j, rows=slice(None)):
    return ref_or_val[rows, j * LANES:(j + 1) * LANES]


def _mix_a_kernel(h_ref, win_ref, cw_ref, cb_ref, wa_ref, ba_ref, wx_ref, bx_ref, lam_ref,
                  grec_ref, cosq_ref, s1q_ref, s2q_ref, cos_ref, s1_ref, s2_ref, ctail_ref,
                  hinit_ref,
                  yrec_ref, q_ref, k4_ref, v4_ref,
                  hist, state, a_buf, b_buf, h_buf):
    rows = h_ref.shape[0]
    ncol = REC_WIDTH // LANES

    @pl.when(pl.program_id(1) == 0)
    def _():
        hist[...] = ctail_ref[...]
        state[...] = hinit_ref[...]

    proj = _dot(h_ref[...].astype(BF16), win_ref[...])
    xr = proj[:, :REC_WIDTH]
    gate = proj[:, REC_WIDTH:2 * REC_WIDTH]
    q = proj[:, 2 * REC_WIDTH:2 * REC_WIDTH + ATTN_WIDTH]
    k = proj[:, 2 * REC_WIDTH + ATTN_WIDTH:2 * REC_WIDTH + ATTN_WIDTH + KV_WIDTH]
    v = proj[:, 2 * REC_WIDTH + ATTN_WIDTH + KV_WIDTH:]

    prev8 = hist[...]
    xc = _conv(xr, lambda d: _delayed(xr, prev8, d), cw_ref, cb_ref)
    hist[...] = xr[rows - SUBLANES:, :]
    a, b = _gate_terms(xc, wa_ref, ba_ref, wx_ref, bx_ref, lam_ref)

    seg = rows // SUBLANES
    for c in range(ncol):
        for s in range(SUBLANES):
            src_rows, lanes = slice(s * seg, (s + 1) * seg), slice(c * LANES, (c + 1) * LANES)
            a_buf[c, pl.ds(s * SEG_PITCH, seg), :] = a[src_rows, lanes]
            b_buf[c, pl.ds(s * SEG_PITCH, seg), :] = b[src_rows, lanes]
    sub = lax.broadcasted_iota(jnp.int32, (SUBLANES, LANES), 0)
    col_seqs = []
    for c in range(ncol):
        lanes = slice(c * LANES, (c + 1) * LANES)
        step = lambda buf, j: buf[c, pl.ds(j, SUBLANES, stride=SEG_PITCH), :]
        prod = step(a_buf, 0)
        loc = step(b_buf, 0)
        prods, locs = [prod], [loc]
        for j in range(1, seg):
            aj = step(a_buf, j)
            loc = aj * loc + step(b_buf, j)
            prod = aj * prod
            prods.append(prod)
            locs.append(loc)
        h0 = state[:, lanes]
        pa, pb = _scan8(prod, loc)
        after = pb + pa * h0
        enter = jnp.where(sub == 0, h0, pltpu.roll(after, 1, 0))
        for j in range(seg):
            h_buf[c, pl.ds(j, SUBLANES, stride=SEG_PITCH), :] = locs[j] + prods[j] * enter
        h_buf[c, pl.ds(seg, SUBLANES), :] = after
        state[:, lanes] = _row_bcast(h_buf.at[c], seg + SUBLANES - 1)
        col_seqs.append(jnp.concatenate(
            [h_buf[c, pl.ds(s * SEG_PITCH, seg), :] for s in range(SUBLANES)], axis=0))
    hseq = jnp.concatenate(col_seqs, axis=1)

    yrec_ref[...] = _gelu_gate_norm(hseq, gate, grec_ref)
    q_ref[...] = _rope(q, cosq_ref[...], s1q_ref[...], s2q_ref[...]).astype(BF16)
    k4_ref[...] = _split_kv(_rope(k, cos_ref[...], s1_ref[...], s2_ref[...]))
    v4_ref[...] = _split_kv(v)


def _mix_a(h, w_in, conv_w, conv_b, wa, ba, wx, bx, lam, g_rec, q_tables, k_tables, ctail,
           hinit, layer, batch, seq):
    rows = MIX_A_ROWS
    nc = seq // rows
    tokens = batch * seq
    tok = lambda width: pl.BlockSpec((rows, width), lambda bi, ci: (bi * nc + ci, 0))
    vec = lambda width: pl.BlockSpec((None, 1, width), lambda bi, ci: (layer, 0, 0))
    gates = pl.BlockSpec((None, 2, GATE_HALF, GATE_HALF), lambda bi, ci: (layer, 0, 0, 0))
    table = pl.BlockSpec((rows, LANES), lambda bi, ci: (ci, 0))
    seed = pl.BlockSpec((SUBLANES, REC_WIDTH), lambda bi, ci: (0, 0))
    ncol = REC_WIDTH // LANES
    return pl.pallas_call(
        _mix_a_kernel,
        grid=(batch, nc),
        in_specs=[
            tok(D_MODEL),
            pl.BlockSpec((None, D_MODEL, D_IN), lambda bi, ci: (layer, 0, 0),
                         pipeline_mode=pl.Buffered(1)),
            pl.BlockSpec((None, CONV_WIDTH, REC_WIDTH), lambda bi, ci: (layer, 0, 0)),
            vec(REC_WIDTH), gates, vec(REC_WIDTH), gates, vec(REC_WIDTH), vec(REC_WIDTH),
            vec(REC_WIDTH), table, table, table, table, table, table, seed, seed,
        ],
        out_specs=[tok(REC_WIDTH), tok(ATTN_WIDTH), tok(4 * LANES), tok(4 * LANES)],
        out_shape=[jax.ShapeDtypeStruct((tokens, REC_WIDTH), BF16),
                   jax.ShapeDtypeStruct((tokens, ATTN_WIDTH), BF16),
                   jax.ShapeDtypeStruct((tokens, 4 * LANES), BF16),
                   jax.ShapeDtypeStruct((tokens, 4 * LANES), BF16)],
        scratch_shapes=[
            pltpu.VMEM((SUBLANES, REC_WIDTH), F32),
            pltpu.VMEM((SUBLANES, REC_WIDTH), F32),
            pltpu.VMEM((ncol, SUBLANES * SEG_PITCH, LANES), F32),
            pltpu.VMEM((ncol, SUBLANES * SEG_PITCH, LANES), F32),
            pltpu.VMEM((ncol, SUBLANES * SEG_PITCH, LANES), F32),
        ],
        compiler_params=pltpu.CompilerParams(
            dimension_semantics=("arbitrary", "arbitrary"), vmem_limit_bytes=VMEM_LIMIT_BYTES),
    )(h, w_in, conv_w, conv_b, wa, ba, wx, bx, lam, g_rec, *q_tables, *k_tables, ctail, hinit)


def _window_probs(s, ok, sink_even, sink_odd):
    rows, half = s.shape[0], s.shape[1] // 2
    se = jnp.where(ok, s[:, :half], NEG_INF)
    so = jnp.where(ok, s[:, half:], NEG_INF)
    m_e = jnp.maximum(jnp.max(se, axis=-1, keepdims=True), sink_even)
    m_o = jnp.maximum(jnp.max(so, axis=-1, keepdims=True), sink_odd)
    pe = jnp.exp(se - m_e)
    po = jnp.exp(so - m_o)
    den_e = jnp.sum(pe, axis=-1, keepdims=True) + jnp.exp(sink_even - m_e)
    den_o = jnp.sum(po, axis=-1, keepdims=True) + jnp.exp(sink_odd - m_o)
    low = lax.broadcasted_iota(jnp.int32, (rows, LANES), 1) < HEAD_DIM
    inv = jnp.where(low, 1.0 / den_e, 1.0 / den_o)
    return jnp.concatenate([pe, po], axis=1).astype(BF16), inv


def _mix_b_kernel(layer, sink_ref, q_ref, k4_ref, v4_ref, kp_ref, vp_ref, km_ref, vm_ref,
                  yrec_ref, h_ref, wout_ref, gattn_ref, lng_ref, lnb_ref, o_ref):
    rows = q_ref.shape[0]
    first = pl.program_id(1) == 0
    span = WINDOW + ATT_BLOCK
    pad = 2 * WINDOW - span - N_META
    qi = lax.broadcasted_iota(jnp.int32, (ATT_BLOCK, 2 * WINDOW), 0)
    kj = lax.broadcasted_iota(jnp.int32, (ATT_BLOCK, 2 * WINDOW), 1)
    is_meta = kj < N_META
    lo = qi + (N_META + pad + 1)
    hi = qi + (N_META + pad + WINDOW)

    def visible(missing):
        floor = jnp.where(first, N_META + pad + missing, 0)
        return is_meta | ((kj >= jnp.maximum(lo, floor)) & (kj <= hi))

    zeros = jnp.zeros((pad, LANES), BF16)

    def window(ref, prev_ref, meta_ref, jb, g):
        end = (jb + 1) * ATT_BLOCK
        parts = []
        for cc in (2 * g, 2 * g + 1):
            parts += [_col(meta_ref, cc), zeros]
            if end < span:
                parts += [_col(prev_ref, cc, slice(end - span + WINDOW, WINDOW)),
                          _col(ref, cc, slice(0, end))]
            else:
                parts.append(_col(ref, cc, slice(end - span, end)))
        return jnp.concatenate(parts, axis=0)

    items = [(jb, g) for jb in range(rows // ATT_BLOCK) for g in range(N_KV_HEADS)]
    per_group = (OUT_ROWS // ATT_BLOCK) * N_KV_HEADS

    def scores(k):
        jb, g = items[k]
        cur = slice(jb * ATT_BLOCK, (jb + 1) * ATT_BLOCK)
        lhs = jnp.concatenate([_col(q_ref, 2 * g, cur), _col(q_ref, 2 * g + 1, cur)], axis=0)
        return _dot_nt(lhs, window(k4_ref, kp_ref, km_ref, jb, g))

    outs = {}

    def finish(m):
        rs = slice(m * OUT_ROWS, (m + 1) * OUT_ROWS)
        blocks = []
        for jb in range(m * OUT_ROWS // ATT_BLOCK, (m + 1) * OUT_ROWS // ATT_BLOCK):
            y = jnp.concatenate([outs[jb, cc] for cc in range(2 * N_KV_HEADS)], axis=1)
            blocks.append(_rms_norm(y, gattn_ref[...]).astype(BF16))
        y_attn = jnp.concatenate(blocks, axis=0)
        mixed = _dot(jnp.concatenate([yrec_ref[rs, :], y_attn], axis=1), wout_ref[...])
        o_ref[rs, :] = _layer_norm(DEEPNORM_ALPHA * h_ref[rs, :] + mixed,
                                   lng_ref[...], lnb_ref[...])

    ready = [scores(k) for k in range(min(QK_AHEAD, len(items)))]
    done = 0
    for k, (jb, g) in enumerate(items):
        if k + QK_AHEAD < len(items):
            ready.append(scores(k + QK_AHEAD))
        s = ready.pop(0)
        ok = visible(max(span - (jb + 1) * ATT_BLOCK, 0))
        p_cols, inv_cols = [], []
        for c in range(2):
            head = 4 * g + 2 * c
            p, inv = _window_probs(s[c * ATT_BLOCK:(c + 1) * ATT_BLOCK], ok,
                                   sink_ref[layer, head], sink_ref[layer, head + 1])
            p_cols.append(p)
            inv_cols.append(inv)
        out = _dot(jnp.concatenate(p_cols, axis=0), window(v4_ref, vp_ref, vm_ref, jb, g))
        out = out * jnp.concatenate(inv_cols, axis=0)
        outs[jb, 2 * g], outs[jb, 2 * g + 1] = out[:ATT_BLOCK], out[ATT_BLOCK:]
        if k + 1 >= (done + 1) * per_group + OUT_DELAY:
            finish(done)
            done += 1
    for m in range(done, rows // OUT_ROWS):
        finish(m)


def _mix_b(sinks, q, k4, v4, km4, vm4, yrec, h, w_out, g_attn, ln_g, ln_b, layer, batch, seq):
    rows = MIX_B_ROWS
    nc = seq // rows
    blocks_per_step = rows // WINDOW
    blocks_per_seq = seq // WINDOW
    tokens = batch * seq
    tok = lambda width: pl.BlockSpec((rows, width), lambda bi, ci: (bi * nc + ci, 0))
    prev = pl.BlockSpec(
        (WINDOW, 4 * LANES),
        lambda bi, ci: (jnp.maximum(bi * blocks_per_seq + ci * blocks_per_step - 1, 0), 0))
    meta = pl.BlockSpec((N_META, 4 * LANES), lambda bi, ci: (0, 0))
    vec = lambda width: pl.BlockSpec((None, 1, width), lambda bi, ci: (layer, 0, 0))
    return pl.pallas_call(
        functools.partial(_mix_b_kernel, layer),
        grid=(batch, nc),
        in_specs=[
            pl.BlockSpec(memory_space=pltpu.SMEM),
            tok(ATTN_WIDTH), tok(4 * LANES), tok(4 * LANES), prev, prev, meta, meta,
            tok(REC_WIDTH), tok(D_MODEL),
            pl.BlockSpec((None, D_MIX, D_MODEL), lambda bi, ci: (layer, 0, 0),
                         pipeline_mode=pl.Buffered(1)),
            vec(ATTN_WIDTH), vec(D_MODEL), vec(D_MODEL),
        ],
        out_specs=tok(D_MODEL),
        out_shape=jax.ShapeDtypeStruct((tokens, D_MODEL), F32),
        compiler_params=pltpu.CompilerParams(
            dimension_semantics=("arbitrary", "arbitrary"), vmem_limit_bytes=VMEM_LIMIT_BYTES),
    )(sinks, q, k4, v4, k4, v4, km4, vm4, yrec, h, w_out, g_attn, ln_g, ln_b)


def _meta_mix_kernel(layer, sink_ref, h_ref, win_ref, cw_ref, cb_ref, wa_ref, ba_ref, wx_ref,
                     bx_ref, lam_ref, grec_ref, cosq_ref, s1q_ref, s2q_ref, cos_ref, s1_ref,
                     s2_ref, wout_ref, gattn_ref, lng_ref, lnb_ref,
                     o_ref, ctail_ref, hstate_ref, k4_ref, v4_ref, hbuf):
    rows = N_META
    h_in = h_ref[...]
    proj = _dot(h_in.astype(BF16), win_ref[...])
    xr = proj[:, :REC_WIDTH]
    gate = proj[:, REC_WIDTH:2 * REC_WIDTH]
    q = proj[:, 2 * REC_WIDTH:2 * REC_WIDTH + ATTN_WIDTH]
    k = proj[:, 2 * REC_WIDTH + ATTN_WIDTH:2 * REC_WIDTH + ATTN_WIDTH + KV_WIDTH]
    v = proj[:, 2 * REC_WIDTH + ATTN_WIDTH + KV_WIDTH:]

    back = lambda d: _delayed(xr, jnp.zeros((SUBLANES, REC_WIDTH), F32), d)
    a, b = _gate_terms(_conv(xr, back, cw_ref, cb_ref), wa_ref, ba_ref, wx_ref, bx_ref, lam_ref)
    ctail_ref[...] = xr[rows - SUBLANES:, :]

    a1, b1 = _scan8(a, b)
    hbuf[0:SUBLANES, :] = b1[:SUBLANES]
    hbuf[SUBLANES:, :] = b1[SUBLANES:] + a1[SUBLANES:] * _row_bcast(hbuf, SUBLANES - 1)
    hseq = hbuf[...]
    hstate_ref[...] = _row_bcast(hbuf, rows - 1)
    y_rec = _gelu_gate_norm(hseq, gate, grec_ref)

    qb = _rope(q, cosq_ref[...], s1q_ref[...], s2q_ref[...]).astype(BF16)
    k4 = _split_kv(_rope(k, cos_ref[...], s1_ref[...], s2_ref[...]))
    v4 = _split_kv(v)
    k4_ref[...] = k4
    v4_ref[...] = v4

    qi = lax.broadcasted_iota(jnp.int32, (rows, 2 * N_META), 0)
    lane32 = lax.broadcasted_iota(jnp.int32, (rows, 2 * N_META), 1)
    meta_even = (lane32 < N_META) & (lane32 <= qi)
    meta_odd = (lane32 >= N_META) & (lane32 - N_META <= qi)
    cols = []
    for g in range(N_KV_HEADS):
        lo_c, hi_c = 2 * g, 2 * g + 1
        k_meta = jnp.concatenate([_col(k4, lo_c), _col(k4, hi_c)], axis=0)
        v_meta = jnp.concatenate([_col(v4, lo_c), _col(v4, hi_c)], axis=0)
        lhs = jnp.concatenate([_col(qb, lo_c), _col(qb, hi_c)], axis=0)
        s_meta = _dot_nt(lhs, k_meta)
        for c in range(2):
            head = 4 * g + 2 * c
            pieces = _pair_probs(None, None, s_meta[c * rows:(c + 1) * rows], meta_even, meta_odd,
                                 sink_ref[layer, head], sink_ref[layer, head + 1])
            cols.append(_pair_out(*pieces, None, v_meta))
    y_attn = _rms_norm(jnp.concatenate(cols, axis=1), gattn_ref[...]).astype(BF16)
    mixed = _dot(jnp.concatenate([y_rec, y_attn], axis=1), wout_ref[...])
    o_ref[...] = _layer_norm(DEEPNORM_ALPHA * h_in + mixed, lng_ref[...], lnb_ref[...])


def _meta_mix(sinks, h, w_in, conv_w, conv_b, wa, ba, wx, bx, lam, g_rec, q_tables, k_tables,
              w_out, g_attn, ln_g, ln_b, layer):
    vec = lambda width: pl.BlockSpec((None, 1, width), lambda i: (layer, 0, 0))
    gates = pl.BlockSpec((None, 2, GATE_HALF, GATE_HALF), lambda i: (layer, 0, 0, 0))
    table = pl.BlockSpec((N_META, LANES), lambda i: (0, 0))
    full = lambda r, w: pl.BlockSpec((r, w), lambda i: (0, 0))
    return pl.pallas_call(
        functools.partial(_meta_mix_kernel, layer),
        grid=(1,),
        in_specs=[
            pl.BlockSpec(memory_space=pltpu.SMEM),
            full(N_META, D_MODEL),
            pl.BlockSpec((None, D_MODEL, D_IN), lambda i: (layer, 0, 0)),
            pl.BlockSpec((None, CONV_WIDTH, REC_WIDTH), lambda i: (layer, 0, 0)),
            vec(REC_WIDTH), gates, vec(REC_WIDTH), gates, vec(REC_WIDTH), vec(REC_WIDTH),
            vec(REC_WIDTH), table, table, table, table, table, table,
            pl.BlockSpec((None, D_MIX, D_MODEL), lambda i: (layer, 0, 0)),
            vec(ATTN_WIDTH), vec(D_MODEL), vec(D_MODEL),
        ],
        out_specs=[full(N_META, D_MODEL), full(SUBLANES, REC_WIDTH), full(SUBLANES, REC_WIDTH),
                   full(N_META, 4 * LANES), full(N_META, 4 * LANES)],
        out_shape=[jax.ShapeDtypeStruct((N_META, D_MODEL), F32),
                   jax.ShapeDtypeStruct((SUBLANES, REC_WIDTH), F32),
                   jax.ShapeDtypeStruct((SUBLANES, REC_WIDTH), F32),
                   jax.ShapeDtypeStruct((N_META, 4 * LANES), BF16),
                   jax.ShapeDtypeStruct((N_META, 4 * LANES), BF16)],
        scratch_shapes=[pltpu.VMEM((N_META, REC_WIDTH), F32)],
        compiler_params=pltpu.CompilerParams(
            dimension_semantics=("arbitrary",), vmem_limit_bytes=VMEM_LIMIT_BYTES),
    )(sinks, h, w_in, conv_w, conv_b, wa, ba, wx, bx, lam, g_rec, *q_tables, *k_tables, w_out,
      g_attn, ln_g, ln_b)


def _block_diag_halves(w):
    per_half = REC_BLOCKS // 2
    out = jnp.zeros((w.shape[0], 2, GATE_HALF, GATE_HALF), w.dtype)
    for blk in range(REC_BLOCKS):
        half, pos = divmod(blk, per_half)
        sl = slice(pos * REC_BLOCK_DIM, (pos + 1) * REC_BLOCK_DIM)
        out = out.at[:, half, sl, sl].set(w[:, blk])
    return out.astype(BF16)


def _rope_tables(first_pos, count):
    pos = first_pos + jnp.arange(count, dtype=F32)
    inv_freq = ROPE_THETA ** (-jnp.arange(0, ROPE_DIM, 2, dtype=F32) / ROPE_DIM)
    ang = pos[:, None] * inv_freq[None, :]
    cos, sin = jnp.cos(ang), jnp.sin(ang)
    half = ROPE_DIM // 2
    pad = HEAD_DIM - ROPE_DIM
    ones = jnp.ones((count, pad), F32)
    zeros_h = jnp.zeros((count, half), F32)
    zeros_p = jnp.zeros((count, pad), F32)
    c_head = jnp.concatenate([cos, cos, ones], axis=1)
    s1_head = jnp.concatenate([-sin, zeros_h, zeros_p], axis=1)
    s2_head = jnp.concatenate([zeros_h, sin, zeros_p], axis=1)
    two = lambda t: jnp.concatenate([t, t], axis=1)
    return two(c_head), two(s1_head), two(s2_head)


def kernel(x, meta_tokens, ffn1_w_gate, ffn1_w_up, ffn1_w_down, ln1_g, ln1_b, w_in, conv_w, conv_b, gate_a_w, gate_a_b, gate_x_w, gate_x_b, lru_lambda, attn_sinks, norm_rec_g, norm_attn_g, w_out, ln2_g, ln2_b, ffn2_w_gate, ffn2_w_up, ffn2_w_down, ln3_g, ln3_b):
    batch, seq, _ = x.shape
    assert seq % MIX_A_ROWS == 0 and seq % MIX_B_ROWS == 0 and (batch * seq) % FFN_ROWS == 0
    assert MIX_A_ROWS // SUBLANES <= SEG_PITCH and SEG_PITCH % SUBLANES == 0

    bf = lambda w: w.astype(BF16)
    row = lambda p: p.reshape(p.shape[0], 1, p.shape[1])
    f1g, f1u, f1d = bf(ffn1_w_gate), bf(ffn1_w_up), bf(ffn1_w_down)
    f2g, f2u, f2d = bf(ffn2_w_gate), bf(ffn2_w_up), bf(ffn2_w_down)
    win, wout = bf(w_in), bf(w_out)
    wa, wx = _block_diag_halves(gate_a_w), _block_diag_halves(gate_x_w)
    l1g, l1b, l2g, l2b, l3g, l3b = map(row, (ln1_g, ln1_b, ln2_g, ln2_b, ln3_g, ln3_b))
    cb, ba, bx, lam, grec, gattn = map(
        row, (conv_b, gate_a_b, gate_x_b, lru_lambda, norm_rec_g, norm_attn_g))
    k_tab_m = _rope_tables(0.0, N_META)
    k_tab_x = _rope_tables(float(N_META), seq)
    q_tab_m = tuple(t * Q_SCALE for t in k_tab_m)
    q_tab_x = tuple(t * Q_SCALE for t in k_tab_x)

    h = x.reshape(batch * seq, D_MODEL)
    hm = meta_tokens.astype(x.dtype)
    for l in range(DEPTH):
        hm = _ffn(hm, f1g, f1u, f1d, l1g, l1b, l, N_META)
        h = _ffn(h, f1g, f1u, f1d, l1g, l1b, l, FFN_ROWS)
        hm, ctail, hinit, km4, vm4 = _meta_mix(
            attn_sinks, hm, win, conv_w, cb, wa, ba, wx, bx, lam, grec, q_tab_m, k_tab_m,
            wout, gattn, l2g, l2b, l)
        yrec, q, k4, v4 = _mix_a(h, win, conv_w, cb, wa, ba, wx, bx, lam, grec,
                                 q_tab_x, k_tab_x, ctail, hinit, l, batch, seq)
        h = _mix_b(attn_sinks, q, k4, v4, km4, vm4, yrec, h, wout, gattn, l2g, l2b,
                   l, batch, seq)
        if l + 1 < DEPTH:
            hm = _ffn(hm, f2g, f2u, f2d, l3g, l3b, l, N_META)
        h = _ffn(h, f2g, f2u, f2d, l3g, l3b, l, FFN_ROWS)
    return h.reshape(batch, seq, D_MODEL)
```
